```python
import jax, jax.numpy as jnp
from jax import lax
import numpy as np

D_MODEL = 1024
BATCH = 8
SEQ = 2048
DEPTH = 4
DEC_BATCH = 128
DEC_SEQ = 1
PAST_LEN = 2048
PAGE_SIZE = 128

N_MIXERS = 4
N_NSA_L = len(range(0, DEPTH, N_MIXERS))
N_RET_L = len(range(1, DEPTH, N_MIXERS))
N_POOL_L = len(range(2, DEPTH, N_MIXERS))
N_CONV_L = len(range(3, DEPTH, N_MIXERS))

NSA_HEADS = 16
NSA_HEAD_DIM = D_MODEL // NSA_HEADS
NSA_KV_HEADS = 4
NSA_GROUP = NSA_HEADS // NSA_KV_HEADS
NSA_KV_W = NSA_KV_HEADS * NSA_HEAD_DIM
NSA_Q_W = NSA_HEADS * NSA_HEAD_DIM
NSA_IN = NSA_Q_W + 6 * NSA_KV_W + 3 * NSA_HEADS
CMP_LEN = 32
CMP_STRIDE = 16
SEL_BLOCK = 64
TOP_N = 8
WINDOW = 512
Q_BLOCK = 128

RET_HEADS = 4
RET_DK = D_MODEL // RET_HEADS
RET_DV = 2 * D_MODEL // RET_HEADS
RET_IN = 2 * RET_HEADS * RET_DK + 2 * RET_HEADS * RET_DV
RET_CHUNK = 128
ROPE_BASE = 10000.0

POOL_WINDOWS = (2, 4, 8, 16)
POOL_GROUP = D_MODEL // len(POOL_WINDOWS)
POOL_MAX = max(POOL_WINDOWS)

CONV_W = 3

MOE_GROUPS = 4
MOE_EPG = 8
MOE_EXPERTS = MOE_GROUPS * MOE_EPG
MOE_FF = D_MODEL // 4
MOE_TOPK = 2

ALPHA = (2.0 * DEPTH) ** 0.25
BETA = (8.0 * DEPTH) ** -0.25
LN_EPS = 1e-5
NEG = -1e30
F32 = jnp.float32

kernel_name = 'nsa_retnet_pool_conv_hmoe_step'


def layer_norm(x, g, b):
    xf = x.astype(F32)
    mu = xf.mean(-1, keepdims=True)
    var = jnp.square(xf - mu).mean(-1, keepdims=True)
    return ((xf - mu) * lax.rsqrt(var + LN_EPS) * g + b).astype(x.dtype)


def masked_softmax(s, mask):
    p = jax.nn.softmax(jnp.where(mask, s, NEG), axis=-1)
    return jnp.where(mask, p, 0.0)


def nsa_project(x, w_in):
    B, T, _ = x.shape
    z = x @ w_in
    q = z[..., :NSA_Q_W].reshape(B, T, NSA_KV_HEADS, NSA_GROUP, NSA_HEAD_DIM)
    kv = z[..., NSA_Q_W:NSA_Q_W + 6 * NSA_KV_W].reshape(B, T, 3, 2, NSA_KV_HEADS, NSA_HEAD_DIM)
    gates = z[..., NSA_Q_W + 6 * NSA_KV_W:].reshape(B, T, NSA_KV_HEADS, NSA_GROUP, 3)
    return q, kv[:, :, 0], kv[:, :, 1], kv[:, :, 2], gates


def nsa_compress(rows, w_cmp, b_cmp):
    B, T = rows.shape[:2]
    n_cmp = (T - CMP_LEN) // CMP_STRIDE + 1
    r = CMP_LEN // CMP_STRIDE
    n_ch = n_cmp + r - 1
    ch = rows[:, :n_ch * CMP_STRIDE].reshape(B, n_ch, CMP_STRIDE, 2, NSA_KV_HEADS, NSA_HEAD_DIM)
    w = w_cmp.reshape(2, r, CMP_STRIDE, NSA_HEAD_DIM, NSA_HEAD_DIM)
    out = b_cmp[None, None, :, None, :]
    for j in range(r):
        out = out + jnp.einsum('bnschd,csde->bnche', ch[:, j:j + n_cmp], w[:, j])
    return out


def nsa_attend(q, q_pos, kvc, gather_sel, n_sel, kvw, kw_pos, gates):
    qf = q.astype(F32) * NSA_HEAD_DIM ** -0.5
    n_cmp = kvc.shape[1]
    s_c = jnp.einsum('bqhgd,bnhd->bhgqn', qf, kvc[:, :, 0].astype(F32))
    ok_c = (jnp.arange(n_cmp) * CMP_STRIDE + CMP_LEN - 1)[None, :] <= q_pos[:, None]
    p_c = masked_softmax(s_c, ok_c)
    o_c = jnp.einsum('bhgqn,bnhd->bqhgd', p_c, kvc[:, :, 1].astype(F32))
    r_sel = SEL_BLOCK // CMP_STRIDE
    imp = p_c.sum(axis=2)
    imp = jnp.pad(imp, ((0, 0), (0, 0), (0, 0), (0, n_sel * r_sel - n_cmp)))
    imp = imp.reshape(*imp.shape[:3], n_sel, r_sel).sum(-1)
    blk = jnp.arange(n_sel)[None, :]
    score = jnp.where(blk == (q_pos // SEL_BLOCK)[:, None], NSA_GROUP + 2.0,
                      jnp.where(blk == 0, NSA_GROUP + 1.0,
                                jnp.where(blk * SEL_BLOCK <= q_pos[:, None], imp, -1.0)))
    top_val, idx = lax.top_k(score, min(TOP_N, n_sel))
    sel_ok = top_val >= 0.0
    kvs = gather_sel(idx).astype(F32)
    s_s = jnp.einsum('bqhgd,bhqnkd->bhgqnk', qf, kvs[..., 0, :])
    kpos = idx[..., None] * SEL_BLOCK + jnp.arange(SEL_BLOCK)
    ok_s = (kpos <= q_pos[:, None, None]) & sel_ok[..., None]
    Bq, H, G, Tq, N, S = s_s.shape
    p_s = masked_softmax(s_s.reshape(Bq, H, G, Tq, N * S), ok_s.reshape(Bq, H, 1, Tq, N * S)).reshape(s_s.shape)
    o_s = jnp.einsum('bhgqnk,bhqnkd->bqhgd', p_s, kvs[..., 1, :])
    s_w = jnp.einsum('bqhgd,bkhd->bhgqk', qf, kvw[:, :, 0].astype(F32))
    dpos = q_pos[:, None] - kw_pos[None, :]
    ok_w = (dpos >= 0) & (dpos <= WINDOW) & (kw_pos[None, :] >= 0)
    p_w = masked_softmax(s_w, ok_w)
    o_w = jnp.einsum('bhgqk,bkhd->bqhgd', p_w, kvw[:, :, 1].astype(F32))
    g = jax.nn.sigmoid(gates.astype(F32))
    o = g[..., 0:1] * o_c + g[..., 1:2] * o_s + g[..., 2:3] * o_w
    return o.astype(q.dtype)


def nsa_prompt(x, w_in, w_cmp, b_cmp, w_o):
    B, T, _ = x.shape
    q, kv_c, kv_s, kv_w, gates = nsa_project(x, w_in)
    kvc = nsa_compress(kv_c, w_cmp, b_cmp)
    n_sel = -(-T // SEL_BLOCK)
    blocks = jnp.pad(kv_s, ((0, 0), (0, n_sel * SEL_BLOCK - T), (0, 0), (0, 0), (0, 0)))
    blocks = blocks.reshape(B, n_sel, SEL_BLOCK, 2, NSA_KV_HEADS, NSA_HEAD_DIM)
    bi = jnp.arange(B)[:, None, None, None]
    hi = jnp.arange(NSA_KV_HEADS)[None, :, None, None]

    def gather_sel(idx):
        return blocks[bi, idx, :, :, hi]

    kvw_pad = jnp.pad(kv_w, ((0, 0), (WINDOW, 0), (0, 0), (0, 0), (0, 0)))

    def q_block(i):
        s = i * Q_BLOCK
        qb = lax.dynamic_slice_in_dim(q, s, Q_BLOCK, axis=1)
        gb = lax.dynamic_slice_in_dim(gates, s, Q_BLOCK, axis=1)
        wb = lax.dynamic_slice_in_dim(kvw_pad, s, WINDOW + Q_BLOCK, axis=1)
        return nsa_attend(qb, s + jnp.arange(Q_BLOCK), kvc, gather_sel, n_sel, wb,
                          s - WINDOW + jnp.arange(WINDOW + Q_BLOCK), gb)

    o = lax.map(q_block, jnp.arange(T // Q_BLOCK))
    o = jnp.moveaxis(o, 0, 1).reshape(B, T, NSA_Q_W)
    return o @ w_o, kv_c, kv_s, kv_w[:, -min(WINDOW, T):]


def nsa_sample(x, cache_cmp, cache_sel, win_buf, page_table, w_in, w_cmp, b_cmp, w_o):
    B, T, _ = x.shape
    q, kv_c, kv_s, kv_w, gates = nsa_project(x, w_in)
    n_pages = page_table.shape[1]
    past = n_pages * PAGE_SIZE
    past_c = cache_cmp[page_table].reshape(B, past, 2, NSA_KV_HEADS, NSA_HEAD_DIM)
    kvc = nsa_compress(jnp.concatenate([past_c, kv_c], axis=1), w_cmp, b_cmp)
    n_sel = -(-(past + T) // SEL_BLOCK)
    n_past_blk = past // SEL_BLOCK
    n_new_blk = n_sel - n_past_blk
    new_blk = jnp.pad(kv_s, ((0, 0), (0, n_new_blk * SEL_BLOCK - T), (0, 0), (0, 0), (0, 0)))
    new_blk = new_blk.reshape(B, n_new_blk, SEL_BLOCK, 2, NSA_KV_HEADS, NSA_HEAD_DIM)
    sub = PAGE_SIZE // SEL_BLOCK
    sel_sub = cache_sel.reshape(-1, SEL_BLOCK, 2, NSA_KV_HEADS, NSA_HEAD_DIM)
    bi = jnp.arange(B)[:, None, None, None]
    hi = jnp.arange(NSA_KV_HEADS)[None, :, None, None]

    def gather_sel(idx):
        phys = page_table[bi, jnp.clip(idx // sub, 0, n_pages - 1)]
        from_past = sel_sub[phys * sub + idx % sub, :, :, hi]
        from_new = new_blk[bi, jnp.clip(idx - n_past_blk, 0, n_new_blk - 1), :, :, hi]
        return jnp.where((idx < n_past_blk)[..., None, None, None], from_past, from_new)

    kvw = jnp.concatenate([win_buf, kv_w], axis=1)
    w_buf = win_buf.shape[1]
    o = nsa_attend(q, past + jnp.arange(T), kvc, gather_sel, n_sel, kvw,
                   past - w_buf + jnp.arange(w_buf + T), gates)
    y = o.reshape(B, T, NSA_Q_W) @ w_o
    return y, kv_c, kv_s, kvw[:, -w_buf:]


def rotary(x, pos):
    half = x.shape[-1] // 2
    inv = ROPE_BASE ** (-jnp.linspace(0.0, 1.0, half, dtype=F32))
    ang = pos.astype(F32)[:, None] * inv[None, :]
    cos = jnp.cos(ang)[None, :, None, :]
    sin = jnp.sin(ang)[None, :, None, :]
    x1, x2 = x[..., :half], x[..., half:]
    return jnp.concatenate([x1 * cos - x2 * sin, x2 * cos + x1 * sin], axis=-1)


def ret_log_decay():
    return jnp.log(1.0 - 2.0 ** (-5.0 - jnp.arange(RET_HEADS, dtype=F32)))


def retention_chunk(S, q, k, v):
    C = q.shape[1]
    lg = ret_log_decay()
    i = jnp.arange(C, dtype=F32)
    diff = i[:, None] - i[None, :]
    dmask = jnp.where(diff >= 0, jnp.exp(lg[:, None, None] * jnp.maximum(diff, 0.0)), 0.0)
    inner = jnp.einsum('bihd,bjhd->bhij', q, k) * dmask
    o = jnp.einsum('bhij,bjhe->bihe', inner, v)
    cross = jnp.einsum('bihd,bhde->bihe', q, S) * jnp.exp((i[:, None] + 1.0) * lg[None, :])[None, :, :, None]
    kd = k * jnp.exp((C - 1.0 - i)[:, None] * lg[None, :])[None, :, :, None]
    S_new = jnp.exp(C * lg)[None, :, None, None] * S + jnp.einsum('bjhd,bjhe->bhde', kd, v)
    return S_new, o + cross


def retention_mixer(x, s0, pos0, w_in, gn_g, gn_b, w_o, chunk):
    B, T, _ = x.shape
    z = (x @ w_in).astype(F32)
    nqk = RET_HEADS * RET_DK
    nv = RET_HEADS * RET_DV
    pos = pos0 + jnp.arange(T)
    q = rotary(z[..., :nqk].reshape(B, T, RET_HEADS, RET_DK), pos)
    k = rotary(z[..., nqk:2 * nqk].reshape(B, T, RET_HEADS, RET_DK), pos) * RET_DK ** -0.5
    v = z[..., 2 * nqk:2 * nqk + nv].reshape(B, T, RET_HEADS, RET_DV)
    gate = z[..., 2 * nqk + nv:]
    n_ch = T // chunk

    def to_chunks(a):
        return jnp.moveaxis(a.reshape(B, n_ch, chunk, *a.shape[2:]), 1, 0)

    s_fin, o = lax.scan(lambda s, c: retention_chunk(s, c[0], c[1], c[2]), s0.astype(F32),
                        (to_chunks(q), to_chunks(k), to_chunks(v)))
    o = jnp.moveaxis(o, 0, 1).reshape(B, T, RET_HEADS, RET_DV)
    mu = o.mean(-1, keepdims=True)
    var = jnp.square(o - mu).mean(-1, keepdims=True)
    o = ((o - mu) * lax.rsqrt(var + LN_EPS)).reshape(B, T, nv) * gn_g + gn_b
    y = (jax.nn.silu(gate) * o).astype(x.dtype) @ w_o
    return y, s_fin.astype(s0.dtype)


def pool_mixer(x, prev, pos0, w_pool, scale):
    B, T, D = x.shape
    P = prev.shape[1]
    xh = jnp.concatenate([prev, x], axis=1).astype(F32)
    cs = jnp.concatenate([jnp.zeros((B, 1, D), F32), jnp.cumsum(xh, axis=1)], axis=1)
    pos = (pos0 + jnp.arange(T)).astype(F32)[None, :, None]
    outs = []
    for gi, w in enumerate(POOL_WINDOWS):
        sl = slice(gi * POOL_GROUP, (gi + 1) * POOL_GROUP)
        win_sum = cs[:, P + 1:P + 1 + T, sl] - cs[:, P + 1 - w:P + 1 - w + T, sl]
        pooled = win_sum / jnp.minimum(float(w), pos + 1.0) - xh[:, P:, sl]
        outs.append(jnp.einsum('btc,ce->bte', pooled, w_pool[gi].astype(F32)))
    y = jnp.concatenate(outs, axis=-1) * scale
    return y.astype(x.dtype), xh[:, -P:].astype(x.dtype)


def conv_mixer(x, prev, w_in, w_conv, w_out):
    B, T, D = x.shape
    z = x @ w_in
    bg, cg, h = z[..., :D], z[..., D:2 * D], z[..., 2 * D:]
    uh = jnp.concatenate([prev, cg * h], axis=1)
    conv = w_conv[0] * uh[:, 0:T]
    for j in range(1, CONV_W):
        conv = conv + w_conv[j] * uh[:, j:j + T]
    return (bg * conv) @ w_out, uh[:, -(CONV_W - 1):]


def hier_moe(x, w_rg, b_rg, w_re, b_re, w1, w3, w2):
    shp = x.shape
    xt = x.reshape(-1, D_MODEL)
    lg = (xt @ w_rg).astype(F32) + b_rg
    pg = jax.nn.softmax(lg, axis=-1)
    gsel = jnp.argmax(lg, axis=-1)
    pg_sel = jnp.take_along_axis(pg, gsel[:, None], axis=1)[:, 0]
    le = ((xt @ w_re).astype(F32) + b_re).reshape(-1, MOE_GROUPS, MOE_EPG)
    le_g = jnp.take_along_axis(le, gsel[:, None, None], axis=1)[:, 0]
    tv, ti = lax.top_k(le_g, MOE_TOPK)
    pe = jax.nn.softmax(tv, axis=-1) * pg_sel[:, None]
    eid = gsel[:, None] * MOE_EPG + ti
    comb = (jax.nn.one_hot(eid, MOE_EXPERTS, dtype=F32) * pe[..., None]).sum(1)
    h = jax.nn.silu(jnp.einsum('nd,edf->nef', xt, w1)) * jnp.einsum('nd,edf->nef', xt, w3)
    h = h * comb[:, :, None].astype(h.dtype)
    return jnp.einsum('nef,efd->nd', h, w2).reshape(shp)


def setup_inputs(seed: int = 0) -> dict:
    key = jax.random.key(seed)
    keys = iter(jax.random.split(key, 48))

    def nrm(shape, scale):
        return jax.random.normal(next(keys), shape, F32) * scale

    def gain(shape):
        return 1.0 + nrm(shape, 0.02)

    D = D_MODEL
    n_pages = PAST_LEN // PAGE_SIZE
    n_used = DEC_BATCH * n_pages
    n_phys = n_used + max(1, n_used // 4)
    w_buf = min(WINDOW, PAST_LEN)
    kv_pool = (n_phys, PAGE_SIZE, 2, NSA_KV_HEADS, NSA_HEAD_DIM)
    page_table = jax.random.permutation(next(keys), n_phys)[:n_used].reshape(DEC_BATCH, n_pages).astype(jnp.int32)
    return {
        'x_prompt': nrm((BATCH, SEQ, D), 1.0),
        'x_sample': nrm((DEC_BATCH, DEC_SEQ, D), 1.0),
        'cache_nsa_cmp': nrm((N_NSA_L,) + kv_pool, 1.0),
        'cache_nsa_sel': nrm((N_NSA_L,) + kv_pool, 1.0),
        'state_nsa_win': nrm((N_NSA_L, DEC_BATCH, w_buf, 2, NSA_KV_HEADS, NSA_HEAD_DIM), 1.0),
        'state_ret': nrm((N_RET_L, DEC_BATCH, RET_HEADS, RET_DK, RET_DV), 0.3),
        'state_pool': nrm((N_POOL_L, DEC_BATCH, POOL_MAX - 1, D), 1.0),
        'state_conv': nrm((N_CONV_L, DEC_BATCH, CONV_W - 1, D), 1.0),
        'page_table': page_table,
        'nsa_w_in': nrm((N_NSA_L, D, NSA_IN), D ** -0.5),
        'nsa_w_cmp': nrm((N_NSA_L, 2, CMP_LEN, NSA_HEAD_DIM, NSA_HEAD_DIM), (CMP_LEN * NSA_HEAD_DIM) ** -0.5),
        'nsa_b_cmp': nrm((N_NSA_L, 2, NSA_HEAD_DIM), 0.02),
        'nsa_w_o': nrm((N_NSA_L, NSA_Q_W, D), BETA * NSA_Q_W ** -0.5),
        'ret_w_in': nrm((N_RET_L, D, RET_IN), D ** -0.5),
        'ret_gn_g': gain((N_RET_L, RET_HEADS * RET_DV)),
        'ret_gn_b': nrm((N_RET_L, RET_HEADS * RET_DV), 0.02),
        'ret_w_o': nrm((N_RET_L, RET_HEADS * RET_DV, D), BETA * (RET_HEADS * RET_DV) ** -0.5),
        'pool_w': nrm((N_POOL_L, len(POOL_WINDOWS), POOL_GROUP, POOL_GROUP), BETA * POOL_GROUP ** -0.5),
        'pool_scale': gain((N_POOL_L, D)),
        'conv_w_in': nrm((N_CONV_L, D, 3 * D), D ** -0.5),
        'conv_w': nrm((N_CONV_L, CONV_W, D), CONV_W ** -0.5),
        'conv_w_out': nrm((N_CONV_L, D, D), BETA * D ** -0.5),
        'ln_g': gain((DEPTH, 2, D)),
        'ln_b': nrm((DEPTH, 2, D), 0.02),
        'moe_w_rg': nrm((DEPTH, D, MOE_GROUPS), D ** -0.5),
        'moe_b_rg': nrm((DEPTH, MOE_GROUPS), 0.01),
        'moe_w_re': nrm((DEPTH, D, MOE_EXPERTS), D ** -0.5),
        'moe_b_re': nrm((DEPTH, MOE_EXPERTS), 0.01),
        'moe_w1': nrm((DEPTH, MOE_EXPERTS, D, MOE_FF), D ** -0.5),
        'moe_w3': nrm((DEPTH, MOE_EXPERTS, D, MOE_FF), D ** -0.5),
        'moe_w2': nrm((DEPTH, MOE_EXPERTS, MOE_FF, D), BETA * MOE_FF ** -0.5),
    }


def reference(x_prompt, x_sample, cache_nsa_cmp, cache_nsa_sel, state_nsa_win, state_ret, state_pool,
              state_conv, page_table, nsa_w_in, nsa_w_cmp, nsa_b_cmp, nsa_w_o, ret_w_in, ret_gn_g, ret_gn_b,
              ret_w_o, pool_w, pool_scale, conv_w_in, conv_w, conv_w_out, ln_g, ln_b, moe_w_rg, moe_b_rg,
              moe_w_re, moe_b_re, moe_w1, moe_w3, moe_w2):
    xp, xs = x_prompt, x_sample
    bp = xp.shape[0]
    cmp_p, sel_p, win_p, ret_p, pool_p, conv_p = [], [], [], [], [], []
    cmp_s, sel_s, win_s, ret_s, pool_s, conv_s = [], [], [], [], [], []
    for i in range(DEPTH):
        kind, j = i % N_MIXERS, i // N_MIXERS
        if kind == 0:
            yp, a, b, c = nsa_prompt(xp, nsa_w_in[j], nsa_w_cmp[j], nsa_b_cmp[j], nsa_w_o[j])
            cmp_p.append(a); sel_p.append(b); win_p.append(c)
            ys, a, b, c = nsa_sample(xs, cache_nsa_cmp[j], cache_nsa_sel[j], state_nsa_win[j], page_table,
                                     nsa_w_in[j], nsa_w_cmp[j], nsa_b_cmp[j], nsa_w_o[j])
            cmp_s.append(a); sel_s.append(b); win_s.append(c)
        elif kind == 1:
            s0 = jnp.zeros((bp, RET_HEADS, RET_DK, RET_DV), xp.dtype)
            yp, a = retention_mixer(xp, s0, 0, ret_w_in[j], ret_gn_g[j], ret_gn_b[j], ret_w_o[j], RET_CHUNK)
            ret_p.append(a)
            ys, a = retention_mixer(xs, state_ret[j], PAST_LEN, ret_w_in[j], ret_gn_g[j], ret_gn_b[j],
                                    ret_w_o[j], xs.shape[1])
            ret_s.append(a)
        elif kind == 2:
            p0 = jnp.zeros((bp, POOL_MAX - 1, D_MODEL), xp.dtype)
            yp, a = pool_mixer(xp, p0, 0, pool_w[j], pool_scale[j])
            pool_p.append(a)
            ys, a = pool_mixer(xs, state_pool[j], PAST_LEN, pool_w[j], pool_scale[j])
            pool_s.append(a)
        else:
            c0 = jnp.zeros((bp, CONV_W - 1, D_MODEL), xp.dtype)
            yp, a = conv_mixer(xp, c0, conv_w_in[j], conv_w[j], conv_w_out[j])
            conv_p.append(a)
            ys, a = conv_mixer(xs, state_conv[j], conv_w_in[j], conv_w[j], conv_w_out[j])
            conv_s.append(a)
        xp = layer_norm(ALPHA * xp + yp, ln_g[i, 0], ln_b[i, 0])
        xs = layer_norm(ALPHA * xs + ys, ln_g[i, 0], ln_b[i, 0])
        moe_args = (moe_w_rg[i], moe_b_rg[i], moe_w_re[i], moe_b_re[i], moe_w1[i], moe_w3[i], moe_w2[i])
        xp = layer_norm(ALPHA * xp + hier_moe(xp, *moe_args), ln_g[i, 1], ln_b[i, 1])
        xs = layer_norm(ALPHA * xs + hier_moe(xs, *moe_args), ln_g[i, 1], ln_b[i, 1])
    new_cmp_p, new_sel_p, new_win_p = jnp.stack(cmp_p), jnp.stack(sel_p), jnp.stack(win_p)
    new_ret_p, new_pool_p, new_conv_p = jnp.stack(ret_p), jnp.stack(pool_p), jnp.stack(conv_p)
    new_cmp_s, new_sel_s, new_win_s = jnp.stack(cmp_s), jnp.stack(sel_s), jnp.stack(win_s)
    new_ret_s, new_pool_s, new_conv_s = jnp.stack(ret_s), jnp.stack(pool_s), jnp.stack(conv_s)
    return (xp, xs, new_cmp_p, new_sel_p, new_win_p, new_ret_p, new_pool_p, new_conv_p,
            new_cmp_s, new_sel_s, new_win_s, new_ret_s, new_pool_s, new_conv_s)
```

```python
import functools

import jax
import jax.numpy as jnp
import numpy as np
from jax import lax
from jax.experimental import pallas as pl
from jax.experimental.pallas import tpu as pltpu

F32, BF16, I32 = jnp.float32, jnp.bfloat16, jnp.int32

D_MODEL = 1024
DEPTH = 4
PAGE_SIZE = 128
NSA_HEADS = 16
NSA_HEAD_DIM = 64
NSA_KV_HEADS = 4
NSA_GROUP = NSA_HEADS // NSA_KV_HEADS
NSA_KV_W = NSA_KV_HEADS * NSA_HEAD_DIM
NSA_Q_W = NSA_HEADS * NSA_HEAD_DIM
CMP_LEN = 32
CMP_STRIDE = 16
SEL_BLOCK = 64
TOP_N = 8
WINDOW = 512
RET_HEADS = 4
RET_DK = D_MODEL // RET_HEADS
RET_DV = 2 * D_MODEL // RET_HEADS
RET_CHUNK = 128
ROPE_BASE = 10000.0
POOL_WINDOWS = (2, 4, 8, 16)
POOL_GROUP = D_MODEL // len(POOL_WINDOWS)
POOL_MAX = max(POOL_WINDOWS)
CONV_W = 3
MOE_GROUPS = 4
MOE_EPG = 8
MOE_EXPERTS = MOE_GROUPS * MOE_EPG
MOE_FF = D_MODEL // 4
ALPHA = (2.0 * DEPTH) ** 0.25
LN_EPS = 1e-5
NEG = -1e30

LANES = 128
Q_TILE = 128
KV_TILE = 256
VMEM_LIMIT = 56 * 1024 * 1024

_NT = (((1,), (1,)), ((), ()))
_TN = (((0,), (0,)), ((), ()))


def _params(*sem):
    return pltpu.CompilerParams(dimension_semantics=sem, vmem_limit_bytes=VMEM_LIMIT)


def _mm_body(x_ref, w_ref, *o_refs, splits, chunk):
    x = x_ref[...].astype(BF16)
    col = 0
    for o_ref, width in zip(o_refs, splits):
        for j in range(0, width, chunk):
            c = min(chunk, width - j)
            y = jnp.dot(x, w_ref[:, col + j:col + j + c], preferred_element_type=F32)
            o_ref[:, j:j + c] = y.astype(o_ref.dtype)
        col += width


def matmul_split(x, w, splits, dtypes, tm, chunk=512):
    m, k = x.shape
    n = w.shape[1]
    assert n == sum(splits) and m % tm == 0
    return pl.pallas_call(
        functools.partial(_mm_body, splits=tuple(splits), chunk=chunk),
        grid=(m // tm,),
        in_specs=[pl.BlockSpec((tm, k), lambda i: (i, 0)), pl.BlockSpec((k, n), lambda i: (0, 0))],
        out_specs=[pl.BlockSpec((tm, s), lambda i: (i, 0)) for s in splits],
        out_shape=[jax.ShapeDtypeStruct((m, s), d) for s, d in zip(splits, dtypes)],
        compiler_params=_params("arbitrary"),
        name="matmul_split",
    )(x, w)


def _layer_norm(v, g, b):
    mu = jnp.mean(v, axis=-1, keepdims=True)
    c = v - mu
    var = jnp.mean(c * c, axis=-1, keepdims=True)
    return c * lax.rsqrt(var + LN_EPS) * g + b


def _mm_res_ln_body(a_ref, w_ref, x_ref, g_ref, b_ref, o_ref):
    y = jnp.dot(a_ref[...].astype(BF16), w_ref[...], preferred_element_type=F32)
    o_ref[...] = _layer_norm(ALPHA * x_ref[...] + y, g_ref[...], b_ref[...])


def matmul_res_ln(a, w, x, g, b, tm):
    m, k = a.shape
    d = x.shape[1]
    return pl.pallas_call(
        _mm_res_ln_body,
        grid=(m // tm,),
        in_specs=[pl.BlockSpec((tm, k), lambda i: (i, 0)), pl.BlockSpec((k, d), lambda i: (0, 0)),
                  pl.BlockSpec((tm, d), lambda i: (i, 0)), pl.BlockSpec((1, d), lambda i: (0, 0)),
                  pl.BlockSpec((1, d), lambda i: (0, 0))],
        out_specs=pl.BlockSpec((tm, d), lambda i: (i, 0)),
        out_shape=jax.ShapeDtypeStruct((m, d), F32),
        compiler_params=_params("arbitrary"),
        name="matmul_res_ln",
    )(a, w, x, g.reshape(1, d), b.reshape(1, d))


def _cmp_weights(w_cmp):
    dh = NSA_HEAD_DIM
    w6 = w_cmp.reshape(2, 2, CMP_STRIDE // 2, 2, dh, dh)
    eye = jnp.eye(2, dtype=w_cmp.dtype)
    r = jnp.einsum("chpsde,xy->cpsxdhye", w6, eye)
    return r.reshape(2, CMP_STRIDE // 2, 4 * dh, 4 * dh).astype(BF16)


def _cmp_body(*refs):
    x_refs, (r_ref, b_ref, o_ref) = refs[:-3], refs[-3:]
    row_w = 2 * NSA_KV_W

    def cols(c0):
        return jnp.concatenate([x_ref[0, :, c0:c0 + LANES] for x_ref in x_refs], axis=0)

    for c in range(2):
        for hp in range(2):
            acc = None
            for sp in range(CMP_STRIDE // 2):
                c0 = (2 * sp) * row_w + c * NSA_KV_W + hp * LANES
                lhs = jnp.concatenate([cols(c0), cols(c0 + row_w)], axis=1).astype(BF16)
                part = jnp.dot(lhs, r_ref[c, sp], preferred_element_type=F32)
                acc = part if acc is None else acc + part
            lo, hi = acc[:, :LANES], acc[:, LANES:]
            nxt = pltpu.roll(hi, hi.shape[0] - 1, 0)
            o0 = c * NSA_KV_W + hp * LANES
            o_ref[0, :, o0:o0 + LANES] = lo + nxt + b_ref[:, o0:o0 + LANES]


def nsa_compress(rows, r_w, b_cmp):
    bsz, t, w = rows.shape
    n_ch = t // CMP_STRIDE
    x = rows.reshape(bsz, n_ch, CMP_STRIDE * w)
    bias = jnp.broadcast_to(b_cmp[:, None, :], (2, NSA_KV_HEADS, NSA_HEAD_DIM)).reshape(1, w)
    return pl.pallas_call(
        _cmp_body,
        grid=(bsz,),
        in_specs=[pl.BlockSpec((1, n_ch, CMP_STRIDE * w), lambda b: (b, 0, 0)),
                  pl.BlockSpec(r_w.shape, lambda b: (0, 0, 0, 0)),
                  pl.BlockSpec((1, w), lambda b: (0, 0))],
        out_specs=pl.BlockSpec((1, n_ch, w), lambda b: (b, 0, 0)),
        out_shape=jax.ShapeDtypeStruct((bsz, n_ch, w), F32),
        compiler_params=_params("arbitrary"),
        name="nsa_compress",
    )(x, r_w, bias)


def _cmp_paged_body(pt_ref, *refs):
    _cmp_body(*refs)


def nsa_compress_paged(cache, page_table, r_w, b_cmp):
    n_phys, page, w = cache.shape
    n, n_pages = page_table.shape
    per_page = page // CMP_STRIDE
    n_ch = n_pages * per_page
    x = cache.reshape(n_phys, per_page, CMP_STRIDE * w)
    bias = jnp.broadcast_to(b_cmp[:, None, :], (2, NSA_KV_HEADS, NSA_HEAD_DIM)).reshape(1, w)
    page_spec = lambda k: pl.BlockSpec((1, per_page, CMP_STRIDE * w), lambda b, pt: (pt[b * n_pages + k], 0, 0))
    return pl.pallas_call(
        _cmp_paged_body,
        grid_spec=pltpu.PrefetchScalarGridSpec(
            num_scalar_prefetch=1,
            grid=(n,),
            in_specs=[page_spec(k) for k in range(n_pages)]
            + [pl.BlockSpec(r_w.shape, lambda b, pt: (0, 0, 0, 0)), pl.BlockSpec((1, w), lambda b, pt: (0, 0))],
            out_specs=pl.BlockSpec((1, n_ch, w), lambda b, pt: (b, 0, 0))),
        out_shape=jax.ShapeDtypeStruct((n, n_ch, w), F32),
        compiler_params=_params("arbitrary"),
        name="nsa_compress_paged",
    )(page_table.reshape(-1), *([x] * n_pages), r_w, bias)


def _softmax_cols(s, ok):
    s = jnp.where(ok, s, NEG)
    m = jnp.max(s, axis=0, keepdims=True)
    p = jnp.where(ok, jnp.exp(s - m), 0.0)
    l = jnp.sum(p, axis=0, keepdims=True)
    return p / jnp.where(l > 0.0, l, 1.0)


def _nsa_prompt_body(q_ref, kc_ref, vct_ref, ks_ref, vst_ref, kw_ref, vwt_ref, g_ref, o_ref, imp_ref, selx_ref,
                     *, n_cmp, n_sel):
    i = pl.program_id(2)
    t0 = i * Q_TILE
    gq = NSA_GROUP * Q_TILE
    tq = t0 + lax.broadcasted_iota(I32, (1, Q_TILE), 1)
    tq4 = t0 + (lax.broadcasted_iota(I32, (1, gq), 1) & (Q_TILE - 1))
    q = q_ref[0, 0, 0]

    n_rows = kc_ref.shape[2]
    s = lax.dot_general(kc_ref[0, 0], q, _NT, preferred_element_type=F32)
    n_idx = lax.broadcasted_iota(I32, (n_rows, 1), 0)
    ok_c = (n_idx * CMP_STRIDE + (CMP_LEN - 1) <= tq4) & (n_idx < n_cmp)
    p_c = _softmax_cols(s, ok_c)
    o_c = jnp.dot(vct_ref[0, 0], p_c.astype(BF16), preferred_element_type=F32)

    imp = p_c[:, 0:Q_TILE]
    for g in range(1, NSA_GROUP):
        imp = imp + p_c[:, g * Q_TILE:(g + 1) * Q_TILE]
    imp_ref[...] = imp
    r_sel = SEL_BLOCK // CMP_STRIDE
    n_blk = n_rows // r_sel
    blk_imp = imp_ref[pl.ds(0, n_blk, stride=r_sel), :]
    for r in range(1, r_sel):
        blk_imp = blk_imp + imp_ref[pl.ds(r, n_blk, stride=r_sel), :]
    j_idx = lax.broadcasted_iota(I32, (n_blk, 1), 0)
    score = jnp.where(j_idx == tq // SEL_BLOCK, NSA_GROUP + 2.0,
                      jnp.where(j_idx == 0, NSA_GROUP + 1.0,
                                jnp.where(j_idx * SEL_BLOCK <= tq, blk_imp, -1.0)))
    score = jnp.where(j_idx < n_sel, score, -3.0)
    sel = jnp.zeros((n_blk, Q_TILE), F32)
    for _ in range(min(TOP_N, n_sel)):
        mx = jnp.max(score, axis=0, keepdims=True)
        first = jnp.min(jnp.where(score == mx, j_idx, n_blk), axis=0, keepdims=True)
        pick = j_idx == first
        sel = jnp.where(pick & (mx >= 0.0), 1.0, sel)
        score = jnp.where(pick, -2.0, score)
    for j in range(n_sel):
        selx_ref[j * SEL_BLOCK:(j + 1) * SEL_BLOCK, :] = jnp.broadcast_to(sel[j:j + 1, :], (SEL_BLOCK, Q_TILE))

    def attend(k_ref, vt_ref, lo, hi, mask_fn):
        def step(kt, carry):
            m, l, acc = carry
            k0 = pl.multiple_of(kt * KV_TILE, KV_TILE)
            s = lax.dot_general(k_ref[0, 0, pl.ds(k0, KV_TILE), :], q, _NT, preferred_element_type=F32)
            kpos = k0 + lax.broadcasted_iota(I32, (KV_TILE, 1), 0)
            ok1 = mask_fn(k0, kpos)
            ok = jnp.concatenate([ok1] * NSA_GROUP, axis=1)
            s = jnp.where(ok, s, NEG)
            mn = jnp.maximum(m, jnp.max(s, axis=0, keepdims=True))
            alpha = jnp.exp(m - mn)
            p = jnp.where(ok, jnp.exp(s - mn), 0.0)
            l = alpha * l + jnp.sum(p, axis=0, keepdims=True)
            acc = alpha * acc + jnp.dot(vt_ref[0, 0, kt], p.astype(BF16), preferred_element_type=F32)
            return mn, l, acc
        init = (jnp.full((1, gq), NEG, F32), jnp.zeros((1, gq), F32), jnp.zeros((NSA_HEAD_DIM, gq), F32))
        m, l, acc = lax.fori_loop(lo, hi, step, init)
        return acc / jnp.where(l > 0.0, l, 1.0)

    last = (t0 + Q_TILE - 1) // KV_TILE
    o_s = attend(ks_ref, vst_ref, 0, last + 1,
                 lambda k0, kpos: (selx_ref[pl.ds(k0, KV_TILE), :] > 0.0) & (kpos <= tq))
    first_w = jnp.maximum(t0 - WINDOW, 0) // KV_TILE
    o_w = attend(kw_ref, vwt_ref, first_w, last + 1,
                 lambda k0, kpos: (tq - kpos >= 0) & (tq - kpos <= WINDOW))

    gate = 1.0 / (1.0 + jnp.exp(-g_ref[0, 0]))
    for g in range(NSA_GROUP):
        sl = slice(g * Q_TILE, (g + 1) * Q_TILE)
        o = gate[g, 0:1, :] * o_c[:, sl] + gate[g, 1:2, :] * o_s[:, sl] + gate[g, 2:3, :] * o_w[:, sl]
        o_ref[0, 0, g] = o.astype(o_ref.dtype)


def _head_major(kv):
    bsz, t, _ = kv.shape
    kv = kv.astype(BF16).reshape(bsz, t, 2, NSA_KV_HEADS, NSA_HEAD_DIM)
    k = kv[:, :, 0].transpose(0, 2, 1, 3)
    vt = kv[:, :, 1].reshape(bsz, t // KV_TILE, KV_TILE, NSA_KV_HEADS, NSA_HEAD_DIM).transpose(0, 3, 1, 4, 2)
    return k, vt


def nsa_prompt_attention(q, kvc, kv_s, kv_w, gates, n_cmp):
    bsz, t, _ = q.shape
    nq = t // Q_TILE
    n_sel = -(-t // SEL_BLOCK)
    n_rows = kvc.shape[1]
    dh, kvh, grp = NSA_HEAD_DIM, NSA_KV_HEADS, NSA_GROUP
    qh = q.reshape(bsz, nq, Q_TILE, kvh, grp, dh).transpose(0, 3, 1, 4, 2, 5).reshape(bsz, kvh, nq, grp * Q_TILE, dh)
    kc5 = kvc.astype(BF16).reshape(bsz, n_rows, 2, kvh, dh)
    kc = kc5[:, :, 0].transpose(0, 2, 1, 3)
    vct = kc5[:, :, 1].transpose(0, 2, 3, 1)
    ks, vst = _head_major(kv_s)
    kw, vwt = _head_major(kv_w)
    gt = gates[:, :, :NSA_HEADS * 3].reshape(bsz, t, kvh, grp, 3).transpose(0, 2, 3, 4, 1)
    nkt = t // KV_TILE
    bh = lambda b, h, i: (b, h, 0, 0)
    out = pl.pallas_call(
        functools.partial(_nsa_prompt_body, n_cmp=n_cmp, n_sel=n_sel),
        grid=(bsz, kvh, nq),
        in_specs=[pl.BlockSpec((1, 1, 1, grp * Q_TILE, dh), lambda b, h, i: (b, h, i, 0, 0)),
                  pl.BlockSpec((1, 1, n_rows, dh), bh),
                  pl.BlockSpec((1, 1, dh, n_rows), bh),
                  pl.BlockSpec((1, 1, t, dh), bh),
                  pl.BlockSpec((1, 1, nkt, dh, KV_TILE), lambda b, h, i: (b, h, 0, 0, 0)),
                  pl.BlockSpec((1, 1, t, dh), bh),
                  pl.BlockSpec((1, 1, nkt, dh, KV_TILE), lambda b, h, i: (b, h, 0, 0, 0)),
                  pl.BlockSpec((1, 1, grp, 3, Q_TILE), lambda b, h, i: (b, h, 0, 0, i))],
        out_specs=pl.BlockSpec((1, 1, grp, dh, Q_TILE), lambda b, h, i: (b, h, 0, 0, i)),
        out_shape=jax.ShapeDtypeStruct((bsz, kvh, grp, dh, t), BF16),
        scratch_shapes=[pltpu.VMEM((n_rows, Q_TILE), F32), pltpu.VMEM((n_sel * SEL_BLOCK, Q_TILE), F32)],
        compiler_params=_params("arbitrary", "arbitrary", "arbitrary"),
        name="nsa_prompt_attention",
    )(qh, kc, vct, ks, vst, kw, vwt, gt)
    return out.transpose(0, 4, 1, 2, 3).reshape(bsz, t, kvh * grp * dh)


def _nsa_w_in_padded(w_in):
    wq = w_in[:, :NSA_Q_W] * NSA_HEAD_DIM ** -0.5
    pad = jnp.zeros((w_in.shape[0], LANES - 3 * NSA_HEADS), w_in.dtype)
    return jnp.concatenate([wq, w_in[:, NSA_Q_W:], pad], axis=1).astype(BF16)


_NSA_SPLITS = (NSA_Q_W, 2 * NSA_KV_W, 2 * NSA_KV_W, 2 * NSA_KV_W, LANES)
_NSA_DTYPES = (BF16, F32, F32, F32, F32)


def nsa_prompt_mixer(x, w_in_p, r_cmp, b_cmp):
    bsz, t, d = x.shape
    q, kv_c, kv_s, kv_w, gates = matmul_split(x.reshape(bsz * t, d), w_in_p, _NSA_SPLITS, _NSA_DTYPES, tm=min(512, bsz * t))
    r3 = lambda a: a.reshape(bsz, t, a.shape[-1])
    kv_c, kv_s, kv_w = r3(kv_c), r3(kv_s), r3(kv_w)
    n_cmp = (t - CMP_LEN) // CMP_STRIDE + 1
    kvc = nsa_compress(kv_c, r_cmp, b_cmp)
    o = nsa_prompt_attention(r3(q), kvc, kv_s, kv_w, r3(gates), n_cmp)
    return o.reshape(bsz * t, NSA_Q_W), kv_c, kv_s, kv_w


def _ret_log_decay():
    return jnp.log(1.0 - 2.0 ** (-5.0 - jnp.arange(RET_HEADS, dtype=F32)))


def _rope_tables(pos):
    half = RET_DK // 2
    inv = ROPE_BASE ** (-jnp.linspace(0.0, 1.0, half, dtype=F32))
    ang = pos.astype(F32)[:, None] * inv[None, :]
    return jnp.cos(ang), jnp.sin(ang)


def _rope(x, cos, sin):
    half = RET_DK // 2
    x1, x2 = x[:, :half], x[:, half:]
    return jnp.concatenate([x1 * cos - x2 * sin, x2 * cos + x1 * sin], axis=1)


def _group_norm_gate(o, gate, g, b):
    mu = jnp.mean(o, axis=-1, keepdims=True)
    c = o - mu
    var = jnp.mean(c * c, axis=-1, keepdims=True)
    on = c * lax.rsqrt(var + LN_EPS) * g + b
    return gate * (1.0 / (1.0 + jnp.exp(-gate))) * on


def _ret_prompt_body(q_ref, k_ref, v_ref, gt_ref, cos_ref, sin_ref, dm_ref, di_ref, dr_ref, dc_ref, gg_ref, gb_ref,
                     o_ref, s_ref, s_scr):
    c = pl.program_id(2)

    @pl.when(c == 0)
    def _():
        s_scr[...] = jnp.zeros_like(s_scr)

    cos, sin = cos_ref[...], sin_ref[...]
    q = _rope(q_ref[...], cos, sin)
    k = _rope(k_ref[...], cos, sin) * RET_DK ** -0.5
    qb, v = q.astype(BF16), v_ref[...]
    inner = lax.dot_general(qb, k.astype(BF16), _NT, preferred_element_type=F32) * dm_ref[0]
    o = jnp.dot(inner.astype(BF16), v, preferred_element_type=F32)
    s_old = s_scr[...]
    o = o + jnp.dot(qb, s_old.astype(BF16), preferred_element_type=F32) * di_ref[0]
    kd = (k * dr_ref[0]).astype(BF16)
    s_new = dc_ref[0] * s_old + lax.dot_general(kd, v, _TN, preferred_element_type=F32)
    s_scr[...] = s_new

    @pl.when(c == pl.num_programs(2) - 1)
    def _():
        s_ref[0, 0] = s_new

    o_ref[...] = _group_norm_gate(o, gt_ref[...], gg_ref[...], gb_ref[...]).astype(o_ref.dtype)


def retention_prompt(q, k, v, gate, gn_g, gn_b, bsz, t):
    ch = RET_CHUNK
    n_ch = t // ch
    lg = _ret_log_decay()
    i = jnp.arange(ch, dtype=F32)
    diff = i[:, None] - i[None, :]
    dmask = jnp.where(diff >= 0, jnp.exp(lg[:, None, None] * jnp.maximum(diff, 0.0)), 0.0)
    d_in = jnp.exp((i[None, :] + 1.0) * lg[:, None])[:, :, None]
    d_rev = jnp.exp((ch - 1.0 - i)[None, :] * lg[:, None])[:, :, None]
    d_c = jnp.exp(ch * lg)[:, None, None]
    cos, sin = _rope_tables(jnp.arange(t))
    half = RET_DK // 2
    row = lambda b, h, c: b * n_ch + c
    nh = RET_HEADS
    return pl.pallas_call(
        _ret_prompt_body,
        grid=(bsz, nh, n_ch),
        in_specs=[pl.BlockSpec((ch, RET_DK), lambda b, h, c: (row(b, h, c), h)),
                  pl.BlockSpec((ch, RET_DK), lambda b, h, c: (row(b, h, c), h)),
                  pl.BlockSpec((ch, RET_DV), lambda b, h, c: (row(b, h, c), h)),
                  pl.BlockSpec((ch, RET_DV), lambda b, h, c: (row(b, h, c), h)),
                  pl.BlockSpec((ch, half), lambda b, h, c: (c, 0)),
                  pl.BlockSpec((ch, half), lambda b, h, c: (c, 0)),
                  pl.BlockSpec((1, ch, ch), lambda b, h, c: (h, 0, 0)),
                  pl.BlockSpec((1, ch, 1), lambda b, h, c: (h, 0, 0)),
                  pl.BlockSpec((1, ch, 1), lambda b, h, c: (h, 0, 0)),
                  pl.BlockSpec((1, 1, 1), lambda b, h, c: (h, 0, 0)),
                  pl.BlockSpec((1, RET_DV), lambda b, h, c: (0, h)),
                  pl.BlockSpec((1, RET_DV), lambda b, h, c: (0, h))],
        out_specs=[pl.BlockSpec((ch, RET_DV), lambda b, h, c: (row(b, h, c), h)),
                   pl.BlockSpec((1, 1, RET_DK, RET_DV), lambda b, h, c: (b, h, 0, 0))],
        out_shape=[jax.ShapeDtypeStruct((bsz * t, nh * RET_DV), BF16),
                   jax.ShapeDtypeStruct((bsz, nh, RET_DK, RET_DV), F32)],
        scratch_shapes=[pltpu.VMEM((RET_DK, RET_DV), F32)],
        compiler_params=_params("arbitrary", "arbitrary", "arbitrary"),
        name="retention_prompt",
    )(q, k, v, gate, cos, sin, dmask, d_in, d_rev, d_c, gn_g.reshape(1, -1), gn_b.reshape(1, -1))


_RET_SPLITS = (RET_HEADS * RET_DK, RET_HEADS * RET_DK, RET_HEADS * RET_DV, RET_HEADS * RET_DV)
_RET_DTYPES = (F32, F32, BF16, F32)


def _ret_sample_body(q_ref, k_ref, v_ref, gt_ref, cos_ref, sin_ref, dec_ref, gg_ref, gb_ref, s_in_ref,
                     o_ref, s_out_ref):
    cos, sin = cos_ref[...], sin_ref[...]
    row0 = lax.broadcasted_iota(I32, (8, 1), 0) == 0
    for h in range(RET_HEADS):
        qh = _rope(q_ref[0, :, h * RET_DK:(h + 1) * RET_DK], cos, sin)
        kh = _rope(k_ref[0, :, h * RET_DK:(h + 1) * RET_DK], cos, sin) * RET_DK ** -0.5
        vh = v_ref[0, :, h * RET_DV:(h + 1) * RET_DV]
        qb, kb = qh.astype(BF16), kh.astype(BF16)
        dec = dec_ref[h]
        s_old = s_in_ref[0, h]
        q8 = jnp.broadcast_to(qb, (8, RET_DK))
        cross = jnp.dot(q8, s_old.astype(BF16), preferred_element_type=F32)[0:1] * dec
        inner = jnp.sum(qb.astype(F32) * kb.astype(F32), axis=1, keepdims=True)
        o = inner.astype(BF16).astype(F32) * vh.astype(F32) + cross
        k8 = jnp.where(row0, jnp.broadcast_to(kb.astype(F32), (8, RET_DK)), 0.0).astype(BF16)
        v8 = jnp.broadcast_to(vh, (8, RET_DV))
        s_out_ref[0, h] = dec * s_old + lax.dot_general(k8, v8, _TN, preferred_element_type=F32)
        sl = slice(h * RET_DV, (h + 1) * RET_DV)
        o_ref[0, :, sl] = _group_norm_gate(o, gt_ref[0, :, sl], gg_ref[:, sl], gb_ref[:, sl]).astype(o_ref.dtype)


def retention_sample(q, k, v, gate, gn_g, gn_b, state, pos0):
    n = q.shape[0]
    cos, sin = _rope_tables(jnp.full((1,), pos0))
    dec = jnp.exp(_ret_log_decay())
    r3 = lambda a: a.reshape(n, 1, a.shape[-1])
    w_qk, w_v = RET_HEADS * RET_DK, RET_HEADS * RET_DV
    half = RET_DK // 2
    vec = lambda w: pl.BlockSpec((1, 1, w), lambda b: (b, 0, 0))
    og, s_new = pl.pallas_call(
        _ret_sample_body,
        grid=(n,),
        in_specs=[vec(w_qk), vec(w_qk), vec(w_v), vec(w_v),
                  pl.BlockSpec((1, half), lambda b: (0, 0)), pl.BlockSpec((1, half), lambda b: (0, 0)),
                  pl.BlockSpec(memory_space=pltpu.SMEM),
                  pl.BlockSpec((1, w_v), lambda b: (0, 0)), pl.BlockSpec((1, w_v), lambda b: (0, 0)),
                  pl.BlockSpec((1, RET_HEADS, RET_DK, RET_DV), lambda b: (b, 0, 0, 0))],
        out_specs=[vec(w_v), pl.BlockSpec((1, RET_HEADS, RET_DK, RET_DV), lambda b: (b, 0, 0, 0))],
        out_shape=[jax.ShapeDtypeStruct((n, 1, w_v), BF16), jax.ShapeDtypeStruct(state.shape, F32)],
        compiler_params=_params("arbitrary"),
        name="retention_sample",
    )(r3(q), r3(k), r3(v), r3(gate), cos, sin, dec, gn_g.reshape(1, -1), gn_b.reshape(1, -1), state)
    return og.reshape(n, w_v), s_new


POOL_HALO = 16


def _pool_mix(win_sum_fn, x, pos, w_ref, sc_ref, g_ref, b_ref):
    ys = []
    for gi, w in enumerate(POOL_WINDOWS):
        sl = slice(gi * POOL_GROUP, (gi + 1) * POOL_GROUP)
        pooled = win_sum_fn(gi, w) / jnp.minimum(float(w), pos + 1.0) - x[:, sl]
        ys.append(jnp.dot(pooled.astype(BF16), w_ref[gi], preferred_element_type=F32))
    y = jnp.concatenate(ys, axis=1) * sc_ref[...]
    return _layer_norm(ALPHA * x + y, g_ref[...], b_ref[...])


def _pool_prompt_body(x_ref, w_ref, sc_ref, g_ref, b_ref, o_ref, xh_scr, *, tt):
    j = pl.program_id(1)

    @pl.when(j == 0)
    def _():
        xh_scr[0:POOL_HALO, :] = jnp.zeros((POOL_HALO, D_MODEL), F32)

    x = x_ref[0]
    xh_scr[POOL_HALO:, :] = x
    pos = (j * tt + lax.broadcasted_iota(I32, (tt, 1), 0)).astype(F32)

    def win_sum(gi, w):
        c0 = gi * POOL_GROUP
        acc = x[:, c0:c0 + POOL_GROUP]
        for u in range(1, w):
            acc = acc + xh_scr[POOL_HALO - u:POOL_HALO - u + tt, c0:c0 + POOL_GROUP]
        return acc

    o_ref[0] = _pool_mix(win_sum, x, pos, w_ref, sc_ref, g_ref, b_ref)
    xh_scr[0:POOL_HALO, :] = x[tt - POOL_HALO:, :]


def pool_prompt(x, w_pool, scale, g, b, tt=256):
    bsz, t, d = x.shape
    tt = min(tt, t)
    row = lambda b_, j: (0, 0)
    return pl.pallas_call(
        functools.partial(_pool_prompt_body, tt=tt),
        grid=(bsz, t // tt),
        in_specs=[pl.BlockSpec((1, tt, d), lambda b_, j: (b_, j, 0)),
                  pl.BlockSpec(w_pool.shape, lambda b_, j: (0, 0, 0)),
                  pl.BlockSpec((1, d), row), pl.BlockSpec((1, d), row), pl.BlockSpec((1, d), row)],
        out_specs=pl.BlockSpec((1, tt, d), lambda b_, j: (b_, j, 0)),
        out_shape=jax.ShapeDtypeStruct(x.shape, F32),
        scratch_shapes=[pltpu.VMEM((POOL_HALO + tt, d), F32)],
        compiler_params=_params("arbitrary", "arbitrary"),
        name="pool_prompt",
    )(x, w_pool, scale.reshape(1, d), g.reshape(1, d), b.reshape(1, d))


def _pool_sample_body(x_ref, st_ref, w_ref, sc_ref, g_ref, b_ref, o_ref, *, pos0):
    x = x_ref[...]
    n_hist = st_ref.shape[0]

    def win_sum(gi, w):
        c0 = gi * POOL_GROUP
        acc = x[:, c0:c0 + POOL_GROUP]
        for u in range(1, w):
            acc = acc + st_ref[n_hist - u, :, c0:c0 + POOL_GROUP]
        return acc

    pos = jnp.full((x.shape[0], 1), float(pos0), F32)
    o_ref[...] = _pool_mix(win_sum, x, pos, w_ref, sc_ref, g_ref, b_ref)


def pool_sample(x, hist, w_pool, scale, g, b, pos0):
    n, d = x.shape
    whole = lambda a: pl.BlockSpec(a.shape, lambda i: (0,) * a.ndim)
    args = (x, hist, w_pool, scale.reshape(1, d), g.reshape(1, d), b.reshape(1, d))
    return pl.pallas_call(
        functools.partial(_pool_sample_body, pos0=pos0),
        grid=(1,),
        in_specs=[whole(a) for a in args],
        out_specs=pl.BlockSpec((n, d), lambda i: (0, 0)),
        out_shape=jax.ShapeDtypeStruct((n, d), F32),
        compiler_params=_params("arbitrary"),
        name="pool_sample",
    )(*args)


CONV_HALO = 8


def _conv_prompt_body(bg_ref, cg_ref, h_ref, wc_ref, a_ref, tail_ref, uh_scr, *, tt):
    j = pl.program_id(1)

    @pl.when(j == 0)
    def _():
        uh_scr[0:CONV_HALO, :] = jnp.zeros((CONV_HALO, D_MODEL), F32)

    u = cg_ref[...] * h_ref[...]
    uh_scr[CONV_HALO:, :] = u
    conv = wc_ref[CONV_W - 1:CONV_W, :] * u
    for jj in range(CONV_W - 1):
        back = CONV_W - 1 - jj
        conv = conv + wc_ref[jj:jj + 1, :] * uh_scr[CONV_HALO - back:CONV_HALO - back + tt, :]
    a_ref[...] = (bg_ref[...] * conv).astype(a_ref.dtype)
    uh_scr[0:CONV_HALO, :] = u[tt - CONV_HALO:, :]
    tail_ref[0] = u[tt - CONV_HALO:, :]


def conv_prompt(bg, cg, h, w_conv, bsz, t, tt=256):
    d = bg.shape[1]
    tt = min(tt, t)
    nt = t // tt
    blk = pl.BlockSpec((tt, d), lambda b_, j: (b_ * nt + j, 0))
    return pl.pallas_call(
        functools.partial(_conv_prompt_body, tt=tt),
        grid=(bsz, nt),
        in_specs=[blk, blk, blk, pl.BlockSpec((CONV_W, d), lambda b_, j: (0, 0))],
        out_specs=[blk, pl.BlockSpec((1, CONV_HALO, d), lambda b_, j: (b_, 0, 0))],
        out_shape=[jax.ShapeDtypeStruct((bsz * t, d), BF16), jax.ShapeDtypeStruct((bsz, CONV_HALO, d), F32)],
        scratch_shapes=[pltpu.VMEM((CONV_HALO + tt, d), F32)],
        compiler_params=_params("arbitrary", "arbitrary"),
        name="conv_prompt",
    )(bg, cg, h, w_conv)


def _conv_sample_body(bg_ref, cg_ref, h_ref, prev_ref, wc_ref, a_ref, u_ref):
    u = cg_ref[...] * h_ref[...]
    conv = wc_ref[CONV_W - 1:CONV_W, :] * u
    for jj in range(CONV_W - 1):
        conv = conv + wc_ref[jj:jj + 1, :] * prev_ref[jj]
    a_ref[...] = (bg_ref[...] * conv).astype(a_ref.dtype)
    u_ref[...] = u


def conv_sample(bg, cg, h, prev, w_conv):
    n, d = bg.shape
    whole = lambda a: pl.BlockSpec(a.shape, lambda i: (0,) * a.ndim)
    args = (bg, cg, h, prev, w_conv)
    return pl.pallas_call(
        _conv_sample_body,
        grid=(1,),
        in_specs=[whole(a) for a in args],
        out_specs=[pl.BlockSpec((n, d), lambda i: (0, 0)), pl.BlockSpec((n, d), lambda i: (0, 0))],
        out_shape=[jax.ShapeDtypeStruct((n, d), BF16), jax.ShapeDtypeStruct((n, d), F32)],
        compiler_params=_params("arbitrary"),
        name="conv_sample",
    )(*args)


_CONV_SPLITS = (D_MODEL, D_MODEL, D_MODEL)
_CONV_DTYPES = (F32, F32, F32)


ROUTER_LANES = LANES
EXPERT_LANE0 = MOE_GROUPS


def _router_body(x_ref, w_ref, b_ref, o_ref):
    logits = jnp.dot(x_ref[...], w_ref[...], preferred_element_type=F32, precision=lax.Precision.HIGHEST) + b_ref[...]
    lane = lax.broadcasted_iota(I32, logits.shape, 1)
    big = ROUTER_LANES

    def top1(mask):
        v = jnp.max(jnp.where(mask, logits, -jnp.inf), axis=1, keepdims=True)
        i = jnp.min(jnp.where(mask & (logits == v), lane, big), axis=1, keepdims=True)
        return v, i

    is_g = lane < MOE_GROUPS
    vg, gsel = top1(is_g)
    pg_sel = 1.0 / jnp.sum(jnp.where(is_g, jnp.exp(logits - vg), 0.0), axis=1, keepdims=True)
    e0 = EXPERT_LANE0 + gsel * MOE_EPG
    is_e = (lane >= e0) & (lane < e0 + MOE_EPG)
    v1, i1 = top1(is_e)
    v2, i2 = top1(is_e & (lane != i1))
    r = jnp.exp(v2 - v1)
    pe1 = pg_sel / (1.0 + r)
    pe2 = pg_sel * r / (1.0 + r)
    comb = jnp.where(lane == i1, pe1, 0.0) + jnp.where(lane == i2, pe2, 0.0)
    o_ref[...] = jnp.where(lane == 0, gsel.astype(F32), comb)


def moe_router(x, w_r, b_r, tm):
    m, d = x.shape
    return pl.pallas_call(
        _router_body,
        grid=(m // tm,),
        in_specs=[pl.BlockSpec((tm, d), lambda i: (i, 0)), pl.BlockSpec((d, ROUTER_LANES), lambda i: (0, 0)),
                  pl.BlockSpec((1, ROUTER_LANES), lambda i: (0, 0))],
        out_specs=pl.BlockSpec((tm, ROUTER_LANES), lambda i: (i, 0)),
        out_shape=jax.ShapeDtypeStruct((m, ROUTER_LANES), F32),
        compiler_params=_params("arbitrary"),
        name="moe_router",
    )(x, w_r, b_r)


def _router_weights(w_rg, b_rg, w_re, b_re):
    d = w_rg.shape[0]
    pad = ROUTER_LANES - MOE_GROUPS - MOE_EXPERTS
    w = jnp.concatenate([w_rg, w_re, jnp.zeros((d, pad), F32)], axis=1)
    b = jnp.concatenate([b_rg, b_re, jnp.zeros((pad,), F32)]).reshape(1, ROUTER_LANES)
    return w, b


def _moe_ffn_body(gid_ref, x_ref, c_ref, w1_ref, w3_ref, w2_ref, g_ref, b_ref, o_ref):
    x = x_ref[...]
    xb = x.astype(BF16)
    comb = c_ref[...]
    y = jnp.zeros(x.shape, F32)
    for e in range(MOE_EPG):
        a = jnp.dot(xb, w1_ref[e], preferred_element_type=F32)
        gte = jnp.dot(xb, w3_ref[e], preferred_element_type=F32)
        hcol = a * (1.0 / (1.0 + jnp.exp(-a))) * gte * comb[:, e:e + 1]
        y = y + jnp.dot(hcol.astype(BF16), w2_ref[e], preferred_element_type=F32)
    o_ref[...] = _layer_norm(ALPHA * x + y, g_ref[...], b_ref[...])


def moe_layer(x, router_w, router_b, w1, w3, w2, g, b, tm):
    m, d = x.shape
    r = moe_router(x, router_w, router_b, tm=min(512, m))
    gid = r[:, 0].astype(I32)
    comb_all = r[:, EXPERT_LANE0:EXPERT_LANE0 + MOE_EXPERTS].reshape(m, MOE_GROUPS, MOE_EPG)
    comb = jnp.take_along_axis(comb_all, gid[:, None, None], axis=1)[:, 0]
    onehot = (gid[:, None] == jnp.arange(MOE_GROUPS)[None, :]).astype(I32)
    rank = jnp.take_along_axis(jnp.cumsum(onehot, axis=0) - onehot, gid[:, None], axis=1)[:, 0]
    counts = onehot.sum(axis=0)
    tiles = (counts + tm - 1) // tm
    tile_end = jnp.cumsum(tiles)
    start = (tile_end - tiles) * tm
    dest = start[gid] + rank
    n_tiles = m // tm + MOE_GROUPS
    mp = n_tiles * tm
    src = jnp.zeros((mp,), I32).at[dest].set(jnp.arange(m, dtype=I32))
    live = jnp.zeros((mp,), F32).at[dest].set(1.0)
    xs = jnp.take(x, src, axis=0)
    cs = jnp.take(comb, src, axis=0) * live[:, None]
    tile_gid = jnp.minimum(jnp.searchsorted(tile_end, jnp.arange(n_tiles, dtype=I32), side="right"),
                           MOE_GROUPS - 1).astype(I32)
    wspec = lambda shp: pl.BlockSpec(shp, lambda i, gid_: (gid_[i], 0, 0))
    out = pl.pallas_call(
        _moe_ffn_body,
        grid_spec=pltpu.PrefetchScalarGridSpec(
            num_scalar_prefetch=1,
            grid=(n_tiles,),
            in_specs=[pl.BlockSpec((tm, d), lambda i, gid_: (i, 0)),
                      pl.BlockSpec((tm, MOE_EPG), lambda i, gid_: (i, 0)),
                      wspec((MOE_EPG, d, MOE_FF)), wspec((MOE_EPG, d, MOE_FF)), wspec((MOE_EPG, MOE_FF, d)),
                      pl.BlockSpec((1, d), lambda i, gid_: (0, 0)), pl.BlockSpec((1, d), lambda i, gid_: (0, 0))],
            out_specs=pl.BlockSpec((tm, d), lambda i, gid_: (i, 0))),
        out_shape=jax.ShapeDtypeStruct((mp, d), F32),
        compiler_params=_params("arbitrary"),
        name="moe_ffn",
    )(tile_gid, xs, cs, w1, w3, w2, g.reshape(1, d), b.reshape(1, d))
    return jnp.take(out, dest, axis=0)


NSA_ROWS = NSA_HEADS
SEL_SAMPLES = 8


def _softmax_rows(s, ok):
    s = jnp.where(ok, s, NEG)
    m = jnp.max(s, axis=1, keepdims=True)
    p = jnp.where(ok, jnp.exp(s - m), 0.0)
    l = jnp.sum(p, axis=1, keepdims=True)
    return p / jnp.where(l > 0.0, l, 1.0)


def _nsa_sample_select_body(qx_ref, kvc_ref, pool_ref, idx_ref, *, n_cmp, n_sel, q_pos):
    n_rows = kvc_ref.shape[1]
    lane = lax.broadcasted_iota(I32, (NSA_ROWS, n_rows), 1)
    ok_c = (lane * CMP_STRIDE + (CMP_LEN - 1) <= q_pos) & (lane < n_cmp)
    for s_i in range(SEL_SAMPLES):
        kc = kvc_ref[s_i, :, 0:NSA_KV_W].astype(BF16)
        s = lax.dot_general(qx_ref[s_i], kc, _NT, preferred_element_type=F32)
        p = _softmax_rows(s, ok_c)
        imp = p
        for g in range(1, NSA_GROUP):
            imp = imp + pltpu.roll(p, g * NSA_KV_HEADS, 0)
        blk = jnp.dot(imp, pool_ref[...], preferred_element_type=F32, precision=lax.Precision.HIGHEST)
        score = jnp.where(lane == q_pos // SEL_BLOCK, NSA_GROUP + 2.0,
                          jnp.where(lane == 0, NSA_GROUP + 1.0,
                                    jnp.where(lane * SEL_BLOCK <= q_pos, blk, -1.0)))
        score = jnp.where(lane < n_sel, score, -3.0)
        idx = jnp.full(score.shape, -1, I32)
        for k in range(min(TOP_N, n_sel)):
            mx = jnp.max(score, axis=1, keepdims=True)
            first = jnp.min(jnp.where(score == mx, lane, n_rows), axis=1, keepdims=True)
            idx = jnp.where(lane == k, jnp.where(mx >= 0.0, first, -1), idx)
            score = jnp.where(lane == first, -2.0, score)
        idx_ref[s_i] = idx[0:8, :]


def nsa_sample_select(qx, kvc, n_cmp, n_sel, q_pos):
    n, n_rows, w = kvc.shape
    r_sel = SEL_BLOCK // CMP_STRIDE
    pool = (jnp.arange(n_rows)[:, None] // r_sel == jnp.arange(n_rows)[None, :]).astype(F32)
    return pl.pallas_call(
        functools.partial(_nsa_sample_select_body, n_cmp=n_cmp, n_sel=n_sel, q_pos=q_pos),
        grid=(n // SEL_SAMPLES,),
        in_specs=[pl.BlockSpec((SEL_SAMPLES, NSA_ROWS, NSA_KV_W), lambda i: (i, 0, 0)),
                  pl.BlockSpec((SEL_SAMPLES, n_rows, w), lambda i: (i, 0, 0)),
                  pl.BlockSpec((n_rows, n_rows), lambda i: (0, 0))],
        out_specs=pl.BlockSpec((SEL_SAMPLES, 8, n_rows), lambda i: (i, 0, 0)),
        out_shape=jax.ShapeDtypeStruct((n, 8, n_rows), I32),
        compiler_params=_params("arbitrary"),
        name="nsa_sample_select",
    )(qx, kvc, pool)


def _nsa_sample_attend_body(sub_ref, ok_ref, qx_ref, kvc_ref, win_ref, ksn_ref, kwn_ref, g_ref, *rest, n_cmp, q_pos):
    blk_refs, o_ref = rest[:-1], rest[-1]
    b = pl.program_id(0)
    kw = NSA_KV_W
    q = qx_ref[0]
    qf = q.astype(F32)
    row_h = lax.broadcasted_iota(I32, (NSA_ROWS, 1), 0) % NSA_KV_HEADS
    rnd = lambda a: a.astype(BF16).astype(F32)

    def with_new_key(s, ok, k_new, v_old, v_new):
        s_new = jnp.sum(qf * rnd(k_new), axis=1, keepdims=True)
        if ok is not None:
            s = jnp.where(ok, s, NEG)
        m = jnp.maximum(jnp.max(s, axis=1, keepdims=True), s_new)
        p = jnp.exp(s - m) if ok is None else jnp.where(ok, jnp.exp(s - m), 0.0)
        p_new = jnp.exp(s_new - m)
        l = jnp.sum(p, axis=1, keepdims=True) + p_new
        return (jnp.dot(p.astype(BF16), v_old, preferred_element_type=F32) + rnd(p_new) * rnd(v_new)) / l

    n_rows = kvc_ref.shape[1]
    lane_c = lax.broadcasted_iota(I32, (NSA_ROWS, n_rows), 1)
    ok_c = (lane_c * CMP_STRIDE + (CMP_LEN - 1) <= q_pos) & (lane_c < n_cmp)
    s = lax.dot_general(q, kvc_ref[0, :, 0:kw].astype(BF16), _NT, preferred_element_type=F32)
    o_c = jnp.dot(_softmax_rows(s, ok_c).astype(BF16), kvc_ref[0, :, kw:].astype(BF16), preferred_element_type=F32)

    n_win = win_ref.shape[1]
    s = lax.dot_general(q, win_ref[0, :, 0:kw].astype(BF16), _NT, preferred_element_type=F32)
    o_w = with_new_key(s, None, kwn_ref[0, :, 0:kw], win_ref[0, :, kw:].astype(BF16), kwn_ref[0, :, kw:])

    n_top = len(blk_refs) // NSA_KV_HEADS
    lane_s = lax.broadcasted_iota(I32, (NSA_ROWS, n_top * SEL_BLOCK), 1) // SEL_BLOCK
    o_s = jnp.zeros((NSA_ROWS, kw), F32)
    for h in range(NSA_KV_HEADS):
        refs = blk_refs[h * n_top:(h + 1) * n_top]
        k_sel = jnp.concatenate([r[0, :, 0:kw] for r in refs], axis=0).astype(BF16)
        v_sel = jnp.concatenate([r[0, :, kw:] for r in refs], axis=0).astype(BF16)
        live = jnp.zeros(lane_s.shape, I32)
        for k in range(n_top):
            live = jnp.where(lane_s == k, ok_ref[(b * NSA_KV_HEADS + h) * n_top + k], live)
        s = lax.dot_general(q, k_sel, _NT, preferred_element_type=F32)
        o_h = with_new_key(s, live > 0, ksn_ref[0, :, 0:kw], v_sel, ksn_ref[0, :, kw:])
        o_s = jnp.where(row_h == h, o_h, o_s)

    gate = 1.0 / (1.0 + jnp.exp(-g_ref[0]))
    o = gate[:, 0:1] * o_c + gate[:, 1:2] * o_s + gate[:, 2:3] * o_w
    lane_h = lax.broadcasted_iota(I32, (NSA_ROWS, kw), 1) // NSA_HEAD_DIM
    o = jnp.where(lane_h == row_h, o, 0.0)
    out = o[:, 0:NSA_HEAD_DIM]
    for h in range(1, NSA_KV_HEADS):
        out = out + o[:, h * NSA_HEAD_DIM:(h + 1) * NSA_HEAD_DIM]
    o_ref[0] = out.astype(o_ref.dtype)


def nsa_sample_mixer(x, cache_cmp, cache_sel, win_buf, page_table, w_in_p, r_cmp, b_cmp):
    n, d = x.shape
    n_pages = page_table.shape[1]
    past = n_pages * PAGE_SIZE
    dh, kvh, grp = NSA_HEAD_DIM, NSA_KV_HEADS, NSA_GROUP
    q, kv_c, kv_s, kv_w, gates = matmul_split(x, w_in_p, _NSA_SPLITS, _NSA_DTYPES, tm=n)
    n_cmp = (past + 1 - CMP_LEN) // CMP_STRIDE + 1
    n_sel = -(-(past + 1) // SEL_BLOCK)
    n_past_blk = past // SEL_BLOCK
    kvc = nsa_compress_paged(cache_cmp, page_table, r_cmp, b_cmp)
    q4 = q.reshape(n, kvh, grp, dh).transpose(0, 2, 1, 3)
    qx = (q4[:, :, :, None, :] * jnp.eye(kvh, dtype=q.dtype)[None, None, :, :, None]).reshape(n, NSA_ROWS, kvh * dh)
    idx = nsa_sample_select(qx, kvc, n_cmp, n_sel, past)[:, :kvh, :TOP_N]
    sub = PAGE_SIZE // SEL_BLOCK
    is_past = (idx >= 0) & (idx < n_past_blk)
    phys = jnp.take_along_axis(page_table, jnp.clip(idx // sub, 0, n_pages - 1).reshape(n, -1), axis=1).reshape(idx.shape)
    sub_idx = jnp.where(is_past, phys * sub + idx % sub, 0).astype(I32).reshape(-1)
    sel_blocks = cache_sel.reshape(-1, SEL_BLOCK, 2 * NSA_KV_W)
    gt = gates[:, :NSA_HEADS * 3].reshape(n, kvh, grp, 3).transpose(0, 2, 1, 3).reshape(n, NSA_ROWS, 3)
    n_top = idx.shape[2]
    per_seq = kvh * n_top
    vec = lambda a: a.reshape(n, 1, a.shape[-1])
    sspec = lambda shp, f: pl.BlockSpec(shp, f)
    blk_spec = lambda k: pl.BlockSpec((1, SEL_BLOCK, 2 * NSA_KV_W), lambda b, s_, o_: (s_[b * per_seq + k], 0, 0))
    row3 = lambda b, s_, o_: (b, 0, 0)
    out = pl.pallas_call(
        functools.partial(_nsa_sample_attend_body, n_cmp=n_cmp, q_pos=past),
        grid_spec=pltpu.PrefetchScalarGridSpec(
            num_scalar_prefetch=2,
            grid=(n,),
            in_specs=[sspec((1, NSA_ROWS, kvh * dh), row3), sspec((1,) + kvc.shape[1:], row3),
                      sspec((1,) + win_buf.shape[1:], row3), sspec((1, 1, 2 * NSA_KV_W), row3),
                      sspec((1, 1, 2 * NSA_KV_W), row3), sspec((1, NSA_ROWS, 3), row3)]
            + [blk_spec(k) for k in range(per_seq)],
            out_specs=pl.BlockSpec((1, NSA_ROWS, dh), row3)),
        out_shape=jax.ShapeDtypeStruct((n, NSA_ROWS, dh), BF16),
        compiler_params=_params("arbitrary"),
        name="nsa_sample_attend",
    )(sub_idx, is_past.astype(I32).reshape(-1), qx, kvc, win_buf, vec(kv_s), vec(kv_w), gt, *([sel_blocks] * per_seq))
    o = out.reshape(n, grp, kvh, dh).transpose(0, 2, 1, 3).reshape(n, NSA_Q_W)
    return o, kv_c, kv_s, kv_w


PROMPT_TM = 512


def kernel(x_prompt, x_sample, cache_nsa_cmp, cache_nsa_sel, state_nsa_win, state_ret, state_pool, state_conv, page_table, nsa_w_in, nsa_w_cmp, nsa_b_cmp, nsa_w_o, ret_w_in, ret_gn_g, ret_gn_b, ret_w_o, pool_w, pool_scale, conv_w_in, conv_w, conv_w_out, ln_g, ln_b, moe_w_rg, moe_b_rg, moe_w_re, moe_b_re, moe_w1, moe_w3, moe_w2):
    bp, t, d = x_prompt.shape
    ns = x_sample.shape[0]
    assert x_sample.shape[1] == 1 and (bp * t) % PROMPT_TM == 0
    past = page_table.shape[1] * PAGE_SIZE
    xp, xs = x_prompt.reshape(bp * t, d), x_sample.reshape(ns, d)
    kv5 = lambda a, n, rows: a.reshape(n, rows, 2, NSA_KV_HEADS, NSA_HEAD_DIM)
    shift_in = lambda old, new: jnp.concatenate([old[:, 1:], new[:, None]], axis=1)
    cmp_p, sel_p, win_p, ret_p, pool_p, conv_p = [], [], [], [], [], []
    cmp_s, sel_s, win_s, ret_s, pool_s, conv_s = [], [], [], [], [], []
    n_mix = 4
    for i in range(DEPTH):
        kind, j = i % n_mix, i // n_mix
        g0, b0 = ln_g[i, 0], ln_b[i, 0]
        if kind == 0:
            w_in_p, r_cmp, w_o = _nsa_w_in_padded(nsa_w_in[j]), _cmp_weights(nsa_w_cmp[j]), nsa_w_o[j].astype(BF16)
            o, a, b, c = nsa_prompt_mixer(xp.reshape(bp, t, d), w_in_p, r_cmp, nsa_b_cmp[j])
            cmp_p.append(kv5(a, bp, t)); sel_p.append(kv5(b, bp, t)); win_p.append(kv5(c, bp, t)[:, -min(WINDOW, t):])
            xp = matmul_res_ln(o, w_o, xp, g0, b0, PROMPT_TM)
            n_phys = cache_nsa_cmp.shape[1]
            win = state_nsa_win[j]
            o, a, b, c = nsa_sample_mixer(xs, cache_nsa_cmp[j].reshape(n_phys, PAGE_SIZE, 2 * NSA_KV_W),
                                          cache_nsa_sel[j].reshape(n_phys, PAGE_SIZE, 2 * NSA_KV_W),
                                          win.reshape(ns, win.shape[1], 2 * NSA_KV_W), page_table, w_in_p, r_cmp, nsa_b_cmp[j])
            cmp_s.append(kv5(a, ns, 1)); sel_s.append(kv5(b, ns, 1)); win_s.append(shift_in(win, kv5(c, ns, 1)[:, 0]))
            xs = matmul_res_ln(o, w_o, xs, g0, b0, ns)
        elif kind == 1:
            w_in, w_o = ret_w_in[j].astype(BF16), ret_w_o[j].astype(BF16)
            q, k, v, g = matmul_split(xp, w_in, _RET_SPLITS, _RET_DTYPES, tm=256)
            og, s_fin = retention_prompt(q, k, v, g, ret_gn_g[j], ret_gn_b[j], bp, t)
            ret_p.append(s_fin)
            xp = matmul_res_ln(og, w_o, xp, g0, b0, PROMPT_TM)
            q, k, v, g = matmul_split(xs, w_in, _RET_SPLITS, _RET_DTYPES, tm=ns)
            og, s_new = retention_sample(q, k, v, g, ret_gn_g[j], ret_gn_b[j], state_ret[j], past)
            ret_s.append(s_new)
            xs = matmul_res_ln(og, w_o, xs, g0, b0, ns)
        elif kind == 2:
            wp = pool_w[j].astype(BF16)
            pool_p.append(xp.reshape(bp, t, d)[:, -(POOL_MAX - 1):])
            xp = pool_prompt(xp.reshape(bp, t, d), wp, pool_scale[j], g0, b0).reshape(bp * t, d)
            pool_s.append(shift_in(state_pool[j], xs))
            xs = pool_sample(xs, state_pool[j].transpose(1, 0, 2), wp, pool_scale[j], g0, b0, past)
        else:
            w_in, w_out = conv_w_in[j].astype(BF16), conv_w_out[j].astype(BF16)
            bg, cg, h = matmul_split(xp, w_in, _CONV_SPLITS, _CONV_DTYPES, tm=PROMPT_TM)
            a, tail = conv_prompt(bg, cg, h, conv_w[j], bp, t)
            conv_p.append(tail[:, -(CONV_W - 1):])
            xp = matmul_res_ln(a, w_out, xp, g0, b0, PROMPT_TM)
            bg, cg, h = matmul_split(xs, w_in, _CONV_SPLITS, _CONV_DTYPES, tm=ns)
            a, u = conv_sample(bg, cg, h, state_conv[j].transpose(1, 0, 2), conv_w[j])
            conv_s.append(shift_in(state_conv[j], u))
            xs = matmul_res_ln(a, w_out, xs, g0, b0, ns)
        rw, rb = _router_weights(moe_w_rg[i], moe_b_rg[i], moe_w_re[i], moe_b_re[i])
        w1, w3, w2 = moe_w1[i].astype(BF16), moe_w3[i].astype(BF16), moe_w2[i].astype(BF16)
        xp = moe_layer(xp, rw, rb, w1, w3, w2, ln_g[i, 1], ln_b[i, 1], tm=PROMPT_TM)
        xs = moe_layer(xs, rw, rb, w1, w3, w2, ln_g[i, 1], ln_b[i, 1], tm=ns)
    st = jnp.stack
    return (xp.reshape(bp, t, d), xs.reshape(ns, 1, d), st(cmp_p), st(sel_p), st(win_p), st(ret_p), st(pool_p), st(conv_p),
            st(cmp_s), st(sel_s), st(win_s), st(ret_s), st(pool_s), st(conv_s))
```

```python
import functools

import jax
import jax.numpy as jnp
import numpy as np
from jax import lax
from jax.experimental import pallas as pl
from jax.experimental.pallas import tpu as pltpu

F32, BF16, I32 = jnp.float32, jnp.bfloat16, jnp.int32

D_MODEL = 1024
DEPTH = 4
PAGE_SIZE = 128
NSA_HEADS = 16
NSA_HEAD_DIM = 64
NSA_KV_HEADS = 4
NSA_GROUP = NSA_HEADS // NSA_KV_HEADS
NSA_KV_W = NSA_KV_HEADS * NSA_HEAD_DIM
NSA_Q_W = NSA_HEADS * NSA_HEAD_DIM
CMP_LEN = 32
CMP_STRIDE = 16
SEL_BLOCK = 64
TOP_N = 8
WINDOW = 512
RET_HEADS = 4
RET_DK = D_MODEL // RET_HEADS
RET_DV = 2 * D_MODEL // RET_HEADS
RET_CHUNK = 128
ROPE_BASE = 10000.0
POOL_WINDOWS = (2, 4, 8, 16)
POOL_GROUP = D_MODEL // len(POOL_WINDOWS)
POOL_MAX = max(POOL_WINDOWS)
CONV_W = 3
MOE_GROUPS = 4
MOE_EPG = 8
MOE_EXPERTS = MOE_GROUPS * MOE_EPG
MOE_FF = D_MODEL // 4
ALPHA = (2.0 * DEPTH) ** 0.25
LN_EPS = 1e-5
NEG = -1e30

LANES = 128
Q_TILE = 128
KV_TILE = 256
Q_TILES_PER_CALL = 4
VMEM_LIMIT = 56 * 1024 * 1024

_NT = (((1,), (1,)), ((), ()))
_TN = (((0,), (0,)), ((), ()))


def _params(*sem):
    return pltpu.CompilerParams(dimension_semantics=sem, vmem_limit_bytes=VMEM_LIMIT)


def _mm_body(x_ref, w_ref, *o_refs, splits, chunk):
    x = x_ref[...].astype(BF16)
    col = 0
    for o_ref, width in zip(o_refs, splits):
        for j in range(0, width, chunk):
            c = min(chunk, width - j)
            y = jnp.dot(x, w_ref[:, col + j:col + j + c], preferred_element_type=F32)
            o_ref[:, j:j + c] = y.astype(o_ref.dtype)
        col += width


def matmul_split(x, w, splits, dtypes, tm, chunk=512):
    m, k = x.shape
    n = w.shape[1]
    assert n == sum(splits) and m % tm == 0
    return pl.pallas_call(
        functools.partial(_mm_body, splits=tuple(splits), chunk=chunk),
        grid=(m // tm,),
        in_specs=[pl.BlockSpec((tm, k), lambda i: (i, 0)), pl.BlockSpec((k, n), lambda i: (0, 0))],
        out_specs=[pl.BlockSpec((tm, s), lambda i: (i, 0)) for s in splits],
        out_shape=[jax.ShapeDtypeStruct((m, s), d) for s, d in zip(splits, dtypes)],
        compiler_params=_params("arbitrary"),
        name="matmul_split",
    )(x, w)


def _layer_norm(v, g, b):
    mu = jnp.mean(v, axis=-1, keepdims=True)
    c = v - mu
    var = jnp.mean(c * c, axis=-1, keepdims=True)
    return c * lax.rsqrt(var + LN_EPS) * g + b


def _mm_res_ln_body(a_ref, w_ref, x_ref, g_ref, b_ref, o_ref):
    y = jnp.dot(a_ref[...].astype(BF16), w_ref[...], preferred_element_type=F32)
    o_ref[...] = _layer_norm(ALPHA * x_ref[...] + y, g_ref[...], b_ref[...])


def matmul_res_ln(a, w, x, g, b, tm):
    m, k = a.shape
    d = x.shape[1]
    return pl.pallas_call(
        _mm_res_ln_body,
        grid=(m // tm,),
        in_specs=[pl.BlockSpec((tm, k), lambda i: (i, 0)), pl.BlockSpec((k, d), lambda i: (0, 0)),
                  pl.BlockSpec((tm, d), lambda i: (i, 0)), pl.BlockSpec((1, d), lambda i: (0, 0)),
                  pl.BlockSpec((1, d), lambda i: (0, 0))],
        out_specs=pl.BlockSpec((tm, d), lambda i: (i, 0)),
        out_shape=jax.ShapeDtypeStruct((m, d), F32),
        compiler_params=_params("arbitrary"),
        name="matmul_res_ln",
    )(a, w, x, g.reshape(1, d), b.reshape(1, d))


def _cmp_weights(w_cmp):
    dh = NSA_HEAD_DIM
    w6 = w_cmp.reshape(2, 2, CMP_STRIDE // 2, 2, dh, dh)
    eye = jnp.eye(2, dtype=w_cmp.dtype)
    r = jnp.einsum("chpsde,xy->cpsxdhye", w6, eye)
    return r.reshape(2, CMP_STRIDE // 2, 4 * dh, 4 * dh).astype(BF16)


def _cmp_body(*refs):
    x_refs, (r_ref, b_ref, o_ref) = refs[:-3], refs[-3:]
    row_w = 2 * NSA_KV_W

    def cols(c0):
        return jnp.concatenate([x_ref[0, :, c0:c0 + LANES] for x_ref in x_refs], axis=0)

    for c in range(2):
        for hp in range(2):
            acc = None
            for sp in range(CMP_STRIDE // 2):
                c0 = (2 * sp) * row_w + c * NSA_KV_W + hp * LANES
                lhs = jnp.concatenate([cols(c0), cols(c0 + row_w)], axis=1).astype(BF16)
                part = jnp.dot(lhs, r_ref[c, sp], preferred_element_type=F32)
                acc = part if acc is None else acc + part
            lo, hi = acc[:, :LANES], acc[:, LANES:]
            nxt = pltpu.roll(hi, hi.shape[0] - 1, 0)
            o0 = c * NSA_KV_W + hp * LANES
            o_ref[0, :, o0:o0 + LANES] = lo + nxt + b_ref[:, o0:o0 + LANES]


def nsa_compress(rows, r_w, b_cmp):
    bsz, t, w = rows.shape
    n_ch = t // CMP_STRIDE
    x = rows.reshape(bsz, n_ch, CMP_STRIDE * w)
    bias = jnp.broadcast_to(b_cmp[:, None, :], (2, NSA_KV_HEADS, NSA_HEAD_DIM)).reshape(1, w)
    return pl.pallas_call(
        _cmp_body,
        grid=(bsz,),
        in_specs=[pl.BlockSpec((1, n_ch, CMP_STRIDE * w), lambda b: (b, 0, 0)),
                  pl.BlockSpec(r_w.shape, lambda b: (0, 0, 0, 0)),
                  pl.BlockSpec((1, w), lambda b: (0, 0))],
        out_specs=pl.BlockSpec((1, n_ch, w), lambda b: (b, 0, 0)),
        out_shape=jax.ShapeDtypeStruct((bsz, n_ch, w), F32),
        compiler_params=_params("arbitrary"),
        name="nsa_compress",
    )(x, r_w, bias)


def _cmp_paged_body(pt_ref, *refs):
    page_refs, (r_ref, b_ref, o_ref, rows_scr) = refs[:-4], refs[-4:]
    page = page_refs[0].shape[-1]
    for pi, p_ref in enumerate(page_refs):
        for c in range(2):
            for hp in range(2):
                pair = [p_ref[0, c, 2 * hp + hh].T for hh in range(2)]
                rows_scr[c * 2 + hp, pi * page:(pi + 1) * page, :] = jnp.concatenate(pair, axis=1)
    n_ch = o_ref.shape[1]
    for c in range(2):
        for hp in range(2):
            acc = None
            for sp in range(CMP_STRIDE // 2):
                pos = [rows_scr[c * 2 + hp, pl.ds(2 * sp + sl, n_ch, stride=CMP_STRIDE), :] for sl in range(2)]
                part = jnp.dot(jnp.concatenate(pos, axis=1).astype(BF16), r_ref[c, sp], preferred_element_type=F32)
                acc = part if acc is None else acc + part
            lo, hi = acc[:, :LANES], acc[:, LANES:]
            nxt = pltpu.roll(hi, hi.shape[0] - 1, 0)
            o0 = c * NSA_KV_W + hp * LANES
            o_ref[0, :, o0:o0 + LANES] = lo + nxt + b_ref[:, o0:o0 + LANES]


def nsa_compress_paged(cache_t, page_table, r_w, b_cmp):
    n_phys, _, kvh, dh, page = cache_t.shape
    n, n_pages = page_table.shape
    w = 2 * kvh * dh
    n_ch = n_pages * page // CMP_STRIDE
    bias = jnp.broadcast_to(b_cmp[:, None, :], (2, kvh, dh)).reshape(1, w)
    page_spec = lambda k: pl.BlockSpec((1, 2, kvh, dh, page), lambda b, pt: (pt[b * n_pages + k], 0, 0, 0, 0))
    return pl.pallas_call(
        _cmp_paged_body,
        grid_spec=pltpu.PrefetchScalarGridSpec(
            num_scalar_prefetch=1,
            grid=(n,),
            in_specs=[page_spec(k) for k in range(n_pages)]
            + [pl.BlockSpec(r_w.shape, lambda b, pt: (0, 0, 0, 0)), pl.BlockSpec((1, w), lambda b, pt: (0, 0))],
            out_specs=pl.BlockSpec((1, n_ch, w), lambda b, pt: (b, 0, 0)),
            scratch_shapes=[pltpu.VMEM((4, n_pages * page, LANES), F32)]),
        out_shape=jax.ShapeDtypeStruct((n, n_ch, w), F32),
        compiler_params=_params("arbitrary"),
        name="nsa_compress_paged",
    )(page_table.reshape(-1), *([cache_t] * n_pages), r_w, bias)


def _softmax_cols(s, ok):
    s = jnp.where(ok, s, NEG)
    m = jnp.max(s, axis=0, keepdims=True)
    p = jnp.where(ok, jnp.exp(s - m), 0.0)
    l = jnp.sum(p, axis=0, keepdims=True)
    return p / jnp.where(l > 0.0, l, 1.0)


def _nsa_prompt_body(q_ref, kc_ref, vct_ref, ks_ref, vst_ref, kw_ref, vwt_ref, g_ref, o_ref, imp_ref, bias_ref,
                     *, n_cmp, n_sel, i0, n_sel_tiles):
    i = i0 + pl.program_id(2)
    t0 = i * Q_TILE
    gq = NSA_GROUP * Q_TILE
    tq = t0 + lax.broadcasted_iota(I32, (1, Q_TILE), 1)
    tq4 = t0 + (lax.broadcasted_iota(I32, (1, gq), 1) & (Q_TILE - 1))
    q = q_ref[0, 0, 0]

    n_rows = kc_ref.shape[2]
    s = lax.dot_general(kc_ref[0, 0], q, _NT, preferred_element_type=F32)
    n_idx = lax.broadcasted_iota(I32, (n_rows, 1), 0)
    ok_c = (n_idx * CMP_STRIDE + (CMP_LEN - 1) <= tq4) & (n_idx < n_cmp)
    p_c = _softmax_cols(s, ok_c)
    o_c = jnp.dot(vct_ref[0, 0], p_c.astype(BF16), preferred_element_type=F32)

    imp = p_c[:, 0:Q_TILE]
    for g in range(1, NSA_GROUP):
        imp = imp + p_c[:, g * Q_TILE:(g + 1) * Q_TILE]
    imp_ref[...] = imp
    r_sel = SEL_BLOCK // CMP_STRIDE
    n_blk = n_rows // r_sel
    blk_imp = imp_ref[pl.ds(0, n_blk, stride=r_sel), :]
    for r in range(1, r_sel):
        blk_imp = blk_imp + imp_ref[pl.ds(r, n_blk, stride=r_sel), :]
    j_idx = lax.broadcasted_iota(I32, (n_blk, 1), 0)
    score = jnp.where(j_idx == tq // SEL_BLOCK, NSA_GROUP + 2.0,
                      jnp.where(j_idx == 0, NSA_GROUP + 1.0,
                                jnp.where(j_idx * SEL_BLOCK <= tq, blk_imp, -1.0)))
    score = jnp.where(j_idx < n_sel, score, -3.0)
    sel = jnp.zeros((n_blk, Q_TILE), F32)
    for _ in range(min(TOP_N, n_sel)):
        mx = jnp.max(score, axis=0, keepdims=True)
        first = jnp.min(jnp.where(score == mx, j_idx, n_blk), axis=0, keepdims=True)
        pick = j_idx == first
        sel = jnp.where(pick & (mx >= 0.0), 1.0, sel)
        score = jnp.where(pick, -2.0, score)
    blk_per_tile = KV_TILE // SEL_BLOCK
    for j in range(min(n_sel, n_sel_tiles * blk_per_tile)):
        bias_ref[j * SEL_BLOCK:(j + 1) * SEL_BLOCK, :] = jnp.broadcast_to(
            jnp.where(sel[j:j + 1, :] > 0.0, 0.0, NEG), (SEL_BLOCK, Q_TILE))

    def attend(k_ref, vt_ref, tiles, bias_fn):
        m = jnp.full((1, gq), NEG, F32)
        l = jnp.zeros((1, gq), F32)
        acc = jnp.zeros((NSA_HEAD_DIM, gq), F32)
        for kt in tiles:
            k0 = kt * KV_TILE if isinstance(kt, int) else pl.multiple_of(kt * KV_TILE, KV_TILE)
            s = lax.dot_general(k_ref[0, 0, pl.ds(k0, KV_TILE), :], q, _NT, preferred_element_type=F32)
            kpos = k0 + lax.broadcasted_iota(I32, (KV_TILE, 1), 0)
            s = s + jnp.concatenate([bias_fn(kt, k0, kpos)] * NSA_GROUP, axis=1)
            mn = jnp.maximum(m, jnp.max(s, axis=0, keepdims=True))
            alpha = jnp.exp(m - mn)
            p = jnp.exp(s - mn)
            l = alpha * l + jnp.sum(p, axis=0, keepdims=True)
            acc = alpha * acc + jnp.dot(vt_ref[0, 0, kt], p.astype(BF16), preferred_element_type=F32)
            m = mn
        return acc / l

    first_diag = (i0 * Q_TILE) // KV_TILE

    def sel_bias(kt, k0, kpos):
        bias = bias_ref[pl.ds(k0, KV_TILE), :]
        return bias if kt < first_diag else jnp.where(kpos <= tq, bias, NEG)

    def win_bias(kt, k0, kpos):
        dist = tq - kpos
        return jnp.where((dist >= 0) & (dist <= WINDOW), 0.0, NEG)

    o_s = attend(ks_ref, vst_ref, list(range(n_sel_tiles)), sel_bias)
    last = (t0 + Q_TILE - 1) // KV_TILE
    n_win_tiles = WINDOW // KV_TILE + 1
    if (i0 * Q_TILE + Q_TILE - 1) // KV_TILE < n_win_tiles - 1:
        win_tiles = list(range(n_sel_tiles))
    else:
        win_tiles = [last - (n_win_tiles - 1) + r for r in range(n_win_tiles)]
    o_w = attend(kw_ref, vwt_ref, win_tiles, win_bias)

    gate = 1.0 / (1.0 + jnp.exp(-g_ref[0, 0]))
    for g in range(NSA_GROUP):
        sl = slice(g * Q_TILE, (g + 1) * Q_TILE)
        o = gate[g, 0:1, :] * o_c[:, sl] + gate[g, 1:2, :] * o_s[:, sl] + gate[g, 2:3, :] * o_w[:, sl]
        o_ref[0, 0, g] = o.astype(o_ref.dtype)


def _head_major(kv):
    bsz, t, _ = kv.shape
    kv = kv.astype(BF16).reshape(bsz, t, 2, NSA_KV_HEADS, NSA_HEAD_DIM)
    k = kv[:, :, 0].transpose(0, 2, 1, 3)
    vt = kv[:, :, 1].reshape(bsz, t // KV_TILE, KV_TILE, NSA_KV_HEADS, NSA_HEAD_DIM).transpose(0, 3, 1, 4, 2)
    return k, vt


def nsa_prompt_attention(q, kvc, kv_s, kv_w, gates, n_cmp):
    bsz, t, _ = q.shape
    nq = t // Q_TILE
    n_sel = -(-t // SEL_BLOCK)
    n_rows = kvc.shape[1]
    dh, kvh, grp = NSA_HEAD_DIM, NSA_KV_HEADS, NSA_GROUP
    qh = q.reshape(bsz, nq, Q_TILE, kvh, grp, dh).transpose(0, 3, 1, 4, 2, 5).reshape(bsz, kvh, nq, grp * Q_TILE, dh)
    kc5 = kvc.astype(BF16).reshape(bsz, n_rows, 2, kvh, dh)
    kc = kc5[:, :, 0].transpose(0, 2, 1, 3)
    vct = kc5[:, :, 1].transpose(0, 2, 3, 1)
    ks, vst = _head_major(kv_s)
    kw, vwt = _head_major(kv_w)
    gt = gates[:, :, :NSA_HEADS * 3].reshape(bsz, t, kvh, grp, 3).transpose(0, 2, 3, 4, 1)
    nkt = t // KV_TILE
    assert WINDOW % KV_TILE == 0 and KV_TILE % Q_TILE == 0 and nq % Q_TILES_PER_CALL == 0
    bh = lambda b, h, i: (b, h, 0, 0)
    outs = []
    for i0 in range(0, nq, Q_TILES_PER_CALL):
        n_sel_tiles = ((i0 + Q_TILES_PER_CALL) * Q_TILE - 1) // KV_TILE + 1
        outs.append(pl.pallas_call(
            functools.partial(_nsa_prompt_body, n_cmp=n_cmp, n_sel=n_sel, i0=i0, n_sel_tiles=n_sel_tiles),
            grid=(bsz, kvh, Q_TILES_PER_CALL),
            in_specs=[pl.BlockSpec((1, 1, 1, grp * Q_TILE, dh), lambda b, h, i, i0=i0: (b, h, i0 + i, 0, 0)),
                      pl.BlockSpec((1, 1, n_rows, dh), bh),
                      pl.BlockSpec((1, 1, dh, n_rows), bh),
                      pl.BlockSpec((1, 1, t, dh), bh),
                      pl.BlockSpec((1, 1, nkt, dh, KV_TILE), lambda b, h, i: (b, h, 0, 0, 0)),
                      pl.BlockSpec((1, 1, t, dh), bh),
                      pl.BlockSpec((1, 1, nkt, dh, KV_TILE), lambda b, h, i: (b, h, 0, 0, 0)),
                      pl.BlockSpec((1, 1, grp, 3, Q_TILE), lambda b, h, i, i0=i0: (b, h, 0, 0, i0 + i))],
            out_specs=pl.BlockSpec((1, 1, grp, dh, Q_TILE), lambda b, h, i: (b, h, 0, 0, i)),
            out_shape=jax.ShapeDtypeStruct((bsz, kvh, grp, dh, Q_TILES_PER_CALL * Q_TILE), BF16),
            scratch_shapes=[pltpu.VMEM((n_rows, Q_TILE), F32), pltpu.VMEM((n_sel * SEL_BLOCK, Q_TILE), F32)],
            compiler_params=_params("arbitrary", "arbitrary", "arbitrary"),
            name=f"nsa_prompt_attention_q{i0}",
        )(qh, kc, vct, ks, vst, kw, vwt, gt))
    out = jnp.concatenate(outs, axis=-1)
    return out.transpose(0, 4, 1, 2, 3).reshape(bsz, t, kvh * grp * dh)


def _nsa_w_in_padded(w_in):
    wq = w_in[:, :NSA_Q_W] * NSA_HEAD_DIM ** -0.5
    pad = jnp.zeros((w_in.shape[0], LANES - 3 * NSA_HEADS), w_in.dtype)
    return jnp.concatenate([wq, w_in[:, NSA_Q_W:], pad], axis=1).astype(BF16)


_NSA_SPLITS = (NSA_Q_W, 2 * NSA_KV_W, 2 * NSA_KV_W, 2 * NSA_KV_W, LANES)
_NSA_DTYPES = (BF16, F32, F32, F32, F32)


def nsa_prompt_mixer(x, w_in_p, r_cmp, b_cmp):
    bsz, t, d = x.shape
    q, kv_c, kv_s, kv_w, gates = matmul_split(x.reshape(bsz * t, d), w_in_p, _NSA_SPLITS, _NSA_DTYPES, tm=min(512, bsz * t))
    r3 = lambda a: a.reshape(bsz, t, a.shape[-1])
    kv_c, kv_s, kv_w = r3(kv_c), r3(kv_s), r3(kv_w)
    n_cmp = (t - CMP_LEN) // CMP_STRIDE + 1
    kvc = nsa_compress(kv_c, r_cmp, b_cmp)
    o = nsa_prompt_attention(r3(q), kvc, kv_s, kv_w, r3(gates), n_cmp)
    return o.reshape(bsz * t, NSA_Q_W), kv_c, kv_s, kv_w


def _ret_log_decay():
    return jnp.log(1.0 - 2.0 ** (-5.0 - jnp.arange(RET_HEADS, dtype=F32)))


def _rope_tables(pos):
    half = RET_DK // 2
    inv = ROPE_BASE ** (-jnp.linspace(0.0, 1.0, half, dtype=F32))
    ang = pos.astype(F32)[:, None] * inv[None, :]
    return jnp.cos(ang), jnp.sin(ang)


def _rope(x, cos, sin):
    half = RET_DK // 2
    x1, x2 = x[:, :half], x[:, half:]
    return jnp.concatenate([x1 * cos - x2 * sin, x2 * cos + x1 * sin], axis=1)


def _group_norm_gate(o, gate, g, b):
    mu = jnp.mean(o, axis=-1, keepdims=True)
    c = o - mu
    var = jnp.mean(c * c, axis=-1, keepdims=True)
    on = c * lax.rsqrt(var + LN_EPS) * g + b
    return gate * (1.0 / (1.0 + jnp.exp(-gate))) * on


def _ret_prompt_body(q_ref, k_ref, v_ref, gt_ref, cos_ref, sin_ref, dm_ref, di_ref, dr_ref, dc_ref, gg_ref, gb_ref,
                     o_ref, s_ref, s_scr):
    c = pl.program_id(2)

    @pl.when(c == 0)
    def _():
        s_scr[...] = jnp.zeros_like(s_scr)

    cos, sin = cos_ref[...], sin_ref[...]
    q = _rope(q_ref[...], cos, sin)
    k = _rope(k_ref[...], cos, sin) * RET_DK ** -0.5
    qb, v = q.astype(BF16), v_ref[...]
    inner = lax.dot_general(qb, k.astype(BF16), _NT, preferred_element_type=F32) * dm_ref[0]
    o = jnp.dot(inner.astype(BF16), v, preferred_element_type=F32)
    s_old = s_scr[...]
    o = o + jnp.dot(qb, s_old.astype(BF16), preferred_element_type=F32) * di_ref[0]
    kd = (k * dr_ref[0]).astype(BF16)
    s_new = dc_ref[0] * s_old + lax.dot_general(kd, v, _TN, preferred_element_type=F32)
    s_scr[...] = s_new

    @pl.when(c == pl.num_programs(2) - 1)
    def _():
        s_ref[0, 0] = s_new

    o_ref[...] = _group_norm_gate(o, gt_ref[...], gg_ref[...], gb_ref[...]).astype(o_ref.dtype)


def retention_prompt(q, k, v, gate, gn_g, gn_b, bsz, t):
    ch = RET_CHUNK
    n_ch = t // ch
    lg = _ret_log_decay()
    i = jnp.arange(ch, dtype=F32)
    diff = i[:, None] - i[None, :]
    dmask = jnp.where(diff >= 0, jnp.exp(lg[:, None, None] * jnp.maximum(diff, 0.0)), 0.0)
    d_in = jnp.exp((i[None, :] + 1.0) * lg[:, None])[:, :, None]
    d_rev = jnp.exp((ch - 1.0 - i)[None, :] * lg[:, None])[:, :, None]
    d_c = jnp.exp(ch * lg)[:, None, None]
    cos, sin = _rope_tables(jnp.arange(t))
    half = RET_DK // 2
    row = lambda b, h, c: b * n_ch + c
    nh = RET_HEADS
    return pl.pallas_call(
        _ret_prompt_body,
        grid=(bsz, nh, n_ch),
        in_specs=[pl.BlockSpec((ch, RET_DK), lambda b, h, c: (row(b, h, c), h)),
                  pl.BlockSpec((ch, RET_DK), lambda b, h, c: (row(b, h, c), h)),
                  pl.BlockSpec((ch, RET_DV), lambda b, h, c: (row(b, h, c), h)),
                  pl.BlockSpec((ch, RET_DV), lambda b, h, c: (row(b, h, c), h)),
                  pl.BlockSpec((ch, half), lambda b, h, c: (c, 0)),
                  pl.BlockSpec((ch, half), lambda b, h, c: (c, 0)),
                  pl.BlockSpec((1, ch, ch), lambda b, h, c: (h, 0, 0)),
                  pl.BlockSpec((1, ch, 1), lambda b, h, c: (h, 0, 0)),
                  pl.BlockSpec((1, ch, 1), lambda b, h, c: (h, 0, 0)),
                  pl.BlockSpec((1, 1, 1), lambda b, h, c: (h, 0, 0)),
                  pl.BlockSpec((1, RET_DV), lambda b, h, c: (0, h)),
                  pl.BlockSpec((1, RET_DV), lambda b, h, c: (0, h))],
        out_specs=[pl.BlockSpec((ch, RET_DV), lambda b, h, c: (row(b, h, c), h)),
                   pl.BlockSpec((1, 1, RET_DK, RET_DV), lambda b, h, c: (b, h, 0, 0))],
        out_shape=[jax.ShapeDtypeStruct((bsz * t, nh * RET_DV), BF16),
                   jax.ShapeDtypeStruct((bsz, nh, RET_DK, RET_DV), F32)],
        scratch_shapes=[pltpu.VMEM((RET_DK, RET_DV), F32)],
        compiler_params=_params("arbitrary", "arbitrary", "arbitrary"),
        name="retention_prompt",
    )(q, k, v, gate, cos, sin, dmask, d_in, d_rev, d_c, gn_g.reshape(1, -1), gn_b.reshape(1, -1))


_RET_SPLITS = (RET_HEADS * RET_DK, RET_HEADS * RET_DK, RET_HEADS * RET_DV, RET_HEADS * RET_DV)
_RET_DTYPES = (F32, F32, BF16, F32)


def _ret_sample_body(q_ref, k_ref, v_ref, gt_ref, cos_ref, sin_ref, dec_ref, gg_ref, gb_ref, s_in_ref,
                     o_ref, s_out_ref):
    cos, sin = cos_ref[...], sin_ref[...]
    row0 = lax.broadcasted_iota(I32, (8, 1), 0) == 0
    for h in range(RET_HEADS):
        qh = _rope(q_ref[0, :, h * RET_DK:(h + 1) * RET_DK], cos, sin)
        kh = _rope(k_ref[0, :, h * RET_DK:(h + 1) * RET_DK], cos, sin) * RET_DK ** -0.5
        vh = v_ref[0, :, h * RET_DV:(h + 1) * RET_DV]
        qb, kb = qh.astype(BF16), kh.astype(BF16)
        dec = dec_ref[h]
        s_old = s_in_ref[0, h]
        q8 = jnp.broadcast_to(qb, (8, RET_DK))
        cross = jnp.dot(q8, s_old.astype(BF16), preferred_element_type=F32)[0:1] * dec
        inner = jnp.sum(qb.astype(F32) * kb.astype(F32), axis=1, keepdims=True)
        o = inner.astype(BF16).astype(F32) * vh.astype(F32) + cross
        k8 = jnp.where(row0, jnp.broadcast_to(kb.astype(F32), (8, RET_DK)), 0.0).astype(BF16)
        v8 = jnp.broadcast_to(vh, (8, RET_DV))
        s_out_ref[0, h] = dec * s_old + lax.dot_general(k8, v8, _TN, preferred_element_type=F32)
        sl = slice(h * RET_DV, (h + 1) * RET_DV)
        o_ref[0, :, sl] = _group_norm_gate(o, gt_ref[0, :, sl], gg_ref[:, sl], gb_ref[:, sl]).astype(o_ref.dtype)


def retention_sample(q, k, v, gate, gn_g, gn_b, state, pos0):
    n = q.shape[0]
    cos, sin = _rope_tables(jnp.full((1,), pos0))
    dec = jnp.exp(_ret_log_decay())
    r3 = lambda a: a.reshape(n, 1, a.shape[-1])
    w_qk, w_v = RET_HEADS * RET_DK, RET_HEADS * RET_DV
    half = RET_DK // 2
    vec = lambda w: pl.BlockSpec((1, 1, w), lambda b: (b, 0, 0))
    og, s_new = pl.pallas_call(
        _ret_sample_body,
        grid=(n,),
        in_specs=[vec(w_qk), vec(w_qk), vec(w_v), vec(w_v),
                  pl.BlockSpec((1, half), lambda b: (0, 0)), pl.BlockSpec((1, half), lambda b: (0, 0)),
                  pl.BlockSpec(memory_space=pltpu.SMEM),
                  pl.BlockSpec((1, w_v), lambda b: (0, 0)), pl.BlockSpec((1, w_v), lambda b: (0, 0)),
                  pl.BlockSpec((1, RET_HEADS, RET_DK, RET_DV), lambda b: (b, 0, 0, 0))],
        out_specs=[vec(w_v), pl.BlockSpec((1, RET_HEADS, RET_DK, RET_DV), lambda b: (b, 0, 0, 0))],
        out_shape=[jax.ShapeDtypeStruct((n, 1, w_v), BF16), jax.ShapeDtypeStruct(state.shape, F32)],
        compiler_params=_params("arbitrary"),
        name="retention_sample",
    )(r3(q), r3(k), r3(v), r3(gate), cos, sin, dec, gn_g.reshape(1, -1), gn_b.reshape(1, -1), state)
    return og.reshape(n, w_v), s_new


POOL_HALO = 16


def _pool_mix(win_sum_fn, x, pos, w_ref, sc_ref, g_ref, b_ref):
    ys = []
    for gi, w in enumerate(POOL_WINDOWS):
        sl = slice(gi * POOL_GROUP, (gi + 1) * POOL_GROUP)
        pooled = win_sum_fn(gi, w) / jnp.minimum(float(w), pos + 1.0) - x[:, sl]
        ys.append(jnp.dot(pooled.astype(BF16), w_ref[gi], preferred_element_type=F32))
    y = jnp.concatenate(ys, axis=1) * sc_ref[...]
    return _layer_norm(ALPHA * x + y, g_ref[...], b_ref[...])


def _pool_prompt_body(x_ref, w_ref, sc_ref, g_ref, b_ref, o_ref, xh_scr, *, tt):
    j = pl.program_id(1)

    @pl.when(j == 0)
    def _():
        xh_scr[0:POOL_HALO, :] = jnp.zeros((POOL_HALO, D_MODEL), F32)

    x = x_ref[0]
    xh_scr[POOL_HALO:, :] = x
    pos = (j * tt + lax.broadcasted_iota(I32, (tt, 1), 0)).astype(F32)

    def win_sum(gi, w):
        c0 = gi * POOL_GROUP
        acc = x[:, c0:c0 + POOL_GROUP]
        for u in range(1, w):
            acc = acc + xh_scr[POOL_HALO - u:POOL_HALO - u + tt, c0:c0 + POOL_GROUP]
        return acc

    o_ref[0] = _pool_mix(win_sum, x, pos, w_ref, sc_ref, g_ref, b_ref)
    xh_scr[0:POOL_HALO, :] = x[tt - POOL_HALO:, :]


def pool_prompt(x, w_pool, scale, g, b, tt=256):
    bsz, t, d = x.shape
    tt = min(tt, t)
    row = lambda b_, j: (0, 0)
    return pl.pallas_call(
        functools.partial(_pool_prompt_body, tt=tt),
        grid=(bsz, t // tt),
        in_specs=[pl.BlockSpec((1, tt, d), lambda b_, j: (b_, j, 0)),
                  pl.BlockSpec(w_pool.shape, lambda b_, j: (0, 0, 0)),
                  pl.BlockSpec((1, d), row), pl.BlockSpec((1, d), row), pl.BlockSpec((1, d), row)],
        out_specs=pl.BlockSpec((1, tt, d), lambda b_, j: (b_, j, 0)),
        out_shape=jax.ShapeDtypeStruct(x.shape, F32),
        scratch_shapes=[pltpu.VMEM((POOL_HALO + tt, d), F32)],
        compiler_params=_params("arbitrary", "arbitrary"),
        name="pool_prompt",
    )(x, w_pool, scale.reshape(1, d), g.reshape(1, d), b.reshape(1, d))


def _pool_sample_body(x_ref, st_ref, w_ref, sc_ref, g_ref, b_ref, o_ref, *, pos0):
    x = x_ref[...]
    n_hist = st_ref.shape[0]

    def win_sum(gi, w):
        c0 = gi * POOL_GROUP
        acc = x[:, c0:c0 + POOL_GROUP]
        for u in range(1, w):
            acc = acc + st_ref[n_hist - u, :, c0:c0 + POOL_GROUP]
        return acc

    pos = jnp.full((x.shape[0], 1), float(pos0), F32)
    o_ref[...] = _pool_mix(win_sum, x, pos, w_ref, sc_ref, g_ref, b_ref)


def pool_sample(x, hist, w_pool, scale, g, b, pos0):
    n, d = x.shape
    whole = lambda a: pl.BlockSpec(a.shape, lambda i: (0,) * a.ndim)
    args = (x, hist, w_pool, scale.reshape(1, d), g.reshape(1, d), b.reshape(1, d))
    return pl.pallas_call(
        functools.partial(_pool_sample_body, pos0=pos0),
        grid=(1,),
        in_specs=[whole(a) for a in args],
        out_specs=pl.BlockSpec((n, d), lambda i: (0, 0)),
        out_shape=jax.ShapeDtypeStruct((n, d), F32),
        compiler_params=_params("arbitrary"),
        name="pool_sample",
    )(*args)


CONV_HALO = 8


def _conv_prompt_body(bg_ref, cg_ref, h_ref, wc_ref, a_ref, tail_ref, uh_scr, *, tt):
    j = pl.program_id(1)

    @pl.when(j == 0)
    def _():
        uh_scr[0:CONV_HALO, :] = jnp.zeros((CONV_HALO, D_MODEL), F32)

    u = cg_ref[...] * h_ref[...]
    uh_scr[CONV_HALO:, :] = u
    conv = wc_ref[CONV_W - 1:CONV_W, :] * u
    for jj in range(CONV_W - 1):
        back = CONV_W - 1 - jj
        conv = conv + wc_ref[jj:jj + 1, :] * uh_scr[CONV_HALO - back:CONV_HALO - back + tt, :]
    a_ref[...] = (bg_ref[...] * conv).astype(a_ref.dtype)
    uh_scr[0:CONV_HALO, :] = u[tt - CONV_HALO:, :]
    tail_ref[0] = u[tt - CONV_HALO:, :]


def conv_prompt(bg, cg, h, w_conv, bsz, t, tt=256):
    d = bg.shape[1]
    tt = min(tt, t)
    nt = t // tt
    blk = pl.BlockSpec((tt, d), lambda b_, j: (b_ * nt + j, 0))
    return pl.pallas_call(
        functools.partial(_conv_prompt_body, tt=tt),
        grid=(bsz, nt),
        in_specs=[blk, blk, blk, pl.BlockSpec((CONV_W, d), lambda b_, j: (0, 0))],
        out_specs=[blk, pl.BlockSpec((1, CONV_HALO, d), lambda b_, j: (b_, 0, 0))],
        out_shape=[jax.ShapeDtypeStruct((bsz * t, d), BF16), jax.ShapeDtypeStruct((bsz, CONV_HALO, d), F32)],
        scratch_shapes=[pltpu.VMEM((CONV_HALO + tt, d), F32)],
        compiler_params=_params("arbitrary", "arbitrary"),
        name="conv_prompt",
    )(bg, cg, h, w_conv)


def _conv_sample_body(bg_ref, cg_ref, h_ref, prev_ref, wc_ref, a_ref, u_ref):
    u = cg_ref[...] * h_ref[...]
    conv = wc_ref[CONV_W - 1:CONV_W, :] * u
    for jj in range(CONV_W - 1):
        conv = conv + wc_ref[jj:jj + 1, :] * prev_ref[jj]
    a_ref[...] = (bg_ref[...] * conv).astype(a_ref.dtype)
    u_ref[...] = u


def conv_sample(bg, cg, h, prev, w_conv):
    n, d = bg.shape
    whole = lambda a: pl.BlockSpec(a.shape, lambda i: (0,) * a.ndim)
    args = (bg, cg, h, prev, w_conv)
    return pl.pallas_call(
        _conv_sample_body,
        grid=(1,),
        in_specs=[whole(a) for a in args],
        out_specs=[pl.BlockSpec((n, d), lambda i: (0, 0)), pl.BlockSpec((n, d), lambda i: (0, 0))],
        out_shape=[jax.ShapeDtypeStruct((n, d), BF16), jax.ShapeDtypeStruct((n, d), F32)],
        compiler_params=_params("arbitrary"),
        name="conv_sample",
    )(*args)


_CONV_SPLITS = (D_MODEL, D_MODEL, D_MODEL)
_CONV_DTYPES = (F32, F32, F32)


ROUTER_LANES = LANES
EXPERT_LANE0 = MOE_GROUPS


def _router_body(x_ref, w_ref, b_ref, o_ref):
    logits = jnp.dot(x_ref[...], w_ref[...], preferred_element_type=F32, precision=lax.Precision.HIGHEST) + b_ref[...]
    lane = lax.broadcasted_iota(I32, logits.shape, 1)
    big = ROUTER_LANES

    def top1(mask):
        v = jnp.max(jnp.where(mask, logits, -jnp.inf), axis=1, keepdims=True)
        i = jnp.min(jnp.where(mask & (logits == v), lane, big), axis=1, keepdims=True)
        return v, i

    is_g = lane < MOE_GROUPS
    vg, gsel = top1(is_g)
    pg_sel = 1.0 / jnp.sum(jnp.where(is_g, jnp.exp(logits - vg), 0.0), axis=1, keepdims=True)
    e0 = EXPERT_LANE0 + gsel * MOE_EPG
    is_e = (lane >= e0) & (lane < e0 + MOE_EPG)
    v1, i1 = top1(is_e)
    v2, i2 = top1(is_e & (lane != i1))
    r = jnp.exp(v2 - v1)
    pe1 = pg_sel / (1.0 + r)
    pe2 = pg_sel * r / (1.0 + r)
    comb = jnp.where(lane == i1, pe1, 0.0) + jnp.where(lane == i2, pe2, 0.0)
    o_ref[...] = jnp.where(lane == 0, gsel.astype(F32), comb)


def moe_router(x, w_r, b_r, tm):
    m, d = x.shape
    return pl.pallas_call(
        _router_body,
        grid=(m // tm,),
        in_specs=[pl.BlockSpec((tm, d), lambda i: (i, 0)), pl.BlockSpec((d, ROUTER_LANES), lambda i: (0, 0)),
                  pl.BlockSpec((1, ROUTER_LANES), lambda i: (0, 0))],
        out_specs=pl.BlockSpec((tm, ROUTER_LANES), lambda i: (i, 0)),
        out_shape=jax.ShapeDtypeStruct((m, ROUTER_LANES), F32),
        compiler_params=_params("arbitrary"),
        name="moe_router",
    )(x, w_r, b_r)


def _router_weights(w_rg, b_rg, w_re, b_re):
    d = w_rg.shape[0]
    pad = ROUTER_LANES - MOE_GROUPS - MOE_EXPERTS
    w = jnp.concatenate([w_rg, w_re, jnp.zeros((d, pad), F32)], axis=1)
    b = jnp.concatenate([b_rg, b_re, jnp.zeros((pad,), F32)]).reshape(1, ROUTER_LANES)
    return w, b


def _moe_ffn_body(gid_ref, x_ref, c_ref, w1_ref, w3_ref, w2_ref, g_ref, b_ref, o_ref, w1b, w3b, w2b):
    i = pl.program_id(0)

    @pl.when((i == 0) | (gid_ref[i] != gid_ref[jnp.maximum(i - 1, 0)]))
    def _():
        for e in range(MOE_EPG):
            w1b[e] = w1_ref[e].astype(BF16)
            w3b[e] = w3_ref[e].astype(BF16)
            w2b[e] = w2_ref[e].astype(BF16)

    x = x_ref[...]
    xb = x.astype(BF16)
    comb = c_ref[...]
    y = jnp.zeros(x.shape, F32)
    for e in range(MOE_EPG):
        a = jnp.dot(xb, w1b[e], preferred_element_type=F32)
        gte = jnp.dot(xb, w3b[e], preferred_element_type=F32)
        hcol = a * (1.0 / (1.0 + jnp.exp(-a))) * gte * comb[:, e:e + 1]
        y = y + jnp.dot(hcol.astype(BF16), w2b[e], preferred_element_type=F32)
    o_ref[...] = _layer_norm(ALPHA * x + y, g_ref[...], b_ref[...])


def moe_layer(x, router_w, router_b, w1, w3, w2, g, b, tm):
    m, d = x.shape
    r = moe_router(x, router_w, router_b, tm=min(512, m))
    gid = r[:, 0].astype(I32)
    comb_all = r[:, EXPERT_LANE0:EXPERT_LANE0 + MOE_EXPERTS].reshape(m, MOE_GROUPS, MOE_EPG)
    comb = jnp.take_along_axis(comb_all, gid[:, None, None], axis=1)[:, 0]
    onehot = (gid[:, None] == jnp.arange(MOE_GROUPS)[None, :]).astype(I32)
    rank = jnp.take_along_axis(jnp.cumsum(onehot, axis=0) - onehot, gid[:, None], axis=1)[:, 0]
    counts = onehot.sum(axis=0)
    tiles = (counts + tm - 1) // tm
    tile_end = jnp.cumsum(tiles)
    start = (tile_end - tiles) * tm
    dest = start[gid] + rank
    n_tiles = m // tm + MOE_GROUPS
    mp = n_tiles * tm
    src = jnp.full((mp,), -1, I32).at[dest].set(jnp.arange(m, dtype=I32))
    live = (src >= 0).astype(F32)
    src = jnp.maximum(src, 0)
    xs = jnp.take(x, src, axis=0)
    cs = jnp.take(comb, src, axis=0) * live[:, None]
    tile_gid = jnp.minimum((jnp.arange(n_tiles, dtype=I32)[:, None] >= tile_end[None, :]).sum(axis=1),
                           MOE_GROUPS - 1).astype(I32)
    wspec = lambda shp: pl.BlockSpec(shp, lambda i, gid_: (gid_[i], 0, 0), pipeline_mode=pl.Buffered(1))
    out = pl.pallas_call(
        _moe_ffn_body,
        grid_spec=pltpu.PrefetchScalarGridSpec(
            num_scalar_prefetch=1,
            grid=(n_tiles,),
            in_specs=[pl.BlockSpec((tm, d), lambda i, gid_: (i, 0)),
                      pl.BlockSpec((tm, MOE_EPG), lambda i, gid_: (i, 0)),
                      wspec((MOE_EPG, d, MOE_FF)), wspec((MOE_EPG, d, MOE_FF)), wspec((MOE_EPG, MOE_FF, d)),
                      pl.BlockSpec((1, d), lambda i, gid_: (0, 0)), pl.BlockSpec((1, d), lambda i, gid_: (0, 0))],
            out_specs=pl.BlockSpec((tm, d), lambda i, gid_: (i, 0)),
            scratch_shapes=[pltpu.VMEM((MOE_EPG, d, MOE_FF), BF16), pltpu.VMEM((MOE_EPG, d, MOE_FF), BF16),
                            pltpu.VMEM((MOE_EPG, MOE_FF, d), BF16)]),
        out_shape=jax.ShapeDtypeStruct((mp, d), F32),
        compiler_params=_params("arbitrary"),
        name="moe_ffn",
    )(tile_gid, xs, cs, w1, w3, w2, g.reshape(1, d), b.reshape(1, d))
    return jnp.take(out, dest, axis=0)


NSA_ROWS = NSA_HEADS
SEL_SAMPLES = 8


def _softmax_rows(s, ok):
    s = jnp.where(ok, s, NEG)
    m = jnp.max(s, axis=1, keepdims=True)
    p = jnp.where(ok, jnp.exp(s - m), 0.0)
    l = jnp.sum(p, axis=1, keepdims=True)
    return p / jnp.where(l > 0.0, l, 1.0)


def _nsa_sample_select_body(qx_ref, kvc_ref, pool_ref, idx_ref, *, n_cmp, n_sel, q_pos):
    n_rows = kvc_ref.shape[1]
    lane = lax.broadcasted_iota(I32, (NSA_ROWS, n_rows), 1)
    ok_c = (lane * CMP_STRIDE + (CMP_LEN - 1) <= q_pos) & (lane < n_cmp)
    for s_i in range(SEL_SAMPLES):
        kc = kvc_ref[s_i, :, 0:NSA_KV_W].astype(BF16)
        s = lax.dot_general(qx_ref[s_i], kc, _NT, preferred_element_type=F32)
        p = _softmax_rows(s, ok_c)
        imp = p
        for g in range(1, NSA_GROUP):
            imp = imp + pltpu.roll(p, g * NSA_KV_HEADS, 0)
        blk = jnp.dot(imp, pool_ref[...], preferred_element_type=F32, precision=lax.Precision.HIGHEST)
        score = jnp.where(lane == q_pos // SEL_BLOCK, NSA_GROUP + 2.0,
                          jnp.where(lane == 0, NSA_GROUP + 1.0,
                                    jnp.where(lane * SEL_BLOCK <= q_pos, blk, -1.0)))
        score = jnp.where(lane < n_sel, score, -3.0)
        idx = jnp.full(score.shape, -1, I32)
        for k in range(min(TOP_N, n_sel)):
            mx = jnp.max(score, axis=1, keepdims=True)
            first = jnp.min(jnp.where(score == mx, lane, n_rows), axis=1, keepdims=True)
            idx = jnp.where(lane == k, jnp.where(mx >= 0.0, first, -1), idx)
            score = jnp.where(lane == first, -2.0, score)
        idx_ref[s_i] = idx[0:8, :]


def nsa_sample_select(qx, kvc, n_cmp, n_sel, q_pos):
    n, n_rows, w = kvc.shape
    r_sel = SEL_BLOCK // CMP_STRIDE
    pool = (jnp.arange(n_rows)[:, None] // r_sel == jnp.arange(n_rows)[None, :]).astype(F32)
    return pl.pallas_call(
        functools.partial(_nsa_sample_select_body, n_cmp=n_cmp, n_sel=n_sel, q_pos=q_pos),
        grid=(n // SEL_SAMPLES,),
        in_specs=[pl.BlockSpec((SEL_SAMPLES, NSA_ROWS, NSA_KV_W), lambda i: (i, 0, 0)),
                  pl.BlockSpec((SEL_SAMPLES, n_rows, w), lambda i: (i, 0, 0)),
                  pl.BlockSpec((n_rows, n_rows), lambda i: (0, 0))],
        out_specs=pl.BlockSpec((SEL_SAMPLES, 8, n_rows), lambda i: (i, 0, 0)),
        out_shape=jax.ShapeDtypeStruct((n, 8, n_rows), I32),
        compiler_params=_params("arbitrary"),
        name="nsa_sample_select",
    )(qx, kvc, pool)


Q_ROWS = 8


def _nsa_sample_attend_body(page_ref, half_ref, qh_ref, kvc_ref, win_ref, ksn_ref, kwn_ref, g_ref, *rest, n_cmp, q_pos):
    blk_refs, o_ref = rest[:-1], rest[-1]
    b = pl.program_id(0)
    dh = NSA_HEAD_DIM
    rnd = lambda a: a.astype(BF16).astype(F32)
    n_top = len(blk_refs) // NSA_KV_HEADS
    page = blk_refs[0].shape[-1]
    n_rows = kvc_ref.shape[1]
    lane_c = lax.broadcasted_iota(I32, (Q_ROWS, n_rows), 1)
    ok_c = (lane_c * CMP_STRIDE + (CMP_LEN - 1) <= q_pos) & (lane_c < n_cmp)
    lane_s = lax.broadcasted_iota(I32, (Q_ROWS, n_top * page), 1)
    gate = 1.0 / (1.0 + jnp.exp(-g_ref[0]))

    for h in range(NSA_KV_HEADS):
        q = qh_ref[0, h]
        qf = q.astype(F32)

        def with_new_key(k_t, v_t, ok, new_ref):
            s = jnp.dot(q, k_t, preferred_element_type=F32)
            s_new = jnp.sum(qf * rnd(new_ref[0, :, h * dh:(h + 1) * dh]), axis=1, keepdims=True)
            if ok is not None:
                s = jnp.where(ok, s, NEG)
            m = jnp.maximum(jnp.max(s, axis=1, keepdims=True), s_new)
            p = jnp.exp(s - m) if ok is None else jnp.where(ok, jnp.exp(s - m), 0.0)
            p_new = jnp.exp(s_new - m)
            l = jnp.sum(p, axis=1, keepdims=True) + p_new
            v_new = new_ref[0, :, NSA_KV_W + h * dh:NSA_KV_W + (h + 1) * dh]
            o = lax.dot_general(p.astype(BF16), v_t, _NT, preferred_element_type=F32)
            return (o + rnd(p_new) * rnd(v_new)) / l

        kc = kvc_ref[0, :, h * dh:(h + 1) * dh].astype(BF16)
        vc = kvc_ref[0, :, NSA_KV_W + h * dh:NSA_KV_W + (h + 1) * dh].astype(BF16)
        s = lax.dot_general(q, kc, _NT, preferred_element_type=F32)
        o_c = jnp.dot(_softmax_rows(s, ok_c).astype(BF16), vc, preferred_element_type=F32)

        o_w = with_new_key(win_ref[0, 0, h].astype(BF16), win_ref[0, 1, h].astype(BF16), None, kwn_ref)

        refs = blk_refs[h * n_top:(h + 1) * n_top]
        k_t = jnp.concatenate([r[0, 0, 0] for r in refs], axis=1).astype(BF16)
        v_t = jnp.concatenate([r[0, 1, 0] for r in refs], axis=1).astype(BF16)
        want = jnp.full(lane_s.shape, -1, I32)
        for k in range(n_top):
            want = jnp.where(lane_s // page == k, half_ref[(b * NSA_KV_HEADS + h) * n_top + k], want)
        o_s = with_new_key(k_t, v_t, (lane_s % page) // SEL_BLOCK == want, ksn_ref)

        o = gate[h, :, 0:1] * o_c + gate[h, :, 1:2] * o_s + gate[h, :, 2:3] * o_w
        o_ref[0, h] = o.astype(o_ref.dtype)


def nsa_sample_mixer(x, cache_cmp_t, cache_sel_t, win_t, page_table, w_in_p, r_cmp, b_cmp):
    n, d = x.shape
    n_pages = page_table.shape[1]
    past = n_pages * PAGE_SIZE
    dh, kvh, grp = NSA_HEAD_DIM, NSA_KV_HEADS, NSA_GROUP
    q, kv_c, kv_s, kv_w, gates = matmul_split(x, w_in_p, _NSA_SPLITS, _NSA_DTYPES, tm=n)
    n_cmp = (past + 1 - CMP_LEN) // CMP_STRIDE + 1
    n_sel = -(-(past + 1) // SEL_BLOCK)
    n_past_blk = past // SEL_BLOCK
    kvc = nsa_compress_paged(cache_cmp_t, page_table, r_cmp, b_cmp)
    q4 = q.reshape(n, kvh, grp, dh)
    q4t = q4.transpose(0, 2, 1, 3)
    qx = (q4t[:, :, :, None, :] * jnp.eye(kvh, dtype=q.dtype)[None, None, :, :, None]).reshape(n, NSA_ROWS, kvh * dh)
    idx = nsa_sample_select(qx, kvc, n_cmp, n_sel, past)[:, :kvh, :TOP_N]
    sub = PAGE_SIZE // SEL_BLOCK
    is_past = (idx >= 0) & (idx < n_past_blk)
    phys = jnp.take_along_axis(page_table, jnp.clip(idx // sub, 0, n_pages - 1).reshape(n, -1), axis=1).reshape(idx.shape)
    page_idx = jnp.where(is_past, phys, 0).astype(I32).reshape(-1)
    half = jnp.where(is_past, idx % sub, -1).astype(I32).reshape(-1)
    pad_rows = lambda a: jnp.pad(a, ((0, 0), (0, 0), (0, Q_ROWS - grp), (0, 0)))
    qh = pad_rows(q4)
    gt = pad_rows(gates[:, :NSA_HEADS * 3].reshape(n, kvh, grp, 3))
    n_top = idx.shape[2]
    per_seq = kvh * n_top
    vec = lambda a: a.reshape(n, 1, a.shape[-1])
    blk_spec = lambda k: pl.BlockSpec((1, 2, 1, dh, PAGE_SIZE),
                                      lambda b, p_, h_: (p_[b * per_seq + k], 0, k // n_top, 0, 0))
    seq = lambda a: pl.BlockSpec((1,) + a.shape[1:], lambda b, p_, h_: (b,) + (0,) * (a.ndim - 1))
    ins = (qh, kvc, win_t, vec(kv_s), vec(kv_w), gt)
    out = pl.pallas_call(
        functools.partial(_nsa_sample_attend_body, n_cmp=n_cmp, q_pos=past),
        grid_spec=pltpu.PrefetchScalarGridSpec(
            num_scalar_prefetch=2,
            grid=(n,),
            in_specs=[seq(a) for a in ins] + [blk_spec(k) for k in range(per_seq)],
            out_specs=pl.BlockSpec((1, kvh, Q_ROWS, dh), lambda b, p_, h_: (b, 0, 0, 0))),
        out_shape=jax.ShapeDtypeStruct((n, kvh, Q_ROWS, dh), BF16),
        compiler_params=_params("arbitrary"),
        name="nsa_sample_attend",
    )(page_idx, half, *ins, *([cache_sel_t] * per_seq))
    return out[:, :, :grp].reshape(n, NSA_Q_W), kv_c, kv_s, kv_w


PROMPT_TM = 512


def kernel(x_prompt, x_sample, cache_nsa_cmp, cache_nsa_sel, state_nsa_win, state_ret, state_pool, state_conv, page_table, nsa_w_in, nsa_w_cmp, nsa_b_cmp, nsa_w_o, ret_w_in, ret_gn_g, ret_gn_b, ret_w_o, pool_w, pool_scale, conv_w_in, conv_w, conv_w_out, ln_g, ln_b, moe_w_rg, moe_b_rg, moe_w_re, moe_b_re, moe_w1, moe_w3, moe_w2):
    bp, t, d = x_prompt.shape
    ns = x_sample.shape[0]
    assert x_sample.shape[1] == 1 and (bp * t) % PROMPT_TM == 0
    past = page_table.shape[1] * PAGE_SIZE
    xp, xs = x_prompt.reshape(bp * t, d), x_sample.reshape(ns, d)
    kv5 = lambda a, n, rows: a.reshape(n, rows, 2, NSA_KV_HEADS, NSA_HEAD_DIM)
    shift_in = lambda old, new: jnp.concatenate([old[:, 1:], new[:, None]], axis=1)
    cmp_p, sel_p, win_p, ret_p, pool_p, conv_p = [], [], [], [], [], []
    cmp_s, sel_s, win_s, ret_s, pool_s, conv_s = [], [], [], [], [], []
    n_mix = 4
    for i in range(DEPTH):
        kind, j = i % n_mix, i // n_mix
        g0, b0 = ln_g[i, 0], ln_b[i, 0]
        if kind == 0:
            w_in_p, r_cmp, w_o = _nsa_w_in_padded(nsa_w_in[j]), _cmp_weights(nsa_w_cmp[j]), nsa_w_o[j].astype(BF16)
            o, a, b, c = nsa_prompt_mixer(xp.reshape(bp, t, d), w_in_p, r_cmp, nsa_b_cmp[j])
            cmp_p.append(kv5(a, bp, t)); sel_p.append(kv5(b, bp, t)); win_p.append(kv5(c, bp, t)[:, -min(WINDOW, t):])
            xp = matmul_res_ln(o, w_o, xp, g0, b0, PROMPT_TM)
            win = state_nsa_win[j]
            rows_minor = lambda a: jnp.transpose(a, (0, 2, 3, 4, 1))
            o, a, b, c = nsa_sample_mixer(xs, rows_minor(cache_nsa_cmp[j]), rows_minor(cache_nsa_sel[j]), rows_minor(win),
                                          page_table, w_in_p, r_cmp, nsa_b_cmp[j])
            cmp_s.append(kv5(a, ns, 1)); sel_s.append(kv5(b, ns, 1)); win_s.append(shift_in(win, kv5(c, ns, 1)[:, 0]))
            xs = matmul_res_ln(o, w_o, xs, g0, b0, ns)
        elif kind == 1:
            w_in, w_o = ret_w_in[j].astype(BF16), ret_w_o[j].astype(BF16)
            q, k, v, g = matmul_split(xp, w_in, _RET_SPLITS, _RET_DTYPES, tm=256)
            og, s_fin = retention_prompt(q, k, v, g, ret_gn_g[j], ret_gn_b[j], bp, t)
            ret_p.append(s_fin)
            xp = matmul_res_ln(og, w_o, xp, g0, b0, PROMPT_TM)
            q, k, v, g = matmul_split(xs, w_in, _RET_SPLITS, _RET_DTYPES, tm=ns)
            og, s_new = retention_sample(q, k, v, g, ret_gn_g[j], ret_gn_b[j], state_ret[j], past)
            ret_s.append(s_new)
            xs = matmul_res_ln(og, w_o, xs, g0, b0, ns)
        elif kind == 2:
            wp = pool_w[j].astype(BF16)
            pool_p.append(xp.reshape(bp, t, d)[:, -(POOL_MAX - 1):])
            xp = pool_prompt(xp.reshape(bp, t, d), wp, pool_scale[j], g0, b0).reshape(bp * t, d)
            pool_s.append(shift_in(state_pool[j], xs))
            xs = pool_sample(xs, state_pool[j].transpose(1, 0, 2), wp, pool_scale[j], g0, b0, past)
        else:
            w_in, w_out = conv_w_in[j].astype(BF16), conv_w_out[j].astype(BF16)
            bg, cg, h = matmul_split(xp, w_in, _CONV_SPLITS, _CONV_DTYPES, tm=PROMPT_TM)
            a, tail = conv_prompt(bg, cg, h, conv_w[j], bp, t)
            conv_p.append(tail[:, -(CONV_W - 1):])
            xp = matmul_res_ln(a, w_out, xp, g0, b0, PROMPT_TM)
            bg, cg, h = matmul_split(xs, w_in, _CONV_SPLITS, _CONV_DTYPES, tm=ns)
            a, u = conv_sample(bg, cg, h, state_conv[j].transpose(1, 0, 2), conv_w[j])
            conv_s.append(shift_in(state_conv[j], u))
            xs = matmul_res_ln(a, w_out, xs, g0, b0, ns)
        rw, rb = _router_weights(moe_w_rg[i], moe_b_rg[i], moe_w_re[i], moe_b_re[i])
        w1, w3, w2 = moe_w1[i], moe_w3[i], moe_w2[i]
        xp = moe_layer(xp, rw, rb, w1, w3, w2, ln_g[i, 1], ln_b[i, 1], tm=PROMPT_TM)
        xs = moe_layer(xs, rw, rb, w1, w3, w2, ln_g[i, 1], ln_b[i, 1], tm=ns)
    st = jnp.stack
    return (xp.reshape(bp, t, d), xs.reshape(ns, 1, d), st(cmp_p), st(sel_p), st(win_p), st(ret_p), st(pool_p), st(conv_p),
            st(cmp_s), st(sel_s), st(win_s), st(ret_s), st(pool_s), st(conv_s))
```

```python
import functools

import jax
import jax.numpy as jnp
import numpy as np
from jax import lax
from jax.experimental import pallas as pl
from jax.experimental.pallas import tpu as pltpu

F32, BF16, I32 = jnp.float32, jnp.bfloat16, jnp.int32

D_MODEL = 1024
DEPTH = 4
PAGE_SIZE = 128
NSA_HEADS = 16
NSA_HEAD_DIM = 64
NSA_KV_HEADS = 4
NSA_GROUP = NSA_HEADS // NSA_KV_HEADS
NSA_KV_W = NSA_KV_HEADS * NSA_HEAD_DIM
NSA_Q_W = NSA_HEADS * NSA_HEAD_DIM
CMP_LEN = 32
CMP_STRIDE = 16
SEL_BLOCK = 64
TOP_N = 8
WINDOW = 512
RET_HEADS = 4
RET_DK = D_MODEL // RET_HEADS
RET_DV = 2 * D_MODEL // RET_HEADS
RET_CHUNK = 128
ROPE_BASE = 10000.0
POOL_WINDOWS = (2, 4, 8, 16)
POOL_GROUP = D_MODEL // len(POOL_WINDOWS)
POOL_MAX = max(POOL_WINDOWS)
CONV_W = 3
MOE_GROUPS = 4
MOE_EPG = 8
MOE_EXPERTS = MOE_GROUPS * MOE_EPG
MOE_FF = D_MODEL // 4
ALPHA = (2.0 * DEPTH) ** 0.25
LN_EPS = 1e-5
NEG = -1e30

LANES = 128
Q_TILE = 128
KV_TILE = 256
Q_TILES_PER_CALL = 4
VMEM_LIMIT = 56 * 1024 * 1024

_NT = (((1,), (1,)), ((), ()))
_TN = (((0,), (0,)), ((), ()))


def _params(*sem):
    return pltpu.CompilerParams(dimension_semantics=sem, vmem_limit_bytes=VMEM_LIMIT)


def _mm_body(x_ref, w_ref, *o_refs, splits, chunk):
    x = x_ref[...].astype(BF16)
    col = 0
    for o_ref, width in zip(o_refs, splits):
        for j in range(0, width, chunk):
            c = min(chunk, width - j)
            y = jnp.dot(x, w_ref[:, col + j:col + j + c], preferred_element_type=F32)
            o_ref[:, j:j + c] = y.astype(o_ref.dtype)
        col += width


def matmul_split(x, w, splits, dtypes, tm, chunk=512):
    m, k = x.shape
    n = w.shape[1]
    assert n == sum(splits) and m % tm == 0
    return pl.pallas_call(
        functools.partial(_mm_body, splits=tuple(splits), chunk=chunk),
        grid=(m // tm,),
        in_specs=[pl.BlockSpec((tm, k), lambda i: (i, 0)), pl.BlockSpec((k, n), lambda i: (0, 0))],
        out_specs=[pl.BlockSpec((tm, s), lambda i: (i, 0)) for s in splits],
        out_shape=[jax.ShapeDtypeStruct((m, s), d) for s, d in zip(splits, dtypes)],
        compiler_params=_params("arbitrary"),
        name="matmul_split",
    )(x, w)


def _layer_norm(v, g, b):
    mu = jnp.mean(v, axis=-1, keepdims=True)
    c = v - mu
    var = jnp.mean(c * c, axis=-1, keepdims=True)
    return c * lax.rsqrt(var + LN_EPS) * g + b


def _mm_res_ln_body(a_ref, w_ref, x_ref, g_ref, b_ref, o_ref):
    y = jnp.dot(a_ref[...].astype(BF16), w_ref[...], preferred_element_type=F32)
    o_ref[...] = _layer_norm(ALPHA * x_ref[...] + y, g_ref[...], b_ref[...])


def matmul_res_ln(a, w, x, g, b, tm):
    m, k = a.shape
    d = x.shape[1]
    return pl.pallas_call(
        _mm_res_ln_body,
        grid=(m // tm,),
        in_specs=[pl.BlockSpec((tm, k), lambda i: (i, 0)), pl.BlockSpec((k, d), lambda i: (0, 0)),
                  pl.BlockSpec((tm, d), lambda i: (i, 0)), pl.BlockSpec((1, d), lambda i: (0, 0)),
                  pl.BlockSpec((1, d), lambda i: (0, 0))],
        out_specs=pl.BlockSpec((tm, d), lambda i: (i, 0)),
        out_shape=jax.ShapeDtypeStruct((m, d), F32),
        compiler_params=_params("arbitrary"),
        name="matmul_res_ln",
    )(a, w, x, g.reshape(1, d), b.reshape(1, d))


def _cmp_weights(w_cmp):
    dh = NSA_HEAD_DIM
    w6 = w_cmp.reshape(2, 2, CMP_STRIDE // 2, 2, dh, dh)
    eye = jnp.eye(2, dtype=w_cmp.dtype)
    r = jnp.einsum("chpsde,xy->cpsxdhye", w6, eye)
    return r.reshape(2, CMP_STRIDE // 2, 4 * dh, 4 * dh).astype(BF16)


def _cmp_body(*refs):
    x_refs, (r_ref, b_ref, o_ref) = refs[:-3], refs[-3:]
    row_w = 2 * NSA_KV_W

    def cols(c0):
        return jnp.concatenate([x_ref[0, :, c0:c0 + LANES] for x_ref in x_refs], axis=0)

    for c in range(2):
        for hp in range(2):
            acc = None
            for sp in range(CMP_STRIDE // 2):
                c0 = (2 * sp) * row_w + c * NSA_KV_W + hp * LANES
                lhs = jnp.concatenate([cols(c0), cols(c0 + row_w)], axis=1).astype(BF16)
                part = jnp.dot(lhs, r_ref[c, sp], preferred_element_type=F32)
                acc = part if acc is None else acc + part
            lo, hi = acc[:, :LANES], acc[:, LANES:]
            nxt = pltpu.roll(hi, hi.shape[0] - 1, 0)
            o0 = c * NSA_KV_W + hp * LANES
            o_ref[0, :, o0:o0 + LANES] = lo + nxt + b_ref[:, o0:o0 + LANES]


def nsa_compress(rows, r_w, b_cmp):
    bsz, t, w = rows.shape
    n_ch = t // CMP_STRIDE
    x = rows.reshape(bsz, n_ch, CMP_STRIDE * w)
    bias = jnp.broadcast_to(b_cmp[:, None, :], (2, NSA_KV_HEADS, NSA_HEAD_DIM)).reshape(1, w)
    return pl.pallas_call(
        _cmp_body,
        grid=(bsz,),
        in_specs=[pl.BlockSpec((1, n_ch, CMP_STRIDE * w), lambda b: (b, 0, 0)),
                  pl.BlockSpec(r_w.shape, lambda b: (0, 0, 0, 0)),
                  pl.BlockSpec((1, w), lambda b: (0, 0))],
        out_specs=pl.BlockSpec((1, n_ch, w), lambda b: (b, 0, 0)),
        out_shape=jax.ShapeDtypeStruct((bsz, n_ch, w), F32),
        compiler_params=_params("arbitrary"),
        name="nsa_compress",
    )(x, r_w, bias)


def _cmp_paged_body(pt_ref, *refs):
    page_refs, (r_ref, b_ref, o_ref, rows_scr) = refs[:-4], refs[-4:]
    page = page_refs[0].shape[-1]
    for pi, p_ref in enumerate(page_refs):
        for c in range(2):
            for hp in range(2):
                pair = p_ref[0, c, 2 * hp:2 * hp + 2].reshape(2 * NSA_HEAD_DIM, page)
                rows_scr[c * 2 + hp, pi * page:(pi + 1) * page, :] = pair.T
    n_ch = o_ref.shape[1]
    for c in range(2):
        for hp in range(2):
            acc = None
            for sp in range(CMP_STRIDE // 2):
                pos = [rows_scr[c * 2 + hp, pl.ds(2 * sp + sl, n_ch, stride=CMP_STRIDE), :] for sl in range(2)]
                part = jnp.dot(jnp.concatenate(pos, axis=1).astype(BF16), r_ref[c, sp], preferred_element_type=F32)
                acc = part if acc is None else acc + part
            lo, hi = acc[:, :LANES], acc[:, LANES:]
            nxt = pltpu.roll(hi, hi.shape[0] - 1, 0)
            o0 = c * NSA_KV_W + hp * LANES
            o_ref[0, :, o0:o0 + LANES] = lo + nxt + b_ref[:, o0:o0 + LANES]


def nsa_compress_paged(cache_t, page_table, r_w, b_cmp):
    n_phys, _, kvh, dh, page = cache_t.shape
    n, n_pages = page_table.shape
    w = 2 * kvh * dh
    n_ch = n_pages * page // CMP_STRIDE
    bias = jnp.broadcast_to(b_cmp[:, None, :], (2, kvh, dh)).reshape(1, w)
    page_spec = lambda k: pl.BlockSpec((1, 2, kvh, dh, page), lambda b, pt: (pt[b * n_pages + k], 0, 0, 0, 0))
    return pl.pallas_call(
        _cmp_paged_body,
        grid_spec=pltpu.PrefetchScalarGridSpec(
            num_scalar_prefetch=1,
            grid=(n,),
            in_specs=[page_spec(k) for k in range(n_pages)]
            + [pl.BlockSpec(r_w.shape, lambda b, pt: (0, 0, 0, 0)), pl.BlockSpec((1, w), lambda b, pt: (0, 0))],
            out_specs=pl.BlockSpec((1, n_ch, w), lambda b, pt: (b, 0, 0)),
            scratch_shapes=[pltpu.VMEM((4, n_pages * page, LANES), F32)]),
        out_shape=jax.ShapeDtypeStruct((n, n_ch, w), F32),
        compiler_params=_params("arbitrary"),
        name="nsa_compress_paged",
    )(page_table.reshape(-1), *([cache_t] * n_pages), r_w, bias)


def _softmax_cols(s, ok):
    s = jnp.where(ok, s, NEG)
    m = jnp.max(s, axis=0, keepdims=True)
    p = jnp.where(ok, jnp.exp(s - m), 0.0)
    l = jnp.sum(p, axis=0, keepdims=True)
    return p / jnp.where(l > 0.0, l, 1.0)


def _nsa_prompt_body(q_ref, kc_ref, vct_ref, ks_ref, vst_ref, kw_ref, vwt_ref, g_ref, o_ref, imp_ref, bias_ref, s_ref,
                     *, n_cmp, n_sel, i0, n_sel_tiles):
    i = i0 + pl.program_id(2)
    t0 = i * Q_TILE
    gq = NSA_GROUP * Q_TILE
    tq = t0 + lax.broadcasted_iota(I32, (1, Q_TILE), 1)
    tq4 = t0 + (lax.broadcasted_iota(I32, (1, gq), 1) & (Q_TILE - 1))
    q = q_ref[0, 0, 0]

    n_rows = kc_ref.shape[2]
    s = lax.dot_general(kc_ref[0, 0], q, _NT, preferred_element_type=F32)
    n_idx = lax.broadcasted_iota(I32, (n_rows, 1), 0)
    ok_c = (n_idx * CMP_STRIDE + (CMP_LEN - 1) <= tq4) & (n_idx < n_cmp)
    p_c = _softmax_cols(s, ok_c)
    o_c = jnp.dot(vct_ref[0, 0], p_c.astype(BF16), preferred_element_type=F32)

    imp = p_c[:, 0:Q_TILE]
    for g in range(1, NSA_GROUP):
        imp = imp + p_c[:, g * Q_TILE:(g + 1) * Q_TILE]
    imp_ref[...] = imp
    r_sel = SEL_BLOCK // CMP_STRIDE
    n_blk = n_rows // r_sel
    blk_imp = imp_ref[pl.ds(0, n_blk, stride=r_sel), :]
    for r in range(1, r_sel):
        blk_imp = blk_imp + imp_ref[pl.ds(r, n_blk, stride=r_sel), :]
    j_idx = lax.broadcasted_iota(I32, (n_blk, 1), 0)
    score = jnp.where(j_idx == tq // SEL_BLOCK, NSA_GROUP + 2.0,
                      jnp.where(j_idx == 0, NSA_GROUP + 1.0,
                                jnp.where(j_idx * SEL_BLOCK <= tq, blk_imp, -1.0)))
    score = jnp.where(j_idx < n_sel, score, -3.0)
    sel = jnp.zeros((n_blk, Q_TILE), F32)
    for _ in range(min(TOP_N, n_sel)):
        mx = jnp.max(score, axis=0, keepdims=True)
        first = jnp.min(jnp.where(score == mx, j_idx, n_blk), axis=0, keepdims=True)
        pick = j_idx == first
        sel = jnp.where(pick & (mx >= 0.0), 1.0, sel)
        score = jnp.where(pick, -2.0, score)
    blk_per_tile = KV_TILE // SEL_BLOCK
    for j in range(min(n_sel, n_sel_tiles * blk_per_tile)):
        bias_ref[j * SEL_BLOCK:(j + 1) * SEL_BLOCK, :] = jnp.broadcast_to(
            jnp.where(sel[j:j + 1, :] > 0.0, 0.0, NEG), (SEL_BLOCK, Q_TILE))

    def attend(k_ref, vt_ref, tiles, bias_fn):
        m = jnp.full((1, gq), NEG, F32)
        for n, kt in enumerate(tiles):
            k0 = kt * KV_TILE if isinstance(kt, int) else pl.multiple_of(kt * KV_TILE, KV_TILE)
            s = lax.dot_general(k_ref[0, 0, pl.ds(k0, KV_TILE), :], q, _NT, preferred_element_type=F32)
            kpos = k0 + lax.broadcasted_iota(I32, (KV_TILE, 1), 0)
            s = s + jnp.concatenate([bias_fn(kt, k0, kpos)] * NSA_GROUP, axis=1)
            s_ref[n * KV_TILE:(n + 1) * KV_TILE, :] = s
            m = jnp.maximum(m, jnp.max(s, axis=0, keepdims=True))
        acc = jnp.zeros((VT_ROWS, gq), F32)
        for n, kt in enumerate(tiles):
            p = jnp.exp(s_ref[n * KV_TILE:(n + 1) * KV_TILE, :] - m).astype(BF16)
            acc = acc + jnp.dot(vt_ref[0, 0, kt], p, preferred_element_type=F32)
        return acc[0:NSA_HEAD_DIM] / acc[NSA_HEAD_DIM:NSA_HEAD_DIM + 1]

    first_diag = (i0 * Q_TILE) // KV_TILE

    def sel_bias(kt, k0, kpos):
        bias = bias_ref[pl.ds(k0, KV_TILE), :]
        return bias if kt < first_diag else jnp.where(kpos <= tq, bias, NEG)

    def win_bias(kt, k0, kpos):
        dist = tq - kpos
        return jnp.where((dist >= 0) & (dist <= WINDOW), 0.0, NEG)

    o_s = attend(ks_ref, vst_ref, list(range(n_sel_tiles)), sel_bias)
    last = (t0 + Q_TILE - 1) // KV_TILE
    n_win_tiles = WINDOW // KV_TILE + 1
    if (i0 * Q_TILE + Q_TILE - 1) // KV_TILE < n_win_tiles - 1:
        win_tiles = list(range(n_sel_tiles))
    else:
        win_tiles = [last - (n_win_tiles - 1) + r for r in range(n_win_tiles)]
    o_w = attend(kw_ref, vwt_ref, win_tiles, win_bias)

    gate = 1.0 / (1.0 + jnp.exp(-g_ref[0, 0]))
    for g in range(NSA_GROUP):
        sl = slice(g * Q_TILE, (g + 1) * Q_TILE)
        o = gate[g, 0:1, :] * o_c[:, sl] + gate[g, 1:2, :] * o_s[:, sl] + gate[g, 2:3, :] * o_w[:, sl]
        o_ref[0, 0, g] = o.astype(o_ref.dtype)


VT_ROWS = NSA_HEAD_DIM + 16


def _head_major(kv):
    bsz, t, _ = kv.shape
    kv = kv.astype(BF16).reshape(bsz, t, 2, NSA_KV_HEADS, NSA_HEAD_DIM)
    k = kv[:, :, 0].transpose(0, 2, 1, 3)
    vt = kv[:, :, 1].reshape(bsz, t // KV_TILE, KV_TILE, NSA_KV_HEADS, NSA_HEAD_DIM).transpose(0, 3, 1, 4, 2)
    lead = vt.shape[:3]
    ones = jnp.ones(lead + (1, KV_TILE), BF16)
    zeros = jnp.zeros(lead + (VT_ROWS - NSA_HEAD_DIM - 1, KV_TILE), BF16)
    return k, jnp.concatenate([vt, ones, zeros], axis=3)


def nsa_prompt_attention(q, kvc, kv_s, kv_w, gates, n_cmp):
    bsz, t, _ = q.shape
    nq = t // Q_TILE
    n_sel = -(-t // SEL_BLOCK)
    n_rows = kvc.shape[1]
    dh, kvh, grp = NSA_HEAD_DIM, NSA_KV_HEADS, NSA_GROUP
    qh = q.reshape(bsz, nq, Q_TILE, kvh, grp, dh).transpose(0, 3, 1, 4, 2, 5).reshape(bsz, kvh, nq, grp * Q_TILE, dh)
    kc5 = kvc.astype(BF16).reshape(bsz, n_rows, 2, kvh, dh)
    kc = kc5[:, :, 0].transpose(0, 2, 1, 3)
    vct = kc5[:, :, 1].transpose(0, 2, 3, 1)
    ks, vst = _head_major(kv_s)
    kw, vwt = _head_major(kv_w)
    gt = gates[:, :, :NSA_HEADS * 3].reshape(bsz, t, kvh, grp, 3).transpose(0, 2, 3, 4, 1)
    nkt = t // KV_TILE
    assert WINDOW % KV_TILE == 0 and KV_TILE % Q_TILE == 0 and nq % Q_TILES_PER_CALL == 0
    bh = lambda b, h, i: (b, h, 0, 0)
    outs = []
    for i0 in range(0, nq, Q_TILES_PER_CALL):
        n_sel_tiles = ((i0 + Q_TILES_PER_CALL) * Q_TILE - 1) // KV_TILE + 1
        outs.append(pl.pallas_call(
            functools.partial(_nsa_prompt_body, n_cmp=n_cmp, n_sel=n_sel, i0=i0, n_sel_tiles=n_sel_tiles),
            grid=(bsz, kvh, Q_TILES_PER_CALL),
            in_specs=[pl.BlockSpec((1, 1, 1, grp * Q_TILE, dh), lambda b, h, i, i0=i0: (b, h, i0 + i, 0, 0)),
                      pl.BlockSpec((1, 1, n_rows, dh), bh),
                      pl.BlockSpec((1, 1, dh, n_rows), bh),
                      pl.BlockSpec((1, 1, t, dh), bh),
                      pl.BlockSpec((1, 1, nkt, VT_ROWS, KV_TILE), lambda b, h, i: (b, h, 0, 0, 0)),
                      pl.BlockSpec((1, 1, t, dh), bh),
                      pl.BlockSpec((1, 1, nkt, VT_ROWS, KV_TILE), lambda b, h, i: (b, h, 0, 0, 0)),
                      pl.BlockSpec((1, 1, grp, 3, Q_TILE), lambda b, h, i, i0=i0: (b, h, 0, 0, i0 + i))],
            out_specs=pl.BlockSpec((1, 1, grp, dh, Q_TILE), lambda b, h, i: (b, h, 0, 0, i)),
            out_shape=jax.ShapeDtypeStruct((bsz, kvh, grp, dh, Q_TILES_PER_CALL * Q_TILE), BF16),
            scratch_shapes=[pltpu.VMEM((n_rows, Q_TILE), F32), pltpu.VMEM((n_sel * SEL_BLOCK, Q_TILE), F32),
                            pltpu.VMEM((max(n_sel_tiles, WINDOW // KV_TILE + 1) * KV_TILE, grp * Q_TILE), F32)],
            compiler_params=_params("arbitrary", "arbitrary", "arbitrary"),
            name=f"nsa_prompt_attention_q{i0}",
        )(qh, kc, vct, ks, vst, kw, vwt, gt))
    out = jnp.concatenate(outs, axis=-1)
    return out.transpose(0, 4, 1, 2, 3).reshape(bsz, t, kvh * grp * dh)


def _nsa_w_in_padded(w_in):
    wq = w_in[:, :NSA_Q_W] * NSA_HEAD_DIM ** -0.5
    pad = jnp.zeros((w_in.shape[0], LANES - 3 * NSA_HEADS), w_in.dtype)
    return jnp.concatenate([wq, w_in[:, NSA_Q_W:], pad], axis=1).astype(BF16)


_NSA_SPLITS = (NSA_Q_W, 2 * NSA_KV_W, 2 * NSA_KV_W, 2 * NSA_KV_W, LANES)
_NSA_DTYPES = (BF16, F32, F32, F32, F32)


def nsa_prompt_mixer(x, w_in_p, r_cmp, b_cmp):
    bsz, t, d = x.shape
    q, kv_c, kv_s, kv_w, gates = matmul_split(x.reshape(bsz * t, d), w_in_p, _NSA_SPLITS, _NSA_DTYPES, tm=min(512, bsz * t))
    r3 = lambda a: a.reshape(bsz, t, a.shape[-1])
    kv_c, kv_s, kv_w = r3(kv_c), r3(kv_s), r3(kv_w)
    n_cmp = (t - CMP_LEN) // CMP_STRIDE + 1
    kvc = nsa_compress(kv_c, r_cmp, b_cmp)
    o = nsa_prompt_attention(r3(q), kvc, kv_s, kv_w, r3(gates), n_cmp)
    return o.reshape(bsz * t, NSA_Q_W), kv_c, kv_s, kv_w


def _ret_log_decay():
    return jnp.log(1.0 - 2.0 ** (-5.0 - jnp.arange(RET_HEADS, dtype=F32)))


def _rope_tables(pos):
    half = RET_DK // 2
    inv = ROPE_BASE ** (-jnp.linspace(0.0, 1.0, half, dtype=F32))
    ang = pos.astype(F32)[:, None] * inv[None, :]
    return jnp.cos(ang), jnp.sin(ang)


def _rope(x, cos, sin):
    half = RET_DK // 2
    x1, x2 = x[:, :half], x[:, half:]
    return jnp.concatenate([x1 * cos - x2 * sin, x2 * cos + x1 * sin], axis=1)


def _group_norm_gate(o, gate, g, b):
    mu = jnp.mean(o, axis=-1, keepdims=True)
    c = o - mu
    var = jnp.mean(c * c, axis=-1, keepdims=True)
    on = c * lax.rsqrt(var + LN_EPS) * g + b
    return gate * (1.0 / (1.0 + jnp.exp(-gate))) * on


def _ret_prompt_body(q_ref, k_ref, v_ref, gt_ref, cos_ref, sin_ref, dm_ref, di_ref, dr_ref, dc_ref, gg_ref, gb_ref,
                     o_ref, s_ref, s_scr):
    c = pl.program_id(1)

    @pl.when(c == 0)
    def _():
        s_scr[...] = jnp.zeros_like(s_scr)

    cos, sin = cos_ref[...], sin_ref[...]
    for h in range(RET_HEADS):
        qk = slice(h * RET_DK, (h + 1) * RET_DK)
        vv = slice(h * RET_DV, (h + 1) * RET_DV)
        q = _rope(q_ref[:, qk], cos, sin)
        k = _rope(k_ref[:, qk], cos, sin) * RET_DK ** -0.5
        qb, v = q.astype(BF16), v_ref[:, vv]
        inner = lax.dot_general(qb, k.astype(BF16), _NT, preferred_element_type=F32) * dm_ref[h]
        o = jnp.dot(inner.astype(BF16), v, preferred_element_type=F32)
        s_old = s_scr[h]
        o = o + jnp.dot(qb, s_old.astype(BF16), preferred_element_type=F32) * di_ref[h]
        kd = (k * dr_ref[h]).astype(BF16)
        s_new = dc_ref[h] * s_old + lax.dot_general(kd, v, _TN, preferred_element_type=F32)
        s_scr[h] = s_new
        s_ref[0, h] = s_new
        o_ref[:, vv] = _group_norm_gate(o, gt_ref[:, vv], gg_ref[:, vv], gb_ref[:, vv]).astype(o_ref.dtype)


def retention_prompt(q, k, v, gate, gn_g, gn_b, bsz, t):
    ch = RET_CHUNK
    n_ch = t // ch
    lg = _ret_log_decay()
    i = jnp.arange(ch, dtype=F32)
    diff = i[:, None] - i[None, :]
    dmask = jnp.where(diff >= 0, jnp.exp(lg[:, None, None] * jnp.maximum(diff, 0.0)), 0.0)
    d_in = jnp.exp((i[None, :] + 1.0) * lg[:, None])[:, :, None]
    d_rev = jnp.exp((ch - 1.0 - i)[None, :] * lg[:, None])[:, :, None]
    d_c = jnp.exp(ch * lg)[:, None, None]
    cos, sin = _rope_tables(jnp.arange(t))
    half = RET_DK // 2
    nh = RET_HEADS
    rows = lambda w: pl.BlockSpec((ch, w), lambda b, c: (b * n_ch + c, 0))
    whole = lambda a: pl.BlockSpec(a.shape, lambda b, c: (0,) * a.ndim)
    consts = (dmask, d_in, d_rev, d_c, gn_g.reshape(1, -1), gn_b.reshape(1, -1))
    return pl.pallas_call(
        _ret_prompt_body,
        grid=(bsz, n_ch),
        in_specs=[rows(nh * RET_DK), rows(nh * RET_DK), rows(nh * RET_DV), rows(nh * RET_DV),
                  pl.BlockSpec((ch, half), lambda b, c: (c, 0)), pl.BlockSpec((ch, half), lambda b, c: (c, 0))]
        + [whole(a) for a in consts],
        out_specs=[rows(nh * RET_DV), pl.BlockSpec((1, nh, RET_DK, RET_DV), lambda b, c: (b, 0, 0, 0))],
        out_shape=[jax.ShapeDtypeStruct((bsz * t, nh * RET_DV), BF16),
                   jax.ShapeDtypeStruct((bsz, nh, RET_DK, RET_DV), F32)],
        scratch_shapes=[pltpu.VMEM((nh, RET_DK, RET_DV), F32)],
        compiler_params=_params("arbitrary", "arbitrary"),
        name="retention_prompt",
    )(q, k, v, gate, cos, sin, *consts)


_RET_SPLITS = (RET_HEADS * RET_DK, RET_HEADS * RET_DK, RET_HEADS * RET_DV, RET_HEADS * RET_DV)
_RET_DTYPES = (F32, F32, BF16, F32)


def _ret_sample_body(q_ref, k_ref, v_ref, gt_ref, cos_ref, sin_ref, dec_ref, gg_ref, gb_ref, s_in_ref,
                     o_ref, s_out_ref):
    cos, sin = cos_ref[...], sin_ref[...]
    row0 = lax.broadcasted_iota(I32, (8, 1), 0) == 0
    for h in range(RET_HEADS):
        qh = _rope(q_ref[0, :, h * RET_DK:(h + 1) * RET_DK], cos, sin)
        kh = _rope(k_ref[0, :, h * RET_DK:(h + 1) * RET_DK], cos, sin) * RET_DK ** -0.5
        vh = v_ref[0, :, h * RET_DV:(h + 1) * RET_DV]
        qb, kb = qh.astype(BF16), kh.astype(BF16)
        dec = dec_ref[h]
        s_old = s_in_ref[0, h]
        q8 = jnp.broadcast_to(qb, (8, RET_DK))
        cross = jnp.dot(q8, s_old.astype(BF16), preferred_element_type=F32)[0:1] * dec
        inner = jnp.sum(qb.astype(F32) * kb.astype(F32), axis=1, keepdims=True)
        o = inner.astype(BF16).astype(F32) * vh.astype(F32) + cross
        k8 = jnp.where(row0, jnp.broadcast_to(kb.astype(F32), (8, RET_DK)), 0.0).astype(BF16)
        v8 = jnp.broadcast_to(vh, (8, RET_DV))
        s_out_ref[0, h] = dec * s_old + lax.dot_general(k8, v8, _TN, preferred_element_type=F32)
        sl = slice(h * RET_DV, (h + 1) * RET_DV)
        o_ref[0, :, sl] = _group_norm_gate(o, gt_ref[0, :, sl], gg_ref[:, sl], gb_ref[:, sl]).astype(o_ref.dtype)


def retention_sample(q, k, v, gate, gn_g, gn_b, state, pos0):
    n = q.shape[0]
    cos, sin = _rope_tables(jnp.full((1,), pos0))
    dec = jnp.exp(_ret_log_decay())
    r3 = lambda a: a.reshape(n, 1, a.shape[-1])
    w_qk, w_v = RET_HEADS * RET_DK, RET_HEADS * RET_DV
    half = RET_DK // 2
    vec = lambda w: pl.BlockSpec((1, 1, w), lambda b: (b, 0, 0))
    og, s_new = pl.pallas_call(
        _ret_sample_body,
        grid=(n,),
        in_specs=[vec(w_qk), vec(w_qk), vec(w_v), vec(w_v),
                  pl.BlockSpec((1, half), lambda b: (0, 0)), pl.BlockSpec((1, half), lambda b: (0, 0)),
                  pl.BlockSpec(memory_space=pltpu.SMEM),
                  pl.BlockSpec((1, w_v), lambda b: (0, 0)), pl.BlockSpec((1, w_v), lambda b: (0, 0)),
                  pl.BlockSpec((1, RET_HEADS, RET_DK, RET_DV), lambda b: (b, 0, 0, 0))],
        out_specs=[vec(w_v), pl.BlockSpec((1, RET_HEADS, RET_DK, RET_DV), lambda b: (b, 0, 0, 0))],
        out_shape=[jax.ShapeDtypeStruct((n, 1, w_v), BF16), jax.ShapeDtypeStruct(state.shape, F32)],
        compiler_params=_params("arbitrary"),
        name="retention_sample",
    )(r3(q), r3(k), r3(v), r3(gate), cos, sin, dec, gn_g.reshape(1, -1), gn_b.reshape(1, -1), state)
    return og.reshape(n, w_v), s_new


POOL_HALO = 16


def _pool_mix(win_sum_fn, x, pos, w_ref, sc_ref, g_ref, b_ref):
    ys = []
    for gi, w in enumerate(POOL_WINDOWS):
        sl = slice(gi * POOL_GROUP, (gi + 1) * POOL_GROUP)
        pooled = win_sum_fn(gi, w) / jnp.minimum(float(w), pos + 1.0) - x[:, sl]
        ys.append(jnp.dot(pooled.astype(BF16), w_ref[gi], preferred_element_type=F32))
    y = jnp.concatenate(ys, axis=1) * sc_ref[...]
    return _layer_norm(ALPHA * x + y, g_ref[...], b_ref[...])


def _pool_prompt_body(x_ref, w_ref, sc_ref, g_ref, b_ref, o_ref, xh_scr, *, tt):
    j = pl.program_id(1)

    @pl.when(j == 0)
    def _():
        xh_scr[0:POOL_HALO, :] = jnp.zeros((POOL_HALO, D_MODEL), F32)

    x = x_ref[0]
    xh_scr[POOL_HALO:, :] = x
    pos = (j * tt + lax.broadcasted_iota(I32, (tt, 1), 0)).astype(F32)

    def win_sum(gi, w):
        c0 = gi * POOL_GROUP
        acc = x[:, c0:c0 + POOL_GROUP]
        for u in range(1, w):
            acc = acc + xh_scr[POOL_HALO - u:POOL_HALO - u + tt, c0:c0 + POOL_GROUP]
        return acc

    o_ref[0] = _pool_mix(win_sum, x, pos, w_ref, sc_ref, g_ref, b_ref)
    xh_scr[0:POOL_HALO, :] = x[tt - POOL_HALO:, :]


def pool_prompt(x, w_pool, scale, g, b, tt=256):
    bsz, t, d = x.shape
    tt = min(tt, t)
    row = lambda b_, j: (0, 0)
    return pl.pallas_call(
        functools.partial(_pool_prompt_body, tt=tt),
        grid=(bsz, t // tt),
        in_specs=[pl.BlockSpec((1, tt, d), lambda b_, j: (b_, j, 0)),
                  pl.BlockSpec(w_pool.shape, lambda b_, j: (0, 0, 0)),
                  pl.BlockSpec((1, d), row), pl.BlockSpec((1, d), row), pl.BlockSpec((1, d), row)],
        out_specs=pl.BlockSpec((1, tt, d), lambda b_, j: (b_, j, 0)),
        out_shape=jax.ShapeDtypeStruct(x.shape, F32),
        scratch_shapes=[pltpu.VMEM((POOL_HALO + tt, d), F32)],
        compiler_params=_params("arbitrary", "arbitrary"),
        name="pool_prompt",
    )(x, w_pool, scale.reshape(1, d), g.reshape(1, d), b.reshape(1, d))


def _pool_sample_body(x_ref, st_ref, w_ref, sc_ref, g_ref, b_ref, o_ref, *, pos0):
    x = x_ref[...]
    n_hist = st_ref.shape[0]

    def win_sum(gi, w):
        c0 = gi * POOL_GROUP
        acc = x[:, c0:c0 + POOL_GROUP]
        for u in range(1, w):
            acc = acc + st_ref[n_hist - u, :, c0:c0 + POOL_GROUP]
        return acc

    pos = jnp.full((x.shape[0], 1), float(pos0), F32)
    o_ref[...] = _pool_mix(win_sum, x, pos, w_ref, sc_ref, g_ref, b_ref)


def pool_sample(x, hist, w_pool, scale, g, b, pos0):
    n, d = x.shape
    whole = lambda a: pl.BlockSpec(a.shape, lambda i: (0,) * a.ndim)
    args = (x, hist, w_pool, scale.reshape(1, d), g.reshape(1, d), b.reshape(1, d))
    return pl.pallas_call(
        functools.partial(_pool_sample_body, pos0=pos0),
        grid=(1,),
        in_specs=[whole(a) for a in args],
        out_specs=pl.BlockSpec((n, d), lambda i: (0, 0)),
        out_shape=jax.ShapeDtypeStruct((n, d), F32),
        compiler_params=_params("arbitrary"),
        name="pool_sample",
    )(*args)


CONV_HALO = 8


def _conv_prompt_body(bg_ref, cg_ref, h_ref, wc_ref, a_ref, tail_ref, uh_scr, *, tt):
    j = pl.program_id(1)

    @pl.when(j == 0)
    def _():
        uh_scr[0:CONV_HALO, :] = jnp.zeros((CONV_HALO, D_MODEL), F32)

    u = cg_ref[...] * h_ref[...]
    uh_scr[CONV_HALO:, :] = u
    conv = wc_ref[CONV_W - 1:CONV_W, :] * u
    for jj in range(CONV_W - 1):
        back = CONV_W - 1 - jj
        conv = conv + wc_ref[jj:jj + 1, :] * uh_scr[CONV_HALO - back:CONV_HALO - back + tt, :]
    a_ref[...] = (bg_ref[...] * conv).astype(a_ref.dtype)
    uh_scr[0:CONV_HALO, :] = u[tt - CONV_HALO:, :]
    tail_ref[0] = u[tt - CONV_HALO:, :]


def conv_prompt(bg, cg, h, w_conv, bsz, t, tt=256):
    d = bg.shape[1]
    tt = min(tt, t)
    nt = t // tt
    blk = pl.BlockSpec((tt, d), lambda b_, j: (b_ * nt + j, 0))
    return pl.pallas_call(
        functools.partial(_conv_prompt_body, tt=tt),
        grid=(bsz, nt),
        in_specs=[blk, blk, blk, pl.BlockSpec((CONV_W, d), lambda b_, j: (0, 0))],
        out_specs=[blk, pl.BlockSpec((1, CONV_HALO, d), lambda b_, j: (b_, 0, 0))],
        out_shape=[jax.ShapeDtypeStruct((bsz * t, d), BF16), jax.ShapeDtypeStruct((bsz, CONV_HALO, d), F32)],
        scratch_shapes=[pltpu.VMEM((CONV_HALO + tt, d), F32)],
        compiler_params=_params("arbitrary", "arbitrary"),
        name="conv_prompt",
    )(bg, cg, h, w_conv)


def _conv_sample_body(bg_ref, cg_ref, h_ref, prev_ref, wc_ref, a_ref, u_ref):
    u = cg_ref[...] * h_ref[...]
    conv = wc_ref[CONV_W - 1:CONV_W, :] * u
    for jj in range(CONV_W - 1):
        conv = conv + wc_ref[jj:jj + 1, :] * prev_ref[jj]
    a_ref[...] = (bg_ref[...] * conv).astype(a_ref.dtype)
    u_ref[...] = u


def conv_sample(bg, cg, h, prev, w_conv):
    n, d = bg.shape
    whole = lambda a: pl.BlockSpec(a.shape, lambda i: (0,) * a.ndim)
    args = (bg, cg, h, prev, w_conv)
    return pl.pallas_call(
        _conv_sample_body,
        grid=(1,),
        in_specs=[whole(a) for a in args],
        out_specs=[pl.BlockSpec((n, d), lambda i: (0, 0)), pl.BlockSpec((n, d), lambda i: (0, 0))],
        out_shape=[jax.ShapeDtypeStruct((n, d), BF16), jax.ShapeDtypeStruct((n, d), F32)],
        compiler_params=_params("arbitrary"),
        name="conv_sample",
    )(*args)


_CONV_SPLITS = (D_MODEL, D_MODEL, D_MODEL)
_CONV_DTYPES = (F32, F32, F32)


ROUTER_LANES = LANES
EXPERT_LANE0 = MOE_GROUPS


COMB_LANE0 = 8


def _router_body(x_ref, w_ref, b_ref, o_ref, cnt_ref, run_scr):
    i = pl.program_id(0)

    @pl.when(i == 0)
    def _():
        run_scr[...] = jnp.zeros_like(run_scr)

    x = x_ref[...]
    x_hi = x.astype(BF16)
    x_lo = (x - x_hi.astype(F32)).astype(BF16)
    logits = (jnp.dot(x_hi, w_ref[0], preferred_element_type=F32) + jnp.dot(x_lo, w_ref[0], preferred_element_type=F32)
              + jnp.dot(x_hi, w_ref[1], preferred_element_type=F32) + b_ref[...])
    tm = logits.shape[0]
    lane = lax.broadcasted_iota(I32, logits.shape, 1)
    big = ROUTER_LANES

    def top1(mask):
        v = jnp.max(jnp.where(mask, logits, -jnp.inf), axis=1, keepdims=True)
        idx = jnp.min(jnp.where(mask & (logits == v), lane, big), axis=1, keepdims=True)
        return v, idx

    is_g = lane < MOE_GROUPS
    vg, gsel = top1(is_g)
    pg_sel = 1.0 / jnp.sum(jnp.where(is_g, jnp.exp(logits - vg), 0.0), axis=1, keepdims=True)
    e0 = EXPERT_LANE0 + gsel * MOE_EPG
    is_e = (lane >= e0) & (lane < e0 + MOE_EPG)
    v1, i1 = top1(is_e)
    v2, i2 = top1(is_e & (lane != i1))
    r = jnp.exp(v2 - v1)
    pe1 = pg_sel / (1.0 + r)
    pe2 = pg_sel * r / (1.0 + r)
    local = lane - COMB_LANE0
    comb = jnp.where(local == i1 - e0, pe1, 0.0) + jnp.where(local == i2 - e0, pe2, 0.0)

    onehot = jnp.where(lane == gsel, 1.0, 0.0)
    rows = lax.broadcasted_iota(I32, (tm, tm), 0)
    cols = lax.broadcasted_iota(I32, (tm, tm), 1)
    before = jnp.where(cols < rows, 1.0, 0.0).astype(BF16)
    prefix = jnp.dot(before, onehot.astype(BF16), preferred_element_type=F32)
    run = run_scr[0:1, :]
    rank = jnp.sum(jnp.where(lane == gsel, prefix + run, 0.0), axis=1, keepdims=True)
    run = run + jnp.sum(onehot, axis=0, keepdims=True)
    run_scr[0:1, :] = run
    cnt_ref[...] = jnp.broadcast_to(run, cnt_ref.shape)
    o_ref[...] = jnp.where(lane == 0, gsel.astype(F32), jnp.where(lane == 1, rank, comb))


def moe_router(x, w_r, b_r, tm):
    m, d = x.shape
    return pl.pallas_call(
        _router_body,
        grid=(m // tm,),
        in_specs=[pl.BlockSpec((tm, d), lambda i: (i, 0)), pl.BlockSpec((2, d, ROUTER_LANES), lambda i: (0, 0, 0)),
                  pl.BlockSpec((1, ROUTER_LANES), lambda i: (0, 0))],
        out_specs=[pl.BlockSpec((tm, ROUTER_LANES), lambda i: (i, 0)),
                   pl.BlockSpec((8, ROUTER_LANES), lambda i: (0, 0))],
        out_shape=[jax.ShapeDtypeStruct((m, ROUTER_LANES), F32), jax.ShapeDtypeStruct((8, ROUTER_LANES), F32)],
        scratch_shapes=[pltpu.VMEM((8, ROUTER_LANES), F32)],
        compiler_params=_params("arbitrary"),
        name="moe_router",
    )(x, w_r, b_r)


def _router_weights(w_rg, b_rg, w_re, b_re):
    d = w_rg.shape[0]
    pad = ROUTER_LANES - MOE_GROUPS - MOE_EXPERTS
    w = jnp.concatenate([w_rg, w_re, jnp.zeros((d, pad), F32)], axis=1)
    b = jnp.concatenate([b_rg, b_re, jnp.zeros((pad,), F32)]).reshape(1, ROUTER_LANES)
    w_hi = w.astype(BF16)
    w_lo = (w - w_hi.astype(F32)).astype(BF16)
    return jnp.stack([w_hi, w_lo]), b


def _moe_ffn_body(gid_ref, x_ref, c_ref, w1_ref, w3_ref, w2_ref, g_ref, b_ref, o_ref, w1b, w3b, w2b):
    i = pl.program_id(0)

    @pl.when((i == 0) | (gid_ref[i] != gid_ref[jnp.maximum(i - 1, 0)]))
    def _():
        for e in range(MOE_EPG):
            w1b[e] = w1_ref[0, e].astype(BF16)
            w3b[e] = w3_ref[0, e].astype(BF16)
            w2b[e] = w2_ref[0, e].astype(BF16)

    x = x_ref[...]
    xb = x.astype(BF16)
    comb = c_ref[...]
    y = jnp.zeros(x.shape, F32)
    for e in range(MOE_EPG):
        a = jnp.dot(xb, w1b[e], preferred_element_type=F32)
        gte = jnp.dot(xb, w3b[e], preferred_element_type=F32)
        hcol = a * (1.0 / (1.0 + jnp.exp(-a))) * gte * comb[:, e:e + 1]
        y = y + jnp.dot(hcol.astype(BF16), w2b[e], preferred_element_type=F32)
    o_ref[...] = _layer_norm(ALPHA * x + y, g_ref[...], b_ref[...])


def moe_layer(x, router_w, router_b, w1, w3, w2, layer, g, b, tm, router_tm):
    m, d = x.shape
    r, cnt = moe_router(x, router_w, router_b, tm=router_tm)
    gid, rank = r[:, 0].astype(I32), r[:, 1].astype(I32)
    comb = r[:, COMB_LANE0:COMB_LANE0 + MOE_EPG]
    counts = cnt[0, :MOE_GROUPS].astype(I32)
    tiles = (counts + tm - 1) // tm
    tile_end = jnp.cumsum(tiles)
    start = (tile_end - tiles) * tm
    dest = rank
    for grp in range(MOE_GROUPS):
        dest = dest + jnp.where(gid == grp, start[grp], 0)
    n_tiles = -(-m // tm) + MOE_GROUPS
    mp = n_tiles * tm
    src = jnp.full((mp,), -1, I32).at[dest].set(jnp.arange(m, dtype=I32))
    live = (src >= 0).astype(F32)
    src = jnp.maximum(src, 0)
    xs = jnp.take(x, src, axis=0, mode="clip")
    cs = jnp.take(comb, src, axis=0, mode="clip") * live[:, None]
    tile_gid = jnp.minimum((jnp.arange(n_tiles, dtype=I32)[:, None] >= tile_end[None, :]).sum(axis=1),
                           MOE_GROUPS - 1).astype(I32)
    wspec = lambda shp: pl.BlockSpec((1,) + shp, lambda i, gid_: (layer, gid_[i], 0, 0), pipeline_mode=pl.Buffered(1))
    out = pl.pallas_call(
        _moe_ffn_body,
        grid_spec=pltpu.PrefetchScalarGridSpec(
            num_scalar_prefetch=1,
            grid=(n_tiles,),
            in_specs=[pl.BlockSpec((tm, d), lambda i, gid_: (i, 0)),
                      pl.BlockSpec((tm, MOE_EPG), lambda i, gid_: (i, 0)),
                      wspec((MOE_EPG, d, MOE_FF)), wspec((MOE_EPG, d, MOE_FF)), wspec((MOE_EPG, MOE_FF, d)),
                      pl.BlockSpec((1, d), lambda i, gid_: (0, 0)), pl.BlockSpec((1, d), lambda i, gid_: (0, 0))],
            out_specs=pl.BlockSpec((tm, d), lambda i, gid_: (i, 0)),
            scratch_shapes=[pltpu.VMEM((MOE_EPG, d, MOE_FF), BF16), pltpu.VMEM((MOE_EPG, d, MOE_FF), BF16),
                            pltpu.VMEM((MOE_EPG, MOE_FF, d), BF16)]),
        out_shape=jax.ShapeDtypeStruct((mp, d), F32),
        compiler_params=_params("arbitrary"),
        name="moe_ffn",
    )(tile_gid, xs, cs, w1, w3, w2, g.reshape(1, d), b.reshape(1, d))
    return out, dest


NSA_ROWS = NSA_HEADS
SEL_SAMPLES = 8


def _softmax_rows(s, ok):
    s = jnp.where(ok, s, NEG)
    m = jnp.max(s, axis=1, keepdims=True)
    p = jnp.where(ok, jnp.exp(s - m), 0.0)
    l = jnp.sum(p, axis=1, keepdims=True)
    return p / jnp.where(l > 0.0, l, 1.0)


def _nsa_sample_select_body(qx_ref, kvc_ref, pool_ref, idx_ref, *, n_cmp, n_sel, q_pos):
    n_rows = kvc_ref.shape[1]
    lane = lax.broadcasted_iota(I32, (NSA_ROWS, n_rows), 1)
    ok_c = (lane * CMP_STRIDE + (CMP_LEN - 1) <= q_pos) & (lane < n_cmp)
    for s_i in range(SEL_SAMPLES):
        kc = kvc_ref[s_i, :, 0:NSA_KV_W].astype(BF16)
        s = lax.dot_general(qx_ref[s_i], kc, _NT, preferred_element_type=F32)
        p = _softmax_rows(s, ok_c)
        imp = p
        for g in range(1, NSA_GROUP):
            imp = imp + pltpu.roll(p, g * NSA_KV_HEADS, 0)
        blk = jnp.dot(imp, pool_ref[...], preferred_element_type=F32, precision=lax.Precision.HIGHEST)
        score = jnp.where(lane == q_pos // SEL_BLOCK, NSA_GROUP + 2.0,
                          jnp.where(lane == 0, NSA_GROUP + 1.0,
                                    jnp.where(lane * SEL_BLOCK <= q_pos, blk, -1.0)))
        score = jnp.where(lane < n_sel, score, -3.0)
        idx = jnp.full(score.shape, -1, I32)
        for k in range(min(TOP_N, n_sel)):
            mx = jnp.max(score, axis=1, keepdims=True)
            first = jnp.min(jnp.where(score == mx, lane, n_rows), axis=1, keepdims=True)
            idx = jnp.where(lane == k, jnp.where(mx >= 0.0, first, -1), idx)
            score = jnp.where(lane == first, -2.0, score)
        idx_ref[s_i] = idx[0:8, :]


def nsa_sample_select(qx, kvc, n_cmp, n_sel, q_pos):
    n, n_rows, w = kvc.shape
    r_sel = SEL_BLOCK // CMP_STRIDE
    pool = (jnp.arange(n_rows)[:, None] // r_sel == jnp.arange(n_rows)[None, :]).astype(F32)
    return pl.pallas_call(
        functools.partial(_nsa_sample_select_body, n_cmp=n_cmp, n_sel=n_sel, q_pos=q_pos),
        grid=(n // SEL_SAMPLES,),
        in_specs=[pl.BlockSpec((SEL_SAMPLES, NSA_ROWS, NSA_KV_W), lambda i: (i, 0, 0)),
                  pl.BlockSpec((SEL_SAMPLES, n_rows, w), lambda i: (i, 0, 0)),
                  pl.BlockSpec((n_rows, n_rows), lambda i: (0, 0))],
        out_specs=pl.BlockSpec((SEL_SAMPLES, 8, n_rows), lambda i: (i, 0, 0)),
        out_shape=jax.ShapeDtypeStruct((n, 8, n_rows), I32),
        compiler_params=_params("arbitrary"),
        name="nsa_sample_select",
    )(qx, kvc, pool)


Q_ROWS = 8


def _nsa_sample_attend_body(page_ref, half_ref, qh_ref, kvc_ref, win_ref, ksn_ref, kwn_ref, g_ref, *rest, n_cmp, q_pos):
    blk_refs, o_ref = rest[:-1], rest[-1]
    b = pl.program_id(0)
    dh = NSA_HEAD_DIM
    rnd = lambda a: a.astype(BF16).astype(F32)
    n_top = len(blk_refs) // NSA_KV_HEADS
    page = blk_refs[0].shape[-1]
    n_rows = kvc_ref.shape[1]
    lane_c = lax.broadcasted_iota(I32, (Q_ROWS, n_rows), 1)
    ok_c = (lane_c * CMP_STRIDE + (CMP_LEN - 1) <= q_pos) & (lane_c < n_cmp)
    lane_s = lax.broadcasted_iota(I32, (Q_ROWS, n_top * page), 1)
    gate = 1.0 / (1.0 + jnp.exp(-g_ref[0]))

    for h in range(NSA_KV_HEADS):
        q = qh_ref[0, h]
        qf = q.astype(F32)

        def with_new_key(k_t, v_t, ok, new_ref):
            s = jnp.dot(q, k_t, preferred_element_type=F32)
            s_new = jnp.sum(qf * rnd(new_ref[0, :, h * dh:(h + 1) * dh]), axis=1, keepdims=True)
            if ok is not None:
                s = jnp.where(ok, s, NEG)
            m = jnp.maximum(jnp.max(s, axis=1, keepdims=True), s_new)
            p = jnp.exp(s - m) if ok is None else jnp.where(ok, jnp.exp(s - m), 0.0)
            p_new = jnp.exp(s_new - m)
            l = jnp.sum(p, axis=1, keepdims=True) + p_new
            v_new = new_ref[0, :, NSA_KV_W + h * dh:NSA_KV_W + (h + 1) * dh]
            o = lax.dot_general(p.astype(BF16), v_t, _NT, preferred_element_type=F32)
            return (o + rnd(p_new) * rnd(v_new)) / l

        kc = kvc_ref[0, :, h * dh:(h + 1) * dh].astype(BF16)
        vc = kvc_ref[0, :, NSA_KV_W + h * dh:NSA_KV_W + (h + 1) * dh].astype(BF16)
        s = lax.dot_general(q, kc, _NT, preferred_element_type=F32)
        o_c = jnp.dot(_softmax_rows(s, ok_c).astype(BF16), vc, preferred_element_type=F32)

        o_w = with_new_key(win_ref[0, 0, h].astype(BF16), win_ref[0, 1, h].astype(BF16), None, kwn_ref)

        refs = blk_refs[h * n_top:(h + 1) * n_top]
        k_t = jnp.concatenate([r[0, 0, 0] for r in refs], axis=1).astype(BF16)
        v_t = jnp.concatenate([r[0, 1, 0] for r in refs], axis=1).astype(BF16)
        want = jnp.full(lane_s.shape, -1, I32)
        for k in range(n_top):
            want = jnp.where(lane_s // page == k, half_ref[(b * NSA_KV_HEADS + h) * n_top + k], want)
        o_s = with_new_key(k_t, v_t, (lane_s % page) // SEL_BLOCK == want, ksn_ref)

        o = gate[h, :, 0:1] * o_c + gate[h, :, 1:2] * o_s + gate[h, :, 2:3] * o_w
        o_ref[0, h] = o.astype(o_ref.dtype)


def nsa_sample_mixer(x, cache_cmp_t, cache_sel_t, win_t, page_table, w_in_p, r_cmp, b_cmp):
    n, d = x.shape
    n_pages = page_table.shape[1]
    past = n_pages * PAGE_SIZE
    dh, kvh, grp = NSA_HEAD_DIM, NSA_KV_HEADS, NSA_GROUP
    q, kv_c, kv_s, kv_w, gates = matmul_split(x, w_in_p, _NSA_SPLITS, _NSA_DTYPES, tm=n)
    n_cmp = (past + 1 - CMP_LEN) // CMP_STRIDE + 1
    n_sel = -(-(past + 1) // SEL_BLOCK)
    n_past_blk = past // SEL_BLOCK
    kvc = nsa_compress_paged(cache_cmp_t, page_table, r_cmp, b_cmp)
    q4 = q.reshape(n, kvh, grp, dh)
    q4t = q4.transpose(0, 2, 1, 3)
    qx = (q4t[:, :, :, None, :] * jnp.eye(kvh, dtype=q.dtype)[None, None, :, :, None]).reshape(n, NSA_ROWS, kvh * dh)
    idx = nsa_sample_select(qx, kvc, n_cmp, n_sel, past)[:, :kvh, :TOP_N]
    sub = PAGE_SIZE // SEL_BLOCK
    is_past = (idx >= 0) & (idx < n_past_blk)
    phys = jnp.take_along_axis(page_table, jnp.clip(idx // sub, 0, n_pages - 1).reshape(n, -1), axis=1).reshape(idx.shape)
    page_idx = jnp.where(is_past, phys, 0).astype(I32).reshape(-1)
    half = jnp.where(is_past, idx % sub, -1).astype(I32).reshape(-1)
    pad_rows = lambda a: jnp.pad(a, ((0, 0), (0, 0), (0, Q_ROWS - grp), (0, 0)))
    qh = pad_rows(q4)
    gt = pad_rows(gates[:, :NSA_HEADS * 3].reshape(n, kvh, grp, 3))
    n_top = idx.shape[2]
    per_seq = kvh * n_top
    vec = lambda a: a.reshape(n, 1, a.shape[-1])
    blk_spec = lambda k: pl.BlockSpec((1, 2, 1, dh, PAGE_SIZE),
                                      lambda b, p_, h_: (p_[b * per_seq + k], 0, k // n_top, 0, 0))
    seq = lambda a: pl.BlockSpec((1,) + a.shape[1:], lambda b, p_, h_: (b,) + (0,) * (a.ndim - 1))
    ins = (qh, kvc, win_t, vec(kv_s), vec(kv_w), gt)
    out = pl.pallas_call(
        functools.partial(_nsa_sample_attend_body, n_cmp=n_cmp, q_pos=past),
        grid_spec=pltpu.PrefetchScalarGridSpec(
            num_scalar_prefetch=2,
            grid=(n,),
            in_specs=[seq(a) for a in ins] + [blk_spec(k) for k in range(per_seq)],
            out_specs=pl.BlockSpec((1, kvh, Q_ROWS, dh), lambda b, p_, h_: (b, 0, 0, 0))),
        out_shape=jax.ShapeDtypeStruct((n, kvh, Q_ROWS, dh), BF16),
        compiler_params=_params("arbitrary"),
        name="nsa_sample_attend",
    )(page_idx, half, *ins, *([cache_sel_t] * per_seq))
    return out[:, :, :grp].reshape(n, NSA_Q_W), kv_c, kv_s, kv_w


PROMPT_TM = 512
ROUTER_TM = 384


def kernel(x_prompt, x_sample, cache_nsa_cmp, cache_nsa_sel, state_nsa_win, state_ret, state_pool, state_conv, page_table, nsa_w_in, nsa_w_cmp, nsa_b_cmp, nsa_w_o, ret_w_in, ret_gn_g, ret_gn_b, ret_w_o, pool_w, pool_scale, conv_w_in, conv_w, conv_w_out, ln_g, ln_b, moe_w_rg, moe_b_rg, moe_w_re, moe_b_re, moe_w1, moe_w3, moe_w2):
    bp, t, d = x_prompt.shape
    ns = x_sample.shape[0]
    assert x_sample.shape[1] == 1 and (bp * t) % PROMPT_TM == 0 and (bp * t + ns) % ROUTER_TM == 0
    past = page_table.shape[1] * PAGE_SIZE
    xp, xs = x_prompt.reshape(bp * t, d), x_sample.reshape(ns, d)
    kv5 = lambda a, n, rows: a.reshape(n, rows, 2, NSA_KV_HEADS, NSA_HEAD_DIM)
    shift_in = lambda old, new: jnp.concatenate([old[:, 1:], new[:, None]], axis=1)
    cmp_p, sel_p, win_p, ret_p, pool_p, conv_p = [], [], [], [], [], []
    cmp_s, sel_s, win_s, ret_s, pool_s, conv_s = [], [], [], [], [], []
    n_mix = 4
    for i in range(DEPTH):
        kind, j = i % n_mix, i // n_mix
        g0, b0 = ln_g[i, 0], ln_b[i, 0]
        if kind == 0:
            w_in_p, r_cmp, w_o = _nsa_w_in_padded(nsa_w_in[j]), _cmp_weights(nsa_w_cmp[j]), nsa_w_o[j].astype(BF16)
            o, a, b, c = nsa_prompt_mixer(xp.reshape(bp, t, d), w_in_p, r_cmp, nsa_b_cmp[j])
            cmp_p.append(kv5(a, bp, t)); sel_p.append(kv5(b, bp, t)); win_p.append(kv5(c, bp, t)[:, -min(WINDOW, t):])
            xp = matmul_res_ln(o, w_o, xp, g0, b0, PROMPT_TM)
            win = state_nsa_win[j]
            rows_minor = lambda a: jnp.transpose(a, (0, 2, 3, 4, 1))
            o, a, b, c = nsa_sample_mixer(xs, rows_minor(cache_nsa_cmp[j]), rows_minor(cache_nsa_sel[j]), rows_minor(win),
                                          page_table, w_in_p, r_cmp, nsa_b_cmp[j])
            cmp_s.append(kv5(a, ns, 1)); sel_s.append(kv5(b, ns, 1)); win_s.append(shift_in(win, kv5(c, ns, 1)[:, 0]))
            xs = matmul_res_ln(o, w_o, xs, g0, b0, ns)
        elif kind == 1:
            w_in, w_o = ret_w_in[j].astype(BF16), ret_w_o[j].astype(BF16)
            q, k, v, g = matmul_split(xp, w_in, _RET_SPLITS, _RET_DTYPES, tm=256)
            og, s_fin = retention_prompt(q, k, v, g, ret_gn_g[j], ret_gn_b[j], bp, t)
            ret_p.append(s_fin)
            xp = matmul_res_ln(og, w_o, xp, g0, b0, PROMPT_TM)
            q, k, v, g = matmul_split(xs, w_in, _RET_SPLITS, _RET_DTYPES, tm=ns)
            og, s_new = retention_sample(q, k, v, g, ret_gn_g[j], ret_gn_b[j], state_ret[j], past)
            ret_s.append(s_new)
            xs = matmul_res_ln(og, w_o, xs, g0, b0, ns)
        elif kind == 2:
            wp = pool_w[j].astype(BF16)
            pool_p.append(xp.reshape(bp, t, d)[:, -(POOL_MAX - 1):])
            xp = pool_prompt(xp.reshape(bp, t, d), wp, pool_scale[j], g0, b0).reshape(bp * t, d)
            pool_s.append(shift_in(state_pool[j], xs))
            xs = pool_sample(xs, state_pool[j].transpose(1, 0, 2), wp, pool_scale[j], g0, b0, past)
        else:
            w_in, w_out = conv_w_in[j].astype(BF16), conv_w_out[j].astype(BF16)
            bg, cg, h = matmul_split(xp, w_in, _CONV_SPLITS, _CONV_DTYPES, tm=PROMPT_TM)
            a, tail = conv_prompt(bg, cg, h, conv_w[j], bp, t)
            conv_p.append(tail[:, -(CONV_W - 1):])
            xp = matmul_res_ln(a, w_out, xp, g0, b0, PROMPT_TM)
            bg, cg, h = matmul_split(xs, w_in, _CONV_SPLITS, _CONV_DTYPES, tm=ns)
            a, u = conv_sample(bg, cg, h, state_conv[j].transpose(1, 0, 2), conv_w[j])
            conv_s.append(shift_in(state_conv[j], u))
            xs = matmul_res_ln(a, w_out, xs, g0, b0, ns)
        rw, rb = _router_weights(moe_w_rg[i], moe_b_rg[i], moe_w_re[i], moe_b_re[i])
        out, dest = moe_layer(jnp.concatenate([xp, xs], axis=0), rw, rb, moe_w1, moe_w3, moe_w2, i,
                              ln_g[i, 1], ln_b[i, 1], tm=PROMPT_TM, router_tm=ROUTER_TM)
        xp = jnp.take(out, dest[:bp * t], axis=0, mode="clip")
        xs = jnp.take(out, dest[bp * t:], axis=0, mode="clip")
    st = jnp.stack
    return (xp.reshape(bp, t, d), xs.reshape(ns, 1, d), st(cmp_p), st(sel_p), st(win_p), st(ret_p), st(pool_p), st(conv_p),
            st(cmp_s), st(sel_s), st(win_s), st(ret_s), st(pool_s), st(conv_s))
```

```python
import functools

import jax
import jax.numpy as jnp
import numpy as np
from jax import lax
from jax.experimental import pallas as pl
from jax.experimental.pallas import tpu as pltpu

F32, BF16, I32 = jnp.float32, jnp.bfloat16, jnp.int32

D_MODEL = 1024
DEPTH = 4
PAGE_SIZE = 128
NSA_HEADS = 16
NSA_HEAD_DIM = 64
NSA_KV_HEADS = 4
NSA_GROUP = NSA_HEADS // NSA_KV_HEADS
NSA_KV_W = NSA_KV_HEADS * NSA_HEAD_DIM
NSA_Q_W = NSA_HEADS * NSA_HEAD_DIM
CMP_LEN = 32
CMP_STRIDE = 16
SEL_BLOCK = 64
TOP_N = 8
WINDOW = 512
RET_HEADS = 4
RET_DK = D_MODEL // RET_HEADS
RET_DV = 2 * D_MODEL // RET_HEADS
RET_CHUNK = 128
ROPE_BASE = 10000.0
POOL_WINDOWS = (2, 4, 8, 16)
POOL_GROUP = D_MODEL // len(POOL_WINDOWS)
POOL_MAX = max(POOL_WINDOWS)
CONV_W = 3
MOE_GROUPS = 4
MOE_EPG = 8
MOE_EXPERTS = MOE_GROUPS * MOE_EPG
MOE_FF = D_MODEL // 4
ALPHA = (2.0 * DEPTH) ** 0.25
LN_EPS = 1e-5
NEG = -1e30

LANES = 128
Q_TILE = 128
KV_TILE = 256
Q_TILES_PER_CALL = 4
VMEM_LIMIT = 56 * 1024 * 1024

_NT = (((1,), (1,)), ((), ()))
_TN = (((0,), (0,)), ((), ()))


def _params(*sem):
    return pltpu.CompilerParams(dimension_semantics=sem, vmem_limit_bytes=VMEM_LIMIT)


def _mm_body(x_ref, w_ref, *o_refs, splits, chunk):
    x = x_ref[...].astype(BF16)
    col = 0
    for o_ref, width in zip(o_refs, splits):
        for j in range(0, width, chunk):
            c = min(chunk, width - j)
            y = jnp.dot(x, w_ref[:, col + j:col + j + c], preferred_element_type=F32)
            o_ref[:, j:j + c] = y.astype(o_ref.dtype)
        col += width


def matmul_split(x, w, splits, dtypes, tm, chunk=512):
    m, k = x.shape
    n = w.shape[1]
    assert n == sum(splits) and m % tm == 0
    return pl.pallas_call(
        functools.partial(_mm_body, splits=tuple(splits), chunk=chunk),
        grid=(m // tm,),
        in_specs=[pl.BlockSpec((tm, k), lambda i: (i, 0)), pl.BlockSpec((k, n), lambda i: (0, 0))],
        out_specs=[pl.BlockSpec((tm, s), lambda i: (i, 0)) for s in splits],
        out_shape=[jax.ShapeDtypeStruct((m, s), d) for s, d in zip(splits, dtypes)],
        compiler_params=_params("arbitrary"),
        name="matmul_split",
    )(x, w)


def _layer_norm(v, g, b):
    mu = jnp.mean(v, axis=-1, keepdims=True)
    c = v - mu
    var = jnp.mean(c * c, axis=-1, keepdims=True)
    return c * lax.rsqrt(var + LN_EPS) * g + b


def _mm_res_ln_body(a_ref, w_ref, x_ref, g_ref, b_ref, o_ref):
    y = jnp.dot(a_ref[...].astype(BF16), w_ref[...], preferred_element_type=F32)
    o_ref[...] = _layer_norm(ALPHA * x_ref[...] + y, g_ref[...], b_ref[...])


def matmul_res_ln(a, w, x, g, b, tm):
    m, k = a.shape
    d = x.shape[1]
    return pl.pallas_call(
        _mm_res_ln_body,
        grid=(m // tm,),
        in_specs=[pl.BlockSpec((tm, k), lambda i: (i, 0)), pl.BlockSpec((k, d), lambda i: (0, 0)),
                  pl.BlockSpec((tm, d), lambda i: (i, 0)), pl.BlockSpec((1, d), lambda i: (0, 0)),
                  pl.BlockSpec((1, d), lambda i: (0, 0))],
        out_specs=pl.BlockSpec((tm, d), lambda i: (i, 0)),
        out_shape=jax.ShapeDtypeStruct((m, d), F32),
        compiler_params=_params("arbitrary"),
        name="matmul_res_ln",
    )(a, w, x, g.reshape(1, d), b.reshape(1, d))


def _cmp_weights(w_cmp):
    dh = NSA_HEAD_DIM
    w6 = w_cmp.reshape(2, 2, CMP_STRIDE // 2, 2, dh, dh)
    eye = jnp.eye(2, dtype=w_cmp.dtype)
    r = jnp.einsum("chpsde,xy->cpsxdhye", w6, eye)
    return r.reshape(2, CMP_STRIDE // 2, 4 * dh, 4 * dh).astype(BF16)


def _cmp_body(*refs):
    x_refs, (r_ref, b_ref, o_ref) = refs[:-3], refs[-3:]
    row_w = 2 * NSA_KV_W

    def cols(c0):
        return jnp.concatenate([x_ref[0, :, c0:c0 + LANES] for x_ref in x_refs], axis=0)

    for c in range(2):
        for hp in range(2):
            acc = None
            for sp in range(CMP_STRIDE // 2):
                c0 = (2 * sp) * row_w + c * NSA_KV_W + hp * LANES
                lhs = jnp.concatenate([cols(c0), cols(c0 + row_w)], axis=1).astype(BF16)
                part = jnp.dot(lhs, r_ref[c, sp], preferred_element_type=F32)
                acc = part if acc is None else acc + part
            lo, hi = acc[:, :LANES], acc[:, LANES:]
            nxt = pltpu.roll(hi, hi.shape[0] - 1, 0)
            o0 = c * NSA_KV_W + hp * LANES
            o_ref[0, :, o0:o0 + LANES] = lo + nxt + b_ref[:, o0:o0 + LANES]


def nsa_compress(rows, r_w, b_cmp):
    bsz, t, w = rows.shape
    n_ch = t // CMP_STRIDE
    x = rows.reshape(bsz, n_ch, CMP_STRIDE * w)
    bias = jnp.broadcast_to(b_cmp[:, None, :], (2, NSA_KV_HEADS, NSA_HEAD_DIM)).reshape(1, w)
    return pl.pallas_call(
        _cmp_body,
        grid=(bsz,),
        in_specs=[pl.BlockSpec((1, n_ch, CMP_STRIDE * w), lambda b: (b, 0, 0)),
                  pl.BlockSpec(r_w.shape, lambda b: (0, 0, 0, 0)),
                  pl.BlockSpec((1, w), lambda b: (0, 0))],
        out_specs=pl.BlockSpec((1, n_ch, w), lambda b: (b, 0, 0)),
        out_shape=jax.ShapeDtypeStruct((bsz, n_ch, w), F32),
        compiler_params=_params("arbitrary"),
        name="nsa_compress",
    )(x, r_w, bias)


def _cmp_paged_body(pt_ref, *refs):
    page_refs, (r_ref, b_ref, o_ref, rows_scr) = refs[:-4], refs[-4:]
    page = page_refs[0].shape[-1]
    for pi, p_ref in enumerate(page_refs):
        for c in range(2):
            for hp in range(2):
                pair = p_ref[0, c, 2 * hp:2 * hp + 2].reshape(2 * NSA_HEAD_DIM, page)
                rows_scr[c * 2 + hp, pi * page:(pi + 1) * page, :] = pair.T
    n_ch = o_ref.shape[1]
    for c in range(2):
        for hp in range(2):
            acc = None
            for sp in range(CMP_STRIDE // 2):
                pos = [rows_scr[c * 2 + hp, pl.ds(2 * sp + sl, n_ch, stride=CMP_STRIDE), :] for sl in range(2)]
                part = jnp.dot(jnp.concatenate(pos, axis=1).astype(BF16), r_ref[c, sp], preferred_element_type=F32)
                acc = part if acc is None else acc + part
            lo, hi = acc[:, :LANES], acc[:, LANES:]
            nxt = pltpu.roll(hi, hi.shape[0] - 1, 0)
            o0 = c * NSA_KV_W + hp * LANES
            o_ref[0, :, o0:o0 + LANES] = lo + nxt + b_ref[:, o0:o0 + LANES]


def nsa_compress_paged(cache_t, page_table, r_w, b_cmp):
    n_phys, _, kvh, dh, page = cache_t.shape
    n, n_pages = page_table.shape
    w = 2 * kvh * dh
    n_ch = n_pages * page // CMP_STRIDE
    bias = jnp.broadcast_to(b_cmp[:, None, :], (2, kvh, dh)).reshape(1, w)
    page_spec = lambda k: pl.BlockSpec((1, 2, kvh, dh, page), lambda b, pt: (pt[b * n_pages + k], 0, 0, 0, 0))
    return pl.pallas_call(
        _cmp_paged_body,
        grid_spec=pltpu.PrefetchScalarGridSpec(
            num_scalar_prefetch=1,
            grid=(n,),
            in_specs=[page_spec(k) for k in range(n_pages)]
            + [pl.BlockSpec(r_w.shape, lambda b, pt: (0, 0, 0, 0)), pl.BlockSpec((1, w), lambda b, pt: (0, 0))],
            out_specs=pl.BlockSpec((1, n_ch, w), lambda b, pt: (b, 0, 0)),
            scratch_shapes=[pltpu.VMEM((4, n_pages * page, LANES), F32)]),
        out_shape=jax.ShapeDtypeStruct((n, n_ch, w), F32),
        compiler_params=_params("arbitrary"),
        name="nsa_compress_paged",
    )(page_table.reshape(-1), *([cache_t] * n_pages), r_w, bias)


def _softmax_cols(s, ok):
    s = jnp.where(ok, s, NEG)
    m = jnp.max(s, axis=0, keepdims=True)
    p = jnp.where(ok, jnp.exp(s - m), 0.0)
    l = jnp.sum(p, axis=0, keepdims=True)
    return p / jnp.where(l > 0.0, l, 1.0)


def _nsa_prompt_body(q_ref, kc_ref, vct_ref, ks_ref, vst_ref, kw_ref, vwt_ref, g_ref, o_ref, imp_ref, bias_ref, s_ref,
                     *, n_cmp, n_sel, i0, n_sel_tiles):
    i = i0 + pl.program_id(2)
    t0 = i * Q_TILE
    gq = NSA_GROUP * Q_TILE
    tq = t0 + lax.broadcasted_iota(I32, (1, Q_TILE), 1)
    tq4 = t0 + (lax.broadcasted_iota(I32, (1, gq), 1) & (Q_TILE - 1))
    q = q_ref[0, 0, 0]

    n_rows = kc_ref.shape[2]
    s = lax.dot_general(kc_ref[0, 0], q, _NT, preferred_element_type=F32)
    n_idx = lax.broadcasted_iota(I32, (n_rows, 1), 0)
    ok_c = (n_idx * CMP_STRIDE + (CMP_LEN - 1) <= tq4) & (n_idx < n_cmp)
    p_c = _softmax_cols(s, ok_c)
    o_c = jnp.dot(vct_ref[0, 0], p_c.astype(BF16), preferred_element_type=F32)

    imp = p_c[:, 0:Q_TILE]
    for g in range(1, NSA_GROUP):
        imp = imp + p_c[:, g * Q_TILE:(g + 1) * Q_TILE]
    imp_ref[...] = imp
    r_sel = SEL_BLOCK // CMP_STRIDE
    n_blk = n_rows // r_sel
    blk_imp = imp_ref[pl.ds(0, n_blk, stride=r_sel), :]
    for r in range(1, r_sel):
        blk_imp = blk_imp + imp_ref[pl.ds(r, n_blk, stride=r_sel), :]
    j_idx = lax.broadcasted_iota(I32, (n_blk, 1), 0)
    score = jnp.where(j_idx == tq // SEL_BLOCK, NSA_GROUP + 2.0,
                      jnp.where(j_idx == 0, NSA_GROUP + 1.0,
                                jnp.where(j_idx * SEL_BLOCK <= tq, blk_imp, -1.0)))
    score = jnp.where(j_idx < n_sel, score, -3.0)
    sel = jnp.zeros((n_blk, Q_TILE), F32)
    for _ in range(min(TOP_N, n_sel)):
        mx = jnp.max(score, axis=0, keepdims=True)
        first = jnp.min(jnp.where(score == mx, j_idx, n_blk), axis=0, keepdims=True)
        pick = j_idx == first
        sel = jnp.where(pick & (mx >= 0.0), 1.0, sel)
        score = jnp.where(pick, -2.0, score)
    blk_per_tile = KV_TILE // SEL_BLOCK
    for j in range(min(n_sel, n_sel_tiles * blk_per_tile)):
        bias_ref[j * SEL_BLOCK:(j + 1) * SEL_BLOCK, :] = jnp.broadcast_to(
            jnp.where(sel[j:j + 1, :] > 0.0, 0.0, NEG), (SEL_BLOCK, Q_TILE))

    def attend(k_ref, vt_ref, tiles, bias_fn):
        m = jnp.full((1, gq), NEG, F32)
        for n, kt in enumerate(tiles):
            k0 = kt * KV_TILE if isinstance(kt, int) else pl.multiple_of(kt * KV_TILE, KV_TILE)
            s = lax.dot_general(k_ref[0, 0, pl.ds(k0, KV_TILE), :], q, _NT, preferred_element_type=F32)
            kpos = k0 + lax.broadcasted_iota(I32, (KV_TILE, 1), 0)
            s = s + jnp.concatenate([bias_fn(kt, k0, kpos)] * NSA_GROUP, axis=1)
            s_ref[n * KV_TILE:(n + 1) * KV_TILE, :] = s
            m = jnp.maximum(m, jnp.max(s, axis=0, keepdims=True))
        acc = jnp.zeros((VT_ROWS, gq), F32)
        for n, kt in enumerate(tiles):
            p = jnp.exp(s_ref[n * KV_TILE:(n + 1) * KV_TILE, :] - m).astype(BF16)
            acc = acc + jnp.dot(vt_ref[0, 0, kt], p, preferred_element_type=F32)
        return acc[0:NSA_HEAD_DIM] / acc[NSA_HEAD_DIM:NSA_HEAD_DIM + 1]

    first_diag = (i0 * Q_TILE) // KV_TILE

    def sel_bias(kt, k0, kpos):
        bias = bias_ref[pl.ds(k0, KV_TILE), :]
        return bias if kt < first_diag else jnp.where(kpos <= tq, bias, NEG)

    def win_bias(kt, k0, kpos):
        dist = tq - kpos
        return jnp.where((dist >= 0) & (dist <= WINDOW), 0.0, NEG)

    o_s = attend(ks_ref, vst_ref, list(range(n_sel_tiles)), sel_bias)
    last = (t0 + Q_TILE - 1) // KV_TILE
    n_win_tiles = WINDOW // KV_TILE + 1
    if (i0 * Q_TILE + Q_TILE - 1) // KV_TILE < n_win_tiles - 1:
        win_tiles = list(range(n_sel_tiles))
    else:
        win_tiles = [last - (n_win_tiles - 1) + r for r in range(n_win_tiles)]
    o_w = attend(kw_ref, vwt_ref, win_tiles, win_bias)

    gate = 1.0 / (1.0 + jnp.exp(-g_ref[0, 0]))
    for g in range(NSA_GROUP):
        sl = slice(g * Q_TILE, (g + 1) * Q_TILE)
        o = gate[g, 0:1, :] * o_c[:, sl] + gate[g, 1:2, :] * o_s[:, sl] + gate[g, 2:3, :] * o_w[:, sl]
        o_ref[0, 0, g] = o.astype(o_ref.dtype)


VT_ROWS = NSA_HEAD_DIM + 16


def _head_major(kv):
    bsz, t, _ = kv.shape
    kv = kv.astype(BF16).reshape(bsz, t, 2, NSA_KV_HEADS, NSA_HEAD_DIM)
    k = kv[:, :, 0].transpose(0, 2, 1, 3)
    vt = kv[:, :, 1].reshape(bsz, t // KV_TILE, KV_TILE, NSA_KV_HEADS, NSA_HEAD_DIM).transpose(0, 3, 1, 4, 2)
    lead = vt.shape[:3]
    ones = jnp.ones(lead + (1, KV_TILE), BF16)
    zeros = jnp.zeros(lead + (VT_ROWS - NSA_HEAD_DIM - 1, KV_TILE), BF16)
    return k, jnp.concatenate([vt, ones, zeros], axis=3)


def nsa_prompt_attention(q, kvc, kv_s, kv_w, gates, n_cmp):
    bsz, t, _ = q.shape
    nq = t // Q_TILE
    n_sel = -(-t // SEL_BLOCK)
    n_rows = kvc.shape[1]
    dh, kvh, grp = NSA_HEAD_DIM, NSA_KV_HEADS, NSA_GROUP
    qh = q.reshape(bsz, nq, Q_TILE, kvh, grp, dh).transpose(0, 3, 1, 4, 2, 5).reshape(bsz, kvh, nq, grp * Q_TILE, dh)
    kc5 = kvc.astype(BF16).reshape(bsz, n_rows, 2, kvh, dh)
    kc = kc5[:, :, 0].transpose(0, 2, 1, 3)
    vct = kc5[:, :, 1].transpose(0, 2, 3, 1)
    ks, vst = _head_major(kv_s)
    kw, vwt = _head_major(kv_w)
    gt = gates[:, :, :NSA_HEADS * 3].reshape(bsz, t, kvh, grp, 3).transpose(0, 2, 3, 4, 1)
    nkt = t // KV_TILE
    assert WINDOW % KV_TILE == 0 and KV_TILE % Q_TILE == 0 and nq % Q_TILES_PER_CALL == 0
    bh = lambda b, h, i: (b, h, 0, 0)
    outs = []
    for i0 in range(0, nq, Q_TILES_PER_CALL):
        n_sel_tiles = ((i0 + Q_TILES_PER_CALL) * Q_TILE - 1) // KV_TILE + 1
        outs.append(pl.pallas_call(
            functools.partial(_nsa_prompt_body, n_cmp=n_cmp, n_sel=n_sel, i0=i0, n_sel_tiles=n_sel_tiles),
            grid=(bsz, kvh, Q_TILES_PER_CALL),
            in_specs=[pl.BlockSpec((1, 1, 1, grp * Q_TILE, dh), lambda b, h, i, i0=i0: (b, h, i0 + i, 0, 0)),
                      pl.BlockSpec((1, 1, n_rows, dh), bh),
                      pl.BlockSpec((1, 1, dh, n_rows), bh),
                      pl.BlockSpec((1, 1, t, dh), bh),
                      pl.BlockSpec((1, 1, nkt, VT_ROWS, KV_TILE), lambda b, h, i: (b, h, 0, 0, 0)),
                      pl.BlockSpec((1, 1, t, dh), bh),
                      pl.BlockSpec((1, 1, nkt, VT_ROWS, KV_TILE), lambda b, h, i: (b, h, 0, 0, 0)),
                      pl.BlockSpec((1, 1, grp, 3, Q_TILE), lambda b, h, i, i0=i0: (b, h, 0, 0, i0 + i))],
            out_specs=pl.BlockSpec((1, 1, grp, dh, Q_TILE), lambda b, h, i: (b, h, 0, 0, i)),
            out_shape=jax.ShapeDtypeStruct((bsz, kvh, grp, dh, Q_TILES_PER_CALL * Q_TILE), BF16),
            scratch_shapes=[pltpu.VMEM((n_rows, Q_TILE), F32), pltpu.VMEM((n_sel * SEL_BLOCK, Q_TILE), F32),
                            pltpu.VMEM((max(n_sel_tiles, WINDOW // KV_TILE + 1) * KV_TILE, grp * Q_TILE), F32)],
            compiler_params=_params("arbitrary", "arbitrary", "arbitrary"),
            name=f"nsa_prompt_attention_q{i0}",
        )(qh, kc, vct, ks, vst, kw, vwt, gt))
    out = jnp.concatenate(outs, axis=-1)
    return out.transpose(0, 4, 1, 2, 3).reshape(bsz, t, kvh * grp * dh)


def _nsa_w_in_padded(w_in):
    wq = w_in[:, :NSA_Q_W] * NSA_HEAD_DIM ** -0.5
    pad = jnp.zeros((w_in.shape[0], LANES - 3 * NSA_HEADS), w_in.dtype)
    return jnp.concatenate([wq, w_in[:, NSA_Q_W:], pad], axis=1).astype(BF16)


_NSA_SPLITS = (NSA_Q_W, 2 * NSA_KV_W, 2 * NSA_KV_W, 2 * NSA_KV_W, LANES)
_NSA_DTYPES = (BF16, F32, F32, F32, F32)


def nsa_prompt_mixer(x, w_in_p, r_cmp, b_cmp):
    bsz, t, d = x.shape
    q, kv_c, kv_s, kv_w, gates = matmul_split(x.reshape(bsz * t, d), w_in_p, _NSA_SPLITS, _NSA_DTYPES, tm=min(512, bsz * t))
    r3 = lambda a: a.reshape(bsz, t, a.shape[-1])
    kv_c, kv_s, kv_w = r3(kv_c), r3(kv_s), r3(kv_w)
    n_cmp = (t - CMP_LEN) // CMP_STRIDE + 1
    kvc = nsa_compress(kv_c, r_cmp, b_cmp)
    o = nsa_prompt_attention(r3(q), kvc, kv_s, kv_w, r3(gates), n_cmp)
    return o.reshape(bsz * t, NSA_Q_W), kv_c, kv_s, kv_w


def _ret_log_decay():
    return jnp.log(1.0 - 2.0 ** (-5.0 - jnp.arange(RET_HEADS, dtype=F32)))


def _rope_tables(pos):
    half = RET_DK // 2
    inv = ROPE_BASE ** (-jnp.linspace(0.0, 1.0, half, dtype=F32))
    ang = pos.astype(F32)[:, None] * inv[None, :]
    return jnp.cos(ang), jnp.sin(ang)


def _rope(x, cos, sin):
    half = RET_DK // 2
    x1, x2 = x[:, :half], x[:, half:]
    return jnp.concatenate([x1 * cos - x2 * sin, x2 * cos + x1 * sin], axis=1)


def _group_norm_gate(o, gate, g, b):
    mu = jnp.mean(o, axis=-1, keepdims=True)
    c = o - mu
    var = jnp.mean(c * c, axis=-1, keepdims=True)
    on = c * lax.rsqrt(var + LN_EPS) * g + b
    return gate * (1.0 / (1.0 + jnp.exp(-gate))) * on


def _ret_prompt_body(q_ref, k_ref, v_ref, gt_ref, cos_ref, sin_ref, dm_ref, di_ref, dr_ref, dc_ref, gg_ref, gb_ref,
                     o_ref, s_ref, s_scr):
    c = pl.program_id(1)

    @pl.when(c == 0)
    def _():
        s_scr[...] = jnp.zeros_like(s_scr)

    cos, sin = cos_ref[...], sin_ref[...]
    for h in range(RET_HEADS):
        qk = slice(h * RET_DK, (h + 1) * RET_DK)
        vv = slice(h * RET_DV, (h + 1) * RET_DV)
        q = _rope(q_ref[:, qk], cos, sin)
        k = _rope(k_ref[:, qk], cos, sin) * RET_DK ** -0.5
        qb, v = q.astype(BF16), v_ref[:, vv]
        inner = lax.dot_general(qb, k.astype(BF16), _NT, preferred_element_type=F32) * dm_ref[h]
        o = jnp.dot(inner.astype(BF16), v, preferred_element_type=F32)
        s_old = s_scr[h]
        o = o + jnp.dot(qb, s_old.astype(BF16), preferred_element_type=F32) * di_ref[h]
        kd = (k * dr_ref[h]).astype(BF16)
        s_new = dc_ref[h] * s_old + lax.dot_general(kd, v, _TN, preferred_element_type=F32)
        s_scr[h] = s_new
        s_ref[0, h] = s_new
        o_ref[:, vv] = _group_norm_gate(o, gt_ref[:, vv], gg_ref[:, vv], gb_ref[:, vv]).astype(o_ref.dtype)


def retention_prompt(q, k, v, gate, gn_g, gn_b, bsz, t):
    ch = RET_CHUNK
    n_ch = t // ch
    lg = _ret_log_decay()
    i = jnp.arange(ch, dtype=F32)
    diff = i[:, None] - i[None, :]
    dmask = jnp.where(diff >= 0, jnp.exp(lg[:, None, None] * jnp.maximum(diff, 0.0)), 0.0)
    d_in = jnp.exp((i[None, :] + 1.0) * lg[:, None])[:, :, None]
    d_rev = jnp.exp((ch - 1.0 - i)[None, :] * lg[:, None])[:, :, None]
    d_c = jnp.exp(ch * lg)[:, None, None]
    cos, sin = _rope_tables(jnp.arange(t))
    half = RET_DK // 2
    nh = RET_HEADS
    rows = lambda w: pl.BlockSpec((ch, w), lambda b, c: (b * n_ch + c, 0))
    whole = lambda a: pl.BlockSpec(a.shape, lambda b, c: (0,) * a.ndim)
    consts = (dmask, d_in, d_rev, d_c, gn_g.reshape(1, -1), gn_b.reshape(1, -1))
    return pl.pallas_call(
        _ret_prompt_body,
        grid=(bsz, n_ch),
        in_specs=[rows(nh * RET_DK), rows(nh * RET_DK), rows(nh * RET_DV), rows(nh * RET_DV),
                  pl.BlockSpec((ch, half), lambda b, c: (c, 0)), pl.BlockSpec((ch, half), lambda b, c: (c, 0))]
        + [whole(a) for a in consts],
        out_specs=[rows(nh * RET_DV), pl.BlockSpec((1, nh, RET_DK, RET_DV), lambda b, c: (b, 0, 0, 0))],
        out_shape=[jax.ShapeDtypeStruct((bsz * t, nh * RET_DV), BF16),
                   jax.ShapeDtypeStruct((bsz, nh, RET_DK, RET_DV), F32)],
        scratch_shapes=[pltpu.VMEM((nh, RET_DK, RET_DV), F32)],
        compiler_params=_params("arbitrary", "arbitrary"),
        name="retention_prompt",
    )(q, k, v, gate, cos, sin, *consts)


_RET_SPLITS = (RET_HEADS * RET_DK, RET_HEADS * RET_DK, RET_HEADS * RET_DV, RET_HEADS * RET_DV)
_RET_DTYPES = (F32, F32, BF16, F32)


def _ret_sample_body(q_ref, k_ref, v_ref, gt_ref, cos_ref, sin_ref, dec_ref, gg_ref, gb_ref, s_in_ref,
                     o_ref, s_out_ref):
    cos, sin = cos_ref[...], sin_ref[...]
    row0 = lax.broadcasted_iota(I32, (8, 1), 0) == 0
    for h in range(RET_HEADS):
        qh = _rope(q_ref[0, :, h * RET_DK:(h + 1) * RET_DK], cos, sin)
        kh = _rope(k_ref[0, :, h * RET_DK:(h + 1) * RET_DK], cos, sin) * RET_DK ** -0.5
        vh = v_ref[0, :, h * RET_DV:(h + 1) * RET_DV]
        qb, kb = qh.astype(BF16), kh.astype(BF16)
        dec = dec_ref[h]
        s_old = s_in_ref[0, h]
        q8 = jnp.broadcast_to(qb, (8, RET_DK))
        cross = jnp.dot(q8, s_old.astype(BF16), preferred_element_type=F32)[0:1] * dec
        inner = jnp.sum(qb.astype(F32) * kb.astype(F32), axis=1, keepdims=True)
        o = inner.astype(BF16).astype(F32) * vh.astype(F32) + cross
        k8 = jnp.where(row0, jnp.broadcast_to(kb.astype(F32), (8, RET_DK)), 0.0).astype(BF16)
        v8 = jnp.broadcast_to(vh, (8, RET_DV))
        s_out_ref[0, h] = dec * s_old + lax.dot_general(k8, v8, _TN, preferred_element_type=F32)
        sl = slice(h * RET_DV, (h + 1) * RET_DV)
        o_ref[0, :, sl] = _group_norm_gate(o, gt_ref[0, :, sl], gg_ref[:, sl], gb_ref[:, sl]).astype(o_ref.dtype)


def retention_sample(q, k, v, gate, gn_g, gn_b, state, pos0):
    n = q.shape[0]
    cos, sin = _rope_tables(jnp.full((1,), pos0))
    dec = jnp.exp(_ret_log_decay())
    r3 = lambda a: a.reshape(n, 1, a.shape[-1])
    w_qk, w_v = RET_HEADS * RET_DK, RET_HEADS * RET_DV
    half = RET_DK // 2
    vec = lambda w: pl.BlockSpec((1, 1, w), lambda b: (b, 0, 0))
    og, s_new = pl.pallas_call(
        _ret_sample_body,
        grid=(n,),
        in_specs=[vec(w_qk), vec(w_qk), vec(w_v), vec(w_v),
                  pl.BlockSpec((1, half), lambda b: (0, 0)), pl.BlockSpec((1, half), lambda b: (0, 0)),
                  pl.BlockSpec(memory_space=pltpu.SMEM),
                  pl.BlockSpec((1, w_v), lambda b: (0, 0)), pl.BlockSpec((1, w_v), lambda b: (0, 0)),
                  pl.BlockSpec((1, RET_HEADS, RET_DK, RET_DV), lambda b: (b, 0, 0, 0))],
        out_specs=[vec(w_v), pl.BlockSpec((1, RET_HEADS, RET_DK, RET_DV), lambda b: (b, 0, 0, 0))],
        out_shape=[jax.ShapeDtypeStruct((n, 1, w_v), BF16), jax.ShapeDtypeStruct(state.shape, F32)],
        compiler_params=_params("arbitrary"),
        name="retention_sample",
    )(r3(q), r3(k), r3(v), r3(gate), cos, sin, dec, gn_g.reshape(1, -1), gn_b.reshape(1, -1), state)
    return og.reshape(n, w_v), s_new


POOL_HALO = 16


def _pool_mix(win_sum_fn, x, pos, w_ref, sc_ref, g_ref, b_ref):
    ys = []
    for gi, w in enumerate(POOL_WINDOWS):
        sl = slice(gi * POOL_GROUP, (gi + 1) * POOL_GROUP)
        pooled = win_sum_fn(gi, w) / jnp.minimum(float(w), pos + 1.0) - x[:, sl]
        ys.append(jnp.dot(pooled.astype(BF16), w_ref[gi], preferred_element_type=F32))
    y = jnp.concatenate(ys, axis=1) * sc_ref[...]
    return _layer_norm(ALPHA * x + y, g_ref[...], b_ref[...])


def _pool_prompt_body(x_ref, w_ref, sc_ref, g_ref, b_ref, o_ref, xh_scr, *, tt):
    j = pl.program_id(1)

    @pl.when(j == 0)
    def _():
        xh_scr[0:POOL_HALO, :] = jnp.zeros((POOL_HALO, D_MODEL), F32)

    x = x_ref[0]
    xh_scr[POOL_HALO:, :] = x
    pos = (j * tt + lax.broadcasted_iota(I32, (tt, 1), 0)).astype(F32)

    def win_sum(gi, w):
        c0 = gi * POOL_GROUP
        acc = x[:, c0:c0 + POOL_GROUP]
        for u in range(1, w):
            acc = acc + xh_scr[POOL_HALO - u:POOL_HALO - u + tt, c0:c0 + POOL_GROUP]
        return acc

    o_ref[0] = _pool_mix(win_sum, x, pos, w_ref, sc_ref, g_ref, b_ref)
    xh_scr[0:POOL_HALO, :] = x[tt - POOL_HALO:, :]


def pool_prompt(x, w_pool, scale, g, b, tt=256):
    bsz, t, d = x.shape
    tt = min(tt, t)
    row = lambda b_, j: (0, 0)
    return pl.pallas_call(
        functools.partial(_pool_prompt_body, tt=tt),
        grid=(bsz, t // tt),
        in_specs=[pl.BlockSpec((1, tt, d), lambda b_, j: (b_, j, 0)),
                  pl.BlockSpec(w_pool.shape, lambda b_, j: (0, 0, 0)),
                  pl.BlockSpec((1, d), row), pl.BlockSpec((1, d), row), pl.BlockSpec((1, d), row)],
        out_specs=pl.BlockSpec((1, tt, d), lambda b_, j: (b_, j, 0)),
        out_shape=jax.ShapeDtypeStruct(x.shape, F32),
        scratch_shapes=[pltpu.VMEM((POOL_HALO + tt, d), F32)],
        compiler_params=_params("arbitrary", "arbitrary"),
        name="pool_prompt",
    )(x, w_pool, scale.reshape(1, d), g.reshape(1, d), b.reshape(1, d))


def _pool_sample_body(x_ref, st_ref, w_ref, sc_ref, g_ref, b_ref, o_ref, *, pos0):
    x = x_ref[...]
    n_hist = st_ref.shape[0]

    def win_sum(gi, w):
        c0 = gi * POOL_GROUP
        acc = x[:, c0:c0 + POOL_GROUP]
        for u in range(1, w):
            acc = acc + st_ref[n_hist - u, :, c0:c0 + POOL_GROUP]
        return acc

    pos = jnp.full((x.shape[0], 1), float(pos0), F32)
    o_ref[...] = _pool_mix(win_sum, x, pos, w_ref, sc_ref, g_ref, b_ref)


def pool_sample(x, hist, w_pool, scale, g, b, pos0):
    n, d = x.shape
    whole = lambda a: pl.BlockSpec(a.shape, lambda i: (0,) * a.ndim)
    args = (x, hist, w_pool, scale.reshape(1, d), g.reshape(1, d), b.reshape(1, d))
    return pl.pallas_call(
        functools.partial(_pool_sample_body, pos0=pos0),
        grid=(1,),
        in_specs=[whole(a) for a in args],
        out_specs=pl.BlockSpec((n, d), lambda i: (0, 0)),
        out_shape=jax.ShapeDtypeStruct((n, d), F32),
        compiler_params=_params("arbitrary"),
        name="pool_sample",
    )(*args)


CONV_HALO = 8


def _conv_prompt_body(bg_ref, cg_ref, h_ref, wc_ref, a_ref, tail_ref, uh_scr, *, tt):
    j = pl.program_id(1)

    @pl.when(j == 0)
    def _():
        uh_scr[0:CONV_HALO, :] = jnp.zeros((CONV_HALO, D_MODEL), F32)

    u = cg_ref[...] * h_ref[...]
    uh_scr[CONV_HALO:, :] = u
    conv = wc_ref[CONV_W - 1:CONV_W, :] * u
    for jj in range(CONV_W - 1):
        back = CONV_W - 1 - jj
        conv = conv + wc_ref[jj:jj + 1, :] * uh_scr[CONV_HALO - back:CONV_HALO - back + tt, :]
    a_ref[...] = (bg_ref[...] * conv).astype(a_ref.dtype)
    uh_scr[0:CONV_HALO, :] = u[tt - CONV_HALO:, :]
    tail_ref[0] = u[tt - CONV_HALO:, :]


def conv_prompt(bg, cg, h, w_conv, bsz, t, tt=256):
    d = bg.shape[1]
    tt = min(tt, t)
    nt = t // tt
    blk = pl.BlockSpec((tt, d), lambda b_, j: (b_ * nt + j, 0))
    return pl.pallas_call(
        functools.partial(_conv_prompt_body, tt=tt),
        grid=(bsz, nt),
        in_specs=[blk, blk, blk, pl.BlockSpec((CONV_W, d), lambda b_, j: (0, 0))],
        out_specs=[blk, pl.BlockSpec((1, CONV_HALO, d), lambda b_, j: (b_, 0, 0))],
        out_shape=[jax.ShapeDtypeStruct((bsz * t, d), BF16), jax.ShapeDtypeStruct((bsz, CONV_HALO, d), F32)],
        scratch_shapes=[pltpu.VMEM((CONV_HALO + tt, d), F32)],
        compiler_params=_params("arbitrary", "arbitrary"),
        name="conv_prompt",
    )(bg, cg, h, w_conv)


def _conv_sample_body(bg_ref, cg_ref, h_ref, prev_ref, wc_ref, a_ref, u_ref):
    u = cg_ref[...] * h_ref[...]
    conv = wc_ref[CONV_W - 1:CONV_W, :] * u
    for jj in range(CONV_W - 1):
        conv = conv + wc_ref[jj:jj + 1, :] * prev_ref[jj]
    a_ref[...] = (bg_ref[...] * conv).astype(a_ref.dtype)
    u_ref[...] = u


def conv_sample(bg, cg, h, prev, w_conv):
    n, d = bg.shape
    whole = lambda a: pl.BlockSpec(a.shape, lambda i: (0,) * a.ndim)
    args = (bg, cg, h, prev, w_conv)
    return pl.pallas_call(
        _conv_sample_body,
        grid=(1,),
        in_specs=[whole(a) for a in args],
        out_specs=[pl.BlockSpec((n, d), lambda i: (0, 0)), pl.BlockSpec((n, d), lambda i: (0, 0))],
        out_shape=[jax.ShapeDtypeStruct((n, d), BF16), jax.ShapeDtypeStruct((n, d), F32)],
        compiler_params=_params("arbitrary"),
        name="conv_sample",
    )(*args)


_CONV_SPLITS = (D_MODEL, D_MODEL, D_MODEL)
_CONV_DTYPES = (F32, F32, F32)


ROUTER_LANES = LANES
EXPERT_LANE0 = MOE_GROUPS


COMB_LANE0 = 8


def _router_body(x_ref, w_ref, b_ref, o_ref, cnt_ref, run_scr):
    i = pl.program_id(0)

    @pl.when(i == 0)
    def _():
        run_scr[...] = jnp.zeros_like(run_scr)

    x = x_ref[...]
    x_hi = x.astype(BF16)
    x_lo = (x - x_hi.astype(F32)).astype(BF16)
    logits = (jnp.dot(x_hi, w_ref[0], preferred_element_type=F32) + jnp.dot(x_lo, w_ref[0], preferred_element_type=F32)
              + jnp.dot(x_hi, w_ref[1], preferred_element_type=F32) + b_ref[...])
    tm = logits.shape[0]
    lane = lax.broadcasted_iota(I32, logits.shape, 1)
    big = ROUTER_LANES

    def top1(mask):
        v = jnp.max(jnp.where(mask, logits, -jnp.inf), axis=1, keepdims=True)
        idx = jnp.min(jnp.where(mask & (logits == v), lane, big), axis=1, keepdims=True)
        return v, idx

    is_g = lane < MOE_GROUPS
    vg, gsel = top1(is_g)
    pg_sel = 1.0 / jnp.sum(jnp.where(is_g, jnp.exp(logits - vg), 0.0), axis=1, keepdims=True)
    e0 = EXPERT_LANE0 + gsel * MOE_EPG
    is_e = (lane >= e0) & (lane < e0 + MOE_EPG)
    v1, i1 = top1(is_e)
    v2, i2 = top1(is_e & (lane != i1))
    r = jnp.exp(v2 - v1)
    pe1 = pg_sel / (1.0 + r)
    pe2 = pg_sel * r / (1.0 + r)
    local = lane - COMB_LANE0
    comb = jnp.where(local == i1 - e0, pe1, 0.0) + jnp.where(local == i2 - e0, pe2, 0.0)

    onehot = jnp.where(lane == gsel, 1.0, 0.0)
    rows = lax.broadcasted_iota(I32, (tm, tm), 0)
    cols = lax.broadcasted_iota(I32, (tm, tm), 1)
    before = jnp.where(cols < rows, 1.0, 0.0).astype(BF16)
    prefix = jnp.dot(before, onehot.astype(BF16), preferred_element_type=F32)
    run = run_scr[0:1, :]
    rank = jnp.sum(jnp.where(lane == gsel, prefix + run, 0.0), axis=1, keepdims=True)
    run = run + jnp.sum(onehot, axis=0, keepdims=True)
    run_scr[0:1, :] = run
    cnt_ref[...] = jnp.broadcast_to(run, cnt_ref.shape)
    o_ref[...] = jnp.where(lane == 0, gsel.astype(F32), jnp.where(lane == 1, rank, comb))


def moe_router(x, w_r, b_r, tm):
    m, d = x.shape
    return pl.pallas_call(
        _router_body,
        grid=(m // tm,),
        in_specs=[pl.BlockSpec((tm, d), lambda i: (i, 0)), pl.BlockSpec((2, d, ROUTER_LANES), lambda i: (0, 0, 0)),
                  pl.BlockSpec((1, ROUTER_LANES), lambda i: (0, 0))],
        out_specs=[pl.BlockSpec((tm, ROUTER_LANES), lambda i: (i, 0)),
                   pl.BlockSpec((8, ROUTER_LANES), lambda i: (0, 0))],
        out_shape=[jax.ShapeDtypeStruct((m, ROUTER_LANES), F32), jax.ShapeDtypeStruct((8, ROUTER_LANES), F32)],
        scratch_shapes=[pltpu.VMEM((8, ROUTER_LANES), F32)],
        compiler_params=_params("arbitrary"),
        name="moe_router",
    )(x, w_r, b_r)


def _router_weights(w_rg, b_rg, w_re, b_re):
    d = w_rg.shape[0]
    pad = ROUTER_LANES - MOE_GROUPS - MOE_EXPERTS
    w = jnp.concatenate([w_rg, w_re, jnp.zeros((d, pad), F32)], axis=1)
    b = jnp.concatenate([b_rg, b_re, jnp.zeros((pad,), F32)]).reshape(1, ROUTER_LANES)
    w_hi = w.astype(BF16)
    w_lo = (w - w_hi.astype(F32)).astype(BF16)
    return jnp.stack([w_hi, w_lo]), b


def _moe_ffn_body(gid_ref, src_ref, x_hbm, c_ref, w1_ref, w3_ref, w2_ref, g_ref, b_ref, o_hbm,
                  w1b, w3b, w2b, xbuf, obuf, sem_in, sem_out, *, tm, n_tiles, m_rows):
    i = pl.program_id(0)
    slot = i % 2
    other = 1 - slot

    def fetch_rows(tile, buf_slot):
        for r in range(tm):
            row = jnp.maximum(src_ref[tile * tm + r], 0)
            pltpu.make_async_copy(x_hbm.at[pl.ds(row, 1)], xbuf.at[buf_slot, pl.ds(r, 1)], sem_in.at[buf_slot]).start()

    def wait_fetch(buf_slot):
        pltpu.make_async_copy(x_hbm.at[pl.ds(0, tm)], xbuf.at[buf_slot], sem_in.at[buf_slot]).wait()

    def wait_write_back(buf_slot):
        pltpu.make_async_copy(obuf.at[buf_slot], o_hbm.at[pl.ds(0, tm)], sem_out.at[buf_slot]).wait()

    @pl.when(i == 0)
    def _():
        fetch_rows(0, 0)

    fetch_rows(jnp.minimum(i + 1, n_tiles - 1), other)
    wait_fetch(slot)

    @pl.when((i == 0) | (gid_ref[i] != gid_ref[jnp.maximum(i - 1, 0)]))
    def _():
        for e in range(MOE_EPG):
            w1b[e] = w1_ref[0, e].astype(BF16)
            w3b[e] = w3_ref[0, e].astype(BF16)
            w2b[e] = w2_ref[0, e].astype(BF16)

    x = xbuf[slot]
    xb = x.astype(BF16)
    comb = c_ref[...]
    y = jnp.zeros(x.shape, F32)
    for e in range(MOE_EPG):
        a = jnp.dot(xb, w1b[e], preferred_element_type=F32)
        gte = jnp.dot(xb, w3b[e], preferred_element_type=F32)
        hcol = a * (1.0 / (1.0 + jnp.exp(-a))) * gte * comb[:, e:e + 1]
        y = y + jnp.dot(hcol.astype(BF16), w2b[e], preferred_element_type=F32)

    @pl.when(i >= 2)
    def _():
        wait_write_back(slot)

    obuf[slot] = _layer_norm(ALPHA * x + y, g_ref[...], b_ref[...])
    for r in range(tm):
        tok = src_ref[i * tm + r]
        row = jnp.where(tok >= 0, tok, m_rows + slot * tm + r)
        pltpu.make_async_copy(obuf.at[slot, pl.ds(r, 1)], o_hbm.at[pl.ds(row, 1)], sem_out.at[slot]).start()

    @pl.when(i == n_tiles - 1)
    def _():
        wait_fetch(other)
        if n_tiles >= 2:
            wait_write_back(other)
        wait_write_back(slot)


def moe_layer(x, router_w, router_b, w1, w3, w2, layer, g, b, tm, router_tm):
    m, d = x.shape
    r, cnt = moe_router(x, router_w, router_b, tm=router_tm)
    gid, rank = r[:, 0].astype(I32), r[:, 1].astype(I32)
    comb = r[:, COMB_LANE0:COMB_LANE0 + MOE_EPG]
    counts = cnt[0, :MOE_GROUPS].astype(I32)
    tiles = (counts + tm - 1) // tm
    tile_end = jnp.cumsum(tiles)
    start = (tile_end - tiles) * tm
    dest = rank
    for grp in range(MOE_GROUPS):
        dest = dest + jnp.where(gid == grp, start[grp], 0)
    n_tiles = -(-m // tm) + MOE_GROUPS
    mp = n_tiles * tm
    src = jnp.full((mp,), -1, I32).at[dest].set(jnp.arange(m, dtype=I32))
    cs = jnp.take(comb, jnp.maximum(src, 0), axis=0, mode="clip") * (src >= 0).astype(F32)[:, None]
    tile_gid = jnp.minimum((jnp.arange(n_tiles, dtype=I32)[:, None] >= tile_end[None, :]).sum(axis=1),
                           MOE_GROUPS - 1).astype(I32)
    wspec = lambda shp: pl.BlockSpec((1,) + shp, lambda i, gid_, src_: (layer, gid_[i], 0, 0),
                                     pipeline_mode=pl.Buffered(1))
    const = lambda i, gid_, src_: (0, 0)
    out = pl.pallas_call(
        functools.partial(_moe_ffn_body, tm=tm, n_tiles=n_tiles, m_rows=m),
        grid_spec=pltpu.PrefetchScalarGridSpec(
            num_scalar_prefetch=2,
            grid=(n_tiles,),
            in_specs=[pl.BlockSpec(memory_space=pl.ANY),
                      pl.BlockSpec((tm, MOE_EPG), lambda i, gid_, src_: (i, 0)),
                      wspec((MOE_EPG, d, MOE_FF)), wspec((MOE_EPG, d, MOE_FF)), wspec((MOE_EPG, MOE_FF, d)),
                      pl.BlockSpec((1, d), const), pl.BlockSpec((1, d), const)],
            out_specs=pl.BlockSpec(memory_space=pl.ANY),
            scratch_shapes=[pltpu.VMEM((MOE_EPG, d, MOE_FF), BF16), pltpu.VMEM((MOE_EPG, d, MOE_FF), BF16),
                            pltpu.VMEM((MOE_EPG, MOE_FF, d), BF16),
                            pltpu.VMEM((2, tm, d), F32), pltpu.VMEM((2, tm, d), F32),
                            pltpu.SemaphoreType.DMA((2,)), pltpu.SemaphoreType.DMA((2,))]),
        out_shape=jax.ShapeDtypeStruct((m + 2 * tm, d), F32),
        compiler_params=_params("arbitrary"),
        name="moe_ffn",
    )(tile_gid, src, x, cs, w1, w3, w2, g.reshape(1, d), b.reshape(1, d))
    return out


NSA_ROWS = NSA_HEADS
SEL_SAMPLES = 8


def _softmax_rows(s, ok):
    s = jnp.where(ok, s, NEG)
    m = jnp.max(s, axis=1, keepdims=True)
    p = jnp.where(ok, jnp.exp(s - m), 0.0)
    l = jnp.sum(p, axis=1, keepdims=True)
    return p / jnp.where(l > 0.0, l, 1.0)


def _nsa_sample_select_body(qx_ref, kvc_ref, pool_ref, idx_ref, *, n_cmp, n_sel, q_pos):
    n_rows = kvc_ref.shape[1]
    lane = lax.broadcasted_iota(I32, (NSA_ROWS, n_rows), 1)
    ok_c = (lane * CMP_STRIDE + (CMP_LEN - 1) <= q_pos) & (lane < n_cmp)
    for s_i in range(SEL_SAMPLES):
        kc = kvc_ref[s_i, :, 0:NSA_KV_W].astype(BF16)
        s = lax.dot_general(qx_ref[s_i], kc, _NT, preferred_element_type=F32)
        p = _softmax_rows(s, ok_c)
        imp = p
        for g in range(1, NSA_GROUP):
            imp = imp + pltpu.roll(p, g * NSA_KV_HEADS, 0)
        blk = jnp.dot(imp, pool_ref[...], preferred_element_type=F32, precision=lax.Precision.HIGHEST)
        score = jnp.where(lane == q_pos // SEL_BLOCK, NSA_GROUP + 2.0,
                          jnp.where(lane == 0, NSA_GROUP + 1.0,
                                    jnp.where(lane * SEL_BLOCK <= q_pos, blk, -1.0)))
        score = jnp.where(lane < n_sel, score, -3.0)
        idx = jnp.full(score.shape, -1, I32)
        for k in range(min(TOP_N, n_sel)):
            mx = jnp.max(score, axis=1, keepdims=True)
            first = jnp.min(jnp.where(score == mx, lane, n_rows), axis=1, keepdims=True)
            idx = jnp.where(lane == k, jnp.where(mx >= 0.0, first, -1), idx)
            score = jnp.where(lane == first, -2.0, score)
        idx_ref[s_i] = idx[0:8, :]


def nsa_sample_select(qx, kvc, n_cmp, n_sel, q_pos):
    n, n_rows, w = kvc.shape
    r_sel = SEL_BLOCK // CMP_STRIDE
    pool = (jnp.arange(n_rows)[:, None] // r_sel == jnp.arange(n_rows)[None, :]).astype(F32)
    return pl.pallas_call(
        functools.partial(_nsa_sample_select_body, n_cmp=n_cmp, n_sel=n_sel, q_pos=q_pos),
        grid=(n // SEL_SAMPLES,),
        in_specs=[pl.BlockSpec((SEL_SAMPLES, NSA_ROWS, NSA_KV_W), lambda i: (i, 0, 0)),
                  pl.BlockSpec((SEL_SAMPLES, n_rows, w), lambda i: (i, 0, 0)),
                  pl.BlockSpec((n_rows, n_rows), lambda i: (0, 0))],
        out_specs=pl.BlockSpec((SEL_SAMPLES, 8, n_rows), lambda i: (i, 0, 0)),
        out_shape=jax.ShapeDtypeStruct((n, 8, n_rows), I32),
        compiler_params=_params("arbitrary"),
        name="nsa_sample_select",
    )(qx, kvc, pool)


Q_ROWS = 8


def _nsa_sample_attend_body(page_ref, half_ref, qh_ref, kvc_ref, win_ref, ksn_ref, kwn_ref, g_ref, *rest, n_cmp, q_pos):
    blk_refs, o_ref = rest[:-1], rest[-1]
    b = pl.program_id(0)
    dh = NSA_HEAD_DIM
    rnd = lambda a: a.astype(BF16).astype(F32)
    n_top = len(blk_refs) // NSA_KV_HEADS
    page = blk_refs[0].shape[-1]
    n_rows = kvc_ref.shape[1]
    lane_c = lax.broadcasted_iota(I32, (Q_ROWS, n_rows), 1)
    ok_c = (lane_c * CMP_STRIDE + (CMP_LEN - 1) <= q_pos) & (lane_c < n_cmp)
    lane_s = lax.broadcasted_iota(I32, (Q_ROWS, n_top * page), 1)
    gate = 1.0 / (1.0 + jnp.exp(-g_ref[0]))

    for h in range(NSA_KV_HEADS):
        q = qh_ref[0, h]
        qf = q.astype(F32)

        def with_new_key(k_t, v_t, ok, new_ref):
            s = jnp.dot(q, k_t, preferred_element_type=F32)
            s_new = jnp.sum(qf * rnd(new_ref[0, :, h * dh:(h + 1) * dh]), axis=1, keepdims=True)
            if ok is not None:
                s = jnp.where(ok, s, NEG)
            m = jnp.maximum(jnp.max(s, axis=1, keepdims=True), s_new)
            p = jnp.exp(s - m) if ok is None else jnp.where(ok, jnp.exp(s - m), 0.0)
            p_new = jnp.exp(s_new - m)
            l = jnp.sum(p, axis=1, keepdims=True) + p_new
            v_new = new_ref[0, :, NSA_KV_W + h * dh:NSA_KV_W + (h + 1) * dh]
            o = lax.dot_general(p.astype(BF16), v_t, _NT, preferred_element_type=F32)
            return (o + rnd(p_new) * rnd(v_new)) / l

        kc = kvc_ref[0, :, h * dh:(h + 1) * dh].astype(BF16)
        vc = kvc_ref[0, :, NSA_KV_W + h * dh:NSA_KV_W + (h + 1) * dh].astype(BF16)
        s = lax.dot_general(q, kc, _NT, preferred_element_type=F32)
        o_c = jnp.dot(_softmax_rows(s, ok_c).astype(BF16), vc, preferred_element_type=F32)

        o_w = with_new_key(win_ref[0, 0, h].astype(BF16), win_ref[0, 1, h].astype(BF16), None, kwn_ref)

        refs = blk_refs[h * n_top:(h + 1) * n_top]
        k_t = jnp.concatenate([r[0, 0, 0] for r in refs], axis=1).astype(BF16)
        v_t = jnp.concatenate([r[0, 1, 0] for r in refs], axis=1).astype(BF16)
        want = jnp.full(lane_s.shape, -1, I32)
        for k in range(n_top):
            want = jnp.where(lane_s // page == k, half_ref[(b * NSA_KV_HEADS + h) * n_top + k], want)
        o_s = with_new_key(k_t, v_t, (lane_s % page) // SEL_BLOCK == want, ksn_ref)

        o = gate[h, :, 0:1] * o_c + gate[h, :, 1:2] * o_s + gate[h, :, 2:3] * o_w
        o_ref[0, h] = o.astype(o_ref.dtype)


def nsa_sample_mixer(x, cache_cmp_t, cache_sel_t, win_t, page_table, w_in_p, r_cmp, b_cmp):
    n, d = x.shape
    n_pages = page_table.shape[1]
    past = n_pages * PAGE_SIZE
    dh, kvh, grp = NSA_HEAD_DIM, NSA_KV_HEADS, NSA_GROUP
    q, kv_c, kv_s, kv_w, gates = matmul_split(x, w_in_p, _NSA_SPLITS, _NSA_DTYPES, tm=n)
    n_cmp = (past + 1 - CMP_LEN) // CMP_STRIDE + 1
    n_sel = -(-(past + 1) // SEL_BLOCK)
    n_past_blk = past // SEL_BLOCK
    kvc = nsa_compress_paged(cache_cmp_t, page_table, r_cmp, b_cmp)
    q4 = q.reshape(n, kvh, grp, dh)
    q4t = q4.transpose(0, 2, 1, 3)
    qx = (q4t[:, :, :, None, :] * jnp.eye(kvh, dtype=q.dtype)[None, None, :, :, None]).reshape(n, NSA_ROWS, kvh * dh)
    idx = nsa_sample_select(qx, kvc, n_cmp, n_sel, past)[:, :kvh, :TOP_N]
    sub = PAGE_SIZE // SEL_BLOCK
    is_past = (idx >= 0) & (idx < n_past_blk)
    phys = jnp.take_along_axis(page_table, jnp.clip(idx // sub, 0, n_pages - 1).reshape(n, -1), axis=1).reshape(idx.shape)
    page_idx = jnp.where(is_past, phys, 0).astype(I32).reshape(-1)
    half = jnp.where(is_past, idx % sub, -1).astype(I32).reshape(-1)
    pad_rows = lambda a: jnp.pad(a, ((0, 0), (0, 0), (0, Q_ROWS - grp), (0, 0)))
    qh = pad_rows(q4)
    gt = pad_rows(gates[:, :NSA_HEADS * 3].reshape(n, kvh, grp, 3))
    n_top = idx.shape[2]
    per_seq = kvh * n_top
    vec = lambda a: a.reshape(n, 1, a.shape[-1])
    blk_spec = lambda k: pl.BlockSpec((1, 2, 1, dh, PAGE_SIZE),
                                      lambda b, p_, h_: (p_[b * per_seq + k], 0, k // n_top, 0, 0))
    seq = lambda a: pl.BlockSpec((1,) + a.shape[1:], lambda b, p_, h_: (b,) + (0,) * (a.ndim - 1))
    ins = (qh, kvc, win_t, vec(kv_s), vec(kv_w), gt)
    out = pl.pallas_call(
        functools.partial(_nsa_sample_attend_body, n_cmp=n_cmp, q_pos=past),
        grid_spec=pltpu.PrefetchScalarGridSpec(
            num_scalar_prefetch=2,
            grid=(n,),
            in_specs=[seq(a) for a in ins] + [blk_spec(k) for k in range(per_seq)],
            out_specs=pl.BlockSpec((1, kvh, Q_ROWS, dh), lambda b, p_, h_: (b, 0, 0, 0))),
        out_shape=jax.ShapeDtypeStruct((n, kvh, Q_ROWS, dh), BF16),
        compiler_params=_params("arbitrary"),
        name="nsa_sample_attend",
    )(page_idx, half, *ins, *([cache_sel_t] * per_seq))
    return out[:, :, :grp].reshape(n, NSA_Q_W), kv_c, kv_s, kv_w


PROMPT_TM = 512
ROUTER_TM = 384


def kernel(x_prompt, x_sample, cache_nsa_cmp, cache_nsa_sel, state_nsa_win, state_ret, state_pool, state_conv, page_table, nsa_w_in, nsa_w_cmp, nsa_b_cmp, nsa_w_o, ret_w_in, ret_gn_g, ret_gn_b, ret_w_o, pool_w, pool_scale, conv_w_in, conv_w, conv_w_out, ln_g, ln_b, moe_w_rg, moe_b_rg, moe_w_re, moe_b_re, moe_w1, moe_w3, moe_w2):
    bp, t, d = x_prompt.shape
    ns = x_sample.shape[0]
    assert x_sample.shape[1] == 1 and (bp * t) % PROMPT_TM == 0 and (bp * t + ns) % ROUTER_TM == 0
    past = page_table.shape[1] * PAGE_SIZE
    xp, xs = x_prompt.reshape(bp * t, d), x_sample.reshape(ns, d)
    kv5 = lambda a, n, rows: a.reshape(n, rows, 2, NSA_KV_HEADS, NSA_HEAD_DIM)
    shift_in = lambda old, new: jnp.concatenate([old[:, 1:], new[:, None]], axis=1)
    cmp_p, sel_p, win_p, ret_p, pool_p, conv_p = [], [], [], [], [], []
    cmp_s, sel_s, win_s, ret_s, pool_s, conv_s = [], [], [], [], [], []
    n_mix = 4
    for i in range(DEPTH):
        kind, j = i % n_mix, i // n_mix
        g0, b0 = ln_g[i, 0], ln_b[i, 0]
        if kind == 0:
            w_in_p, r_cmp, w_o = _nsa_w_in_padded(nsa_w_in[j]), _cmp_weights(nsa_w_cmp[j]), nsa_w_o[j].astype(BF16)
            o, a, b, c = nsa_prompt_mixer(xp.reshape(bp, t, d), w_in_p, r_cmp, nsa_b_cmp[j])
            cmp_p.append(kv5(a, bp, t)); sel_p.append(kv5(b, bp, t)); win_p.append(kv5(c, bp, t)[:, -min(WINDOW, t):])
            xp = matmul_res_ln(o, w_o, xp, g0, b0, PROMPT_TM)
            win = state_nsa_win[j]
            rows_minor = lambda a: jnp.transpose(a, (0, 2, 3, 4, 1))
            o, a, b, c = nsa_sample_mixer(xs, rows_minor(cache_nsa_cmp[j]), rows_minor(cache_nsa_sel[j]), rows_minor(win),
                                          page_table, w_in_p, r_cmp, nsa_b_cmp[j])
            cmp_s.append(kv5(a, ns, 1)); sel_s.append(kv5(b, ns, 1)); win_s.append(shift_in(win, kv5(c, ns, 1)[:, 0]))
            xs = matmul_res_ln(o, w_o, xs, g0, b0, ns)
        elif kind == 1:
            w_in, w_o = ret_w_in[j].astype(BF16), ret_w_o[j].astype(BF16)
            q, k, v, g = matmul_split(xp, w_in, _RET_SPLITS, _RET_DTYPES, tm=256)
            og, s_fin = retention_prompt(q, k, v, g, ret_gn_g[j], ret_gn_b[j], bp, t)
            ret_p.append(s_fin)
            xp = matmul_res_ln(og, w_o, xp, g0, b0, PROMPT_TM)
            q, k, v, g = matmul_split(xs, w_in, _RET_SPLITS, _RET_DTYPES, tm=ns)
            og, s_new = retention_sample(q, k, v, g, ret_gn_g[j], ret_gn_b[j], state_ret[j], past)
            ret_s.append(s_new)
            xs = matmul_res_ln(og, w_o, xs, g0, b0, ns)
        elif kind == 2:
            wp = pool_w[j].astype(BF16)
            pool_p.append(xp.reshape(bp, t, d)[:, -(POOL_MAX - 1):])
            xp = pool_prompt(xp.reshape(bp, t, d), wp, pool_scale[j], g0, b0).reshape(bp * t, d)
            pool_s.append(shift_in(state_pool[j], xs))
            xs = pool_sample(xs, state_pool[j].transpose(1, 0, 2), wp, pool_scale[j], g0, b0, past)
        else:
            w_in, w_out = conv_w_in[j].astype(BF16), conv_w_out[j].astype(BF16)
            bg, cg, h = matmul_split(xp, w_in, _CONV_SPLITS, _CONV_DTYPES, tm=PROMPT_TM)
            a, tail = conv_prompt(bg, cg, h, conv_w[j], bp, t)
            conv_p.append(tail[:, -(CONV_W - 1):])
            xp = matmul_res_ln(a, w_out, xp, g0, b0, PROMPT_TM)
            bg, cg, h = matmul_split(xs, w_in, _CONV_SPLITS, _CONV_DTYPES, tm=ns)
            a, u = conv_sample(bg, cg, h, state_conv[j].transpose(1, 0, 2), conv_w[j])
            conv_s.append(shift_in(state_conv[j], u))
            xs = matmul_res_ln(a, w_out, xs, g0, b0, ns)
        rw, rb = _router_weights(moe_w_rg[i], moe_b_rg[i], moe_w_re[i], moe_b_re[i])
        out = moe_layer(jnp.concatenate([xp, xs], axis=0), rw, rb, moe_w1, moe_w3, moe_w2, i,
                        ln_g[i, 1], ln_b[i, 1], tm=PROMPT_TM, router_tm=ROUTER_TM)
        xp, xs = out[:bp * t], out[bp * t:bp * t + ns]
    st = jnp.stack
    return (xp.reshape(bp, t, d), xs.reshape(ns, 1, d), st(cmp_p), st(sel_p), st(win_p), st(ret_p), st(pool_p), st(conv_p),
            st(cmp_s), st(sel_s), st(win_s), st(ret_s), st(pool_s), st(conv_s))
```

```python
import functools

import jax
import jax.numpy as jnp
import numpy as np
from jax import lax
from jax.experimental import pallas as pl
from jax.experimental.pallas import tpu as pltpu

F32, BF16, I32 = jnp.float32, jnp.bfloat16, jnp.int32

D_MODEL = 1024
DEPTH = 4
PAGE_SIZE = 128
NSA_HEADS = 16
NSA_HEAD_DIM = 64
NSA_KV_HEADS = 4
NSA_GROUP = NSA_HEADS // NSA_KV_HEADS
NSA_KV_W = NSA_KV_HEADS * NSA_HEAD_DIM
NSA_Q_W = NSA_HEADS * NSA_HEAD_DIM
CMP_LEN = 32
CMP_STRIDE = 16
SEL_BLOCK = 64
TOP_N = 8
WINDOW = 512
RET_HEADS = 4
RET_DK = D_MODEL // RET_HEADS
RET_DV = 2 * D_MODEL // RET_HEADS
RET_CHUNK = 128
ROPE_BASE = 10000.0
POOL_WINDOWS = (2, 4, 8, 16)
POOL_GROUP = D_MODEL // len(POOL_WINDOWS)
POOL_MAX = max(POOL_WINDOWS)
CONV_W = 3
MOE_GROUPS = 4
MOE_EPG = 8
MOE_EXPERTS = MOE_GROUPS * MOE_EPG
MOE_FF = D_MODEL // 4
ALPHA = (2.0 * DEPTH) ** 0.25
LN_EPS = 1e-5
NEG = -1e30

LANES = 128
Q_TILE = 128
KV_TILE = 256
Q_TILES_PER_CALL = 4
VMEM_LIMIT = 56 * 1024 * 1024

_NT = (((1,), (1,)), ((), ()))
_TN = (((0,), (0,)), ((), ()))


def _params(*sem):
    return pltpu.CompilerParams(dimension_semantics=sem, vmem_limit_bytes=VMEM_LIMIT)


def _mm_body(x_ref, w_ref, *o_refs, splits, chunk):
    x = x_ref[...].astype(BF16)
    col = 0
    for o_ref, width in zip(o_refs, splits):
        for j in range(0, width, chunk):
            c = min(chunk, width - j)
            y = jnp.dot(x, w_ref[:, col + j:col + j + c], preferred_element_type=F32)
            o_ref[:, j:j + c] = y.astype(o_ref.dtype)
        col += width


def matmul_split(x, w, splits, dtypes, tm, chunk=512):
    m, k = x.shape
    n = w.shape[1]
    assert n == sum(splits) and m % tm == 0
    return pl.pallas_call(
        functools.partial(_mm_body, splits=tuple(splits), chunk=chunk),
        grid=(m // tm,),
        in_specs=[pl.BlockSpec((tm, k), lambda i: (i, 0)), pl.BlockSpec((k, n), lambda i: (0, 0))],
        out_specs=[pl.BlockSpec((tm, s), lambda i: (i, 0)) for s in splits],
        out_shape=[jax.ShapeDtypeStruct((m, s), d) for s, d in zip(splits, dtypes)],
        compiler_params=_params("arbitrary"),
        name="matmul_split",
    )(x, w)


def _layer_norm(v, g, b):
    mu = jnp.mean(v, axis=-1, keepdims=True)
    c = v - mu
    var = jnp.mean(c * c, axis=-1, keepdims=True)
    return c * lax.rsqrt(var + LN_EPS) * g + b


def _mm_res_ln_body(a_ref, w_ref, x_ref, g_ref, b_ref, o_ref):
    y = jnp.dot(a_ref[...].astype(BF16), w_ref[...], preferred_element_type=F32)
    o_ref[...] = _layer_norm(ALPHA * x_ref[...] + y, g_ref[...], b_ref[...])


def matmul_res_ln(a, w, x, g, b, tm):
    m, k = a.shape
    d = x.shape[1]
    return pl.pallas_call(
        _mm_res_ln_body,
        grid=(m // tm,),
        in_specs=[pl.BlockSpec((tm, k), lambda i: (i, 0)), pl.BlockSpec((k, d), lambda i: (0, 0)),
                  pl.BlockSpec((tm, d), lambda i: (i, 0)), pl.BlockSpec((1, d), lambda i: (0, 0)),
                  pl.BlockSpec((1, d), lambda i: (0, 0))],
        out_specs=pl.BlockSpec((tm, d), lambda i: (i, 0)),
        out_shape=jax.ShapeDtypeStruct((m, d), F32),
        compiler_params=_params("arbitrary"),
        name="matmul_res_ln",
    )(a, w, x, g.reshape(1, d), b.reshape(1, d))


def _cmp_weights(w_cmp):
    dh = NSA_HEAD_DIM
    w6 = w_cmp.reshape(2, 2, CMP_STRIDE // 2, 2, dh, dh)
    eye = jnp.eye(2, dtype=w_cmp.dtype)
    r = jnp.einsum("chpsde,xy->cpsxdhye", w6, eye)
    return r.reshape(2, CMP_STRIDE // 2, 4 * dh, 4 * dh).astype(BF16)


def _cmp_body(*refs):
    x_refs, (r_ref, b_ref, o_ref) = refs[:-3], refs[-3:]
    row_w = 2 * NSA_KV_W

    def cols(c0):
        return jnp.concatenate([x_ref[0, :, c0:c0 + LANES] for x_ref in x_refs], axis=0)

    for c in range(2):
        for hp in range(2):
            acc = None
            for sp in range(CMP_STRIDE // 2):
                c0 = (2 * sp) * row_w + c * NSA_KV_W + hp * LANES
                lhs = jnp.concatenate([cols(c0), cols(c0 + row_w)], axis=1).astype(BF16)
                part = jnp.dot(lhs, r_ref[c, sp], preferred_element_type=F32)
                acc = part if acc is None else acc + part
            lo, hi = acc[:, :LANES], acc[:, LANES:]
            nxt = pltpu.roll(hi, hi.shape[0] - 1, 0)
            o0 = c * NSA_KV_W + hp * LANES
            o_ref[0, :, o0:o0 + LANES] = lo + nxt + b_ref[:, o0:o0 + LANES]


def nsa_compress(rows, r_w, b_cmp):
    bsz, t, w = rows.shape
    n_ch = t // CMP_STRIDE
    x = rows.reshape(bsz, n_ch, CMP_STRIDE * w)
    bias = jnp.broadcast_to(b_cmp[:, None, :], (2, NSA_KV_HEADS, NSA_HEAD_DIM)).reshape(1, w)
    return pl.pallas_call(
        _cmp_body,
        grid=(bsz,),
        in_specs=[pl.BlockSpec((1, n_ch, CMP_STRIDE * w), lambda b: (b, 0, 0)),
                  pl.BlockSpec(r_w.shape, lambda b: (0, 0, 0, 0)),
                  pl.BlockSpec((1, w), lambda b: (0, 0))],
        out_specs=pl.BlockSpec((1, n_ch, w), lambda b: (b, 0, 0)),
        out_shape=jax.ShapeDtypeStruct((bsz, n_ch, w), F32),
        compiler_params=_params("arbitrary"),
        name="nsa_compress",
    )(x, r_w, bias)


def _cmp_paged_body(pt_ref, *refs):
    page_refs, (r_ref, b_ref, o_ref, rows_scr) = refs[:-4], refs[-4:]
    page = page_refs[0].shape[-1]
    for pi, p_ref in enumerate(page_refs):
        for c in range(2):
            for hp in range(2):
                pair = p_ref[0, c, 2 * hp:2 * hp + 2].reshape(2 * NSA_HEAD_DIM, page)
                rows_scr[c * 2 + hp, pi * page:(pi + 1) * page, :] = pair.T
    n_ch = o_ref.shape[1]
    for c in range(2):
        for hp in range(2):
            acc = None
            for sp in range(CMP_STRIDE // 2):
                pos = [rows_scr[c * 2 + hp, pl.ds(2 * sp + sl, n_ch, stride=CMP_STRIDE), :] for sl in range(2)]
                part = jnp.dot(jnp.concatenate(pos, axis=1).astype(BF16), r_ref[c, sp], preferred_element_type=F32)
                acc = part if acc is None else acc + part
            lo, hi = acc[:, :LANES], acc[:, LANES:]
            nxt = pltpu.roll(hi, hi.shape[0] - 1, 0)
            o0 = c * NSA_KV_W + hp * LANES
            o_ref[0, :, o0:o0 + LANES] = lo + nxt + b_ref[:, o0:o0 + LANES]


def nsa_compress_paged(cache_t, page_table, r_w, b_cmp):
    n_phys, _, kvh, dh, page = cache_t.shape
    n, n_pages = page_table.shape
    w = 2 * kvh * dh
    n_ch = n_pages * page // CMP_STRIDE
    bias = jnp.broadcast_to(b_cmp[:, None, :], (2, kvh, dh)).reshape(1, w)
    page_spec = lambda k: pl.BlockSpec((1, 2, kvh, dh, page), lambda b, pt: (pt[b * n_pages + k], 0, 0, 0, 0))
    return pl.pallas_call(
        _cmp_paged_body,
        grid_spec=pltpu.PrefetchScalarGridSpec(
            num_scalar_prefetch=1,
            grid=(n,),
            in_specs=[page_spec(k) for k in range(n_pages)]
            + [pl.BlockSpec(r_w.shape, lambda b, pt: (0, 0, 0, 0)), pl.BlockSpec((1, w), lambda b, pt: (0, 0))],
            out_specs=pl.BlockSpec((1, n_ch, w), lambda b, pt: (b, 0, 0)),
            scratch_shapes=[pltpu.VMEM((4, n_pages * page, LANES), F32)]),
        out_shape=jax.ShapeDtypeStruct((n, n_ch, w), F32),
        compiler_params=_params("arbitrary"),
        name="nsa_compress_paged",
    )(page_table.reshape(-1), *([cache_t] * n_pages), r_w, bias)


def _softmax_cols(s, ok):
    s = jnp.where(ok, s, NEG)
    m = jnp.max(s, axis=0, keepdims=True)
    p = jnp.where(ok, jnp.exp(s - m), 0.0)
    l = jnp.sum(p, axis=0, keepdims=True)
    return p / jnp.where(l > 0.0, l, 1.0)


def _nsa_prompt_body(q_ref, kc_ref, vct_ref, ks_ref, vst_ref, kw_ref, vwt_ref, g_ref, o_ref, imp_ref, bias_ref, s_ref,
                     *, n_cmp, n_sel, i0, n_sel_tiles):
    i = i0 + pl.program_id(2)
    t0 = i * Q_TILE
    gq = NSA_GROUP * Q_TILE
    tq = t0 + lax.broadcasted_iota(I32, (1, Q_TILE), 1)
    tq4 = t0 + (lax.broadcasted_iota(I32, (1, gq), 1) & (Q_TILE - 1))
    q = q_ref[0, 0, 0]

    n_rows = kc_ref.shape[2]
    s = lax.dot_general(kc_ref[0, 0], q, _NT, preferred_element_type=F32)
    n_idx = lax.broadcasted_iota(I32, (n_rows, 1), 0)
    ok_c = (n_idx * CMP_STRIDE + (CMP_LEN - 1) <= tq4) & (n_idx < n_cmp)
    p_c = _softmax_cols(s, ok_c)
    o_c = jnp.dot(vct_ref[0, 0], p_c.astype(BF16), preferred_element_type=F32)

    imp = p_c[:, 0:Q_TILE]
    for g in range(1, NSA_GROUP):
        imp = imp + p_c[:, g * Q_TILE:(g + 1) * Q_TILE]
    imp_ref[...] = imp
    r_sel = SEL_BLOCK // CMP_STRIDE
    n_blk = n_rows // r_sel
    blk_imp = imp_ref[pl.ds(0, n_blk, stride=r_sel), :]
    for r in range(1, r_sel):
        blk_imp = blk_imp + imp_ref[pl.ds(r, n_blk, stride=r_sel), :]
    j_idx = lax.broadcasted_iota(I32, (n_blk, 1), 0)
    score = jnp.where(j_idx == tq // SEL_BLOCK, NSA_GROUP + 2.0,
                      jnp.where(j_idx == 0, NSA_GROUP + 1.0,
                                jnp.where(j_idx * SEL_BLOCK <= tq, blk_imp, -1.0)))
    score = jnp.where(j_idx < n_sel, score, -3.0)
    sel = jnp.zeros((n_blk, Q_TILE), F32)
    for _ in range(min(TOP_N, n_sel)):
        mx = jnp.max(score, axis=0, keepdims=True)
        first = jnp.min(jnp.where(score == mx, j_idx, n_blk), axis=0, keepdims=True)
        pick = j_idx == first
        sel = jnp.where(pick & (mx >= 0.0), 1.0, sel)
        score = jnp.where(pick, -2.0, score)
    blk_per_tile = KV_TILE // SEL_BLOCK
    for j in range(min(n_sel, n_sel_tiles * blk_per_tile)):
        bias_ref[j * SEL_BLOCK:(j + 1) * SEL_BLOCK, :] = jnp.broadcast_to(
            jnp.where(sel[j:j + 1, :] > 0.0, 0.0, NEG), (SEL_BLOCK, Q_TILE))

    def attend(k_ref, vt_ref, tiles, bias_fn):
        m = jnp.full((1, gq), NEG, F32)
        for n, kt in enumerate(tiles):
            k0 = kt * KV_TILE if isinstance(kt, int) else pl.multiple_of(kt * KV_TILE, KV_TILE)
            s = lax.dot_general(k_ref[0, 0, pl.ds(k0, KV_TILE), :], q, _NT, preferred_element_type=F32)
            kpos = k0 + lax.broadcasted_iota(I32, (KV_TILE, 1), 0)
            s = s + jnp.concatenate([bias_fn(kt, k0, kpos)] * NSA_GROUP, axis=1)
            s_ref[n * KV_TILE:(n + 1) * KV_TILE, :] = s
            m = jnp.maximum(m, jnp.max(s, axis=0, keepdims=True))
        acc = jnp.zeros((VT_ROWS, gq), F32)
        for n, kt in enumerate(tiles):
            p = jnp.exp(s_ref[n * KV_TILE:(n + 1) * KV_TILE, :] - m).astype(BF16)
            acc = acc + jnp.dot(vt_ref[0, 0, kt], p, preferred_element_type=F32)
        return acc[0:NSA_HEAD_DIM] / acc[NSA_HEAD_DIM:NSA_HEAD_DIM + 1]

    first_diag = (i0 * Q_TILE) // KV_TILE

    def sel_bias(kt, k0, kpos):
        bias = bias_ref[pl.ds(k0, KV_TILE), :]
        return bias if kt < first_diag else jnp.where(kpos <= tq, bias, NEG)

    def win_bias(kt, k0, kpos):
        dist = tq - kpos
        return jnp.where((dist >= 0) & (dist <= WINDOW), 0.0, NEG)

    o_s = attend(ks_ref, vst_ref, list(range(n_sel_tiles)), sel_bias)
    last = (t0 + Q_TILE - 1) // KV_TILE
    n_win_tiles = WINDOW // KV_TILE + 1
    if (i0 * Q_TILE + Q_TILE - 1) // KV_TILE < n_win_tiles - 1:
        win_tiles = list(range(n_sel_tiles))
    else:
        win_tiles = [last - (n_win_tiles - 1) + r for r in range(n_win_tiles)]
    o_w = attend(kw_ref, vwt_ref, win_tiles, win_bias)

    gate = 1.0 / (1.0 + jnp.exp(-g_ref[0, 0]))
    for g in range(NSA_GROUP):
        sl = slice(g * Q_TILE, (g + 1) * Q_TILE)
        o = gate[g, 0:1, :] * o_c[:, sl] + gate[g, 1:2, :] * o_s[:, sl] + gate[g, 2:3, :] * o_w[:, sl]
        o_ref[0, 0, g] = o.astype(o_ref.dtype)


VT_ROWS = NSA_HEAD_DIM + 16


def _head_major(kv):
    bsz, t, _ = kv.shape
    kv = kv.astype(BF16).reshape(bsz, t, 2, NSA_KV_HEADS, NSA_HEAD_DIM)
    k = kv[:, :, 0].transpose(0, 2, 1, 3)
    vt = kv[:, :, 1].reshape(bsz, t // KV_TILE, KV_TILE, NSA_KV_HEADS, NSA_HEAD_DIM).transpose(0, 3, 1, 4, 2)
    lead = vt.shape[:3]
    ones = jnp.ones(lead + (1, KV_TILE), BF16)
    zeros = jnp.zeros(lead + (VT_ROWS - NSA_HEAD_DIM - 1, KV_TILE), BF16)
    return k, jnp.concatenate([vt, ones, zeros], axis=3)


def nsa_prompt_attention(q, kvc, kv_s, kv_w, gates, n_cmp):
    bsz, t, _ = q.shape
    nq = t // Q_TILE
    n_sel = -(-t // SEL_BLOCK)
    n_rows = kvc.shape[1]
    dh, kvh, grp = NSA_HEAD_DIM, NSA_KV_HEADS, NSA_GROUP
    qh = q.reshape(bsz, nq, Q_TILE, kvh, grp, dh).transpose(0, 3, 1, 4, 2, 5).reshape(bsz, kvh, nq, grp * Q_TILE, dh)
    kc5 = kvc.astype(BF16).reshape(bsz, n_rows, 2, kvh, dh)
    kc = kc5[:, :, 0].transpose(0, 2, 1, 3)
    vct = kc5[:, :, 1].transpose(0, 2, 3, 1)
    ks, vst = _head_major(kv_s)
    kw, vwt = _head_major(kv_w)
    gt = gates[:, :, :NSA_HEADS * 3].reshape(bsz, t, kvh, grp, 3).transpose(0, 2, 3, 4, 1)
    nkt = t // KV_TILE
    assert WINDOW % KV_TILE == 0 and KV_TILE % Q_TILE == 0 and nq % Q_TILES_PER_CALL == 0
    bh = lambda b, h, i: (b, h, 0, 0)
    outs = []
    for i0 in range(0, nq, Q_TILES_PER_CALL):
        n_sel_tiles = ((i0 + Q_TILES_PER_CALL) * Q_TILE - 1) // KV_TILE + 1
        outs.append(pl.pallas_call(
            functools.partial(_nsa_prompt_body, n_cmp=n_cmp, n_sel=n_sel, i0=i0, n_sel_tiles=n_sel_tiles),
            grid=(bsz, kvh, Q_TILES_PER_CALL),
            in_specs=[pl.BlockSpec((1, 1, 1, grp * Q_TILE, dh), lambda b, h, i, i0=i0: (b, h, i0 + i, 0, 0)),
                      pl.BlockSpec((1, 1, n_rows, dh), bh),
                      pl.BlockSpec((1, 1, dh, n_rows), bh),
                      pl.BlockSpec((1, 1, t, dh), bh),
                      pl.BlockSpec((1, 1, nkt, VT_ROWS, KV_TILE), lambda b, h, i: (b, h, 0, 0, 0)),
                      pl.BlockSpec((1, 1, t, dh), bh),
                      pl.BlockSpec((1, 1, nkt, VT_ROWS, KV_TILE), lambda b, h, i: (b, h, 0, 0, 0)),
                      pl.BlockSpec((1, 1, grp, 3, Q_TILE), lambda b, h, i, i0=i0: (b, h, 0, 0, i0 + i))],
            out_specs=pl.BlockSpec((1, 1, grp, dh, Q_TILE), lambda b, h, i: (b, h, 0, 0, i)),
            out_shape=jax.ShapeDtypeStruct((bsz, kvh, grp, dh, Q_TILES_PER_CALL * Q_TILE), BF16),
            scratch_shapes=[pltpu.VMEM((n_rows, Q_TILE), F32), pltpu.VMEM((n_sel * SEL_BLOCK, Q_TILE), F32),
                            pltpu.VMEM((max(n_sel_tiles, WINDOW // KV_TILE + 1) * KV_TILE, grp * Q_TILE), F32)],
            compiler_params=_params("arbitrary", "arbitrary", "arbitrary"),
            name=f"nsa_prompt_attention_q{i0}",
        )(qh, kc, vct, ks, vst, kw, vwt, gt))
    out = jnp.concatenate(outs, axis=-1)
    return out.transpose(0, 4, 1, 2, 3).reshape(bsz, t, kvh * grp * dh)


def _nsa_w_in_padded(w_in):
    wq = w_in[:, :NSA_Q_W] * NSA_HEAD_DIM ** -0.5
    pad = jnp.zeros((w_in.shape[0], LANES - 3 * NSA_HEADS), w_in.dtype)
    return jnp.concatenate([wq, w_in[:, NSA_Q_W:], pad], axis=1).astype(BF16)


_NSA_SPLITS = (NSA_Q_W, 2 * NSA_KV_W, 2 * NSA_KV_W, 2 * NSA_KV_W, LANES)
_NSA_DTYPES = (BF16, F32, F32, F32, F32)


def nsa_prompt_mixer(x, w_in_p, r_cmp, b_cmp):
    bsz, t, d = x.shape
    q, kv_c, kv_s, kv_w, gates = matmul_split(x.reshape(bsz * t, d), w_in_p, _NSA_SPLITS, _NSA_DTYPES, tm=min(512, bsz * t))
    r3 = lambda a: a.reshape(bsz, t, a.shape[-1])
    kv_c, kv_s, kv_w = r3(kv_c), r3(kv_s), r3(kv_w)
    n_cmp = (t - CMP_LEN) // CMP_STRIDE + 1
    kvc = nsa_compress(kv_c, r_cmp, b_cmp)
    o = nsa_prompt_attention(r3(q), kvc, kv_s, kv_w, r3(gates), n_cmp)
    return o.reshape(bsz * t, NSA_Q_W), kv_c, kv_s, kv_w


def _ret_log_decay():
    return jnp.log(1.0 - 2.0 ** (-5.0 - jnp.arange(RET_HEADS, dtype=F32)))


def _rope_tables(pos):
    half = RET_DK // 2
    inv = ROPE_BASE ** (-jnp.linspace(0.0, 1.0, half, dtype=F32))
    ang = pos.astype(F32)[:, None] * inv[None, :]
    return jnp.cos(ang), jnp.sin(ang)


def _rope(x, cos, sin):
    half = RET_DK // 2
    x1, x2 = x[:, :half], x[:, half:]
    return jnp.concatenate([x1 * cos - x2 * sin, x2 * cos + x1 * sin], axis=1)


def _group_norm_gate(o, gate, g, b):
    mu = jnp.mean(o, axis=-1, keepdims=True)
    c = o - mu
    var = jnp.mean(c * c, axis=-1, keepdims=True)
    on = c * lax.rsqrt(var + LN_EPS) * g + b
    return gate * (1.0 / (1.0 + jnp.exp(-gate))) * on


def _ret_prompt_body(q_ref, k_ref, v_ref, gt_ref, cos_ref, sin_ref, dm_ref, di_ref, dr_ref, dc_ref, gg_ref, gb_ref,
                     o_ref, s_ref, s_scr):
    c = pl.program_id(1)

    @pl.when(c == 0)
    def _():
        s_scr[...] = jnp.zeros_like(s_scr)

    cos, sin = cos_ref[...], sin_ref[...]
    for h in range(RET_HEADS):
        qk = slice(h * RET_DK, (h + 1) * RET_DK)
        vv = slice(h * RET_DV, (h + 1) * RET_DV)
        q = _rope(q_ref[:, qk], cos, sin)
        k = _rope(k_ref[:, qk], cos, sin) * RET_DK ** -0.5
        qb, v = q.astype(BF16), v_ref[:, vv]
        inner = lax.dot_general(qb, k.astype(BF16), _NT, preferred_element_type=F32) * dm_ref[h]
        o = jnp.dot(inner.astype(BF16), v, preferred_element_type=F32)
        s_old = s_scr[h]
        o = o + jnp.dot(qb, s_old.astype(BF16), preferred_element_type=F32) * di_ref[h]
        kd = (k * dr_ref[h]).astype(BF16)
        s_new = dc_ref[h] * s_old + lax.dot_general(kd, v, _TN, preferred_element_type=F32)
        s_scr[h] = s_new
        s_ref[0, h] = s_new
        o_ref[:, vv] = _group_norm_gate(o, gt_ref[:, vv], gg_ref[:, vv], gb_ref[:, vv]).astype(o_ref.dtype)


def retention_prompt(q, k, v, gate, gn_g, gn_b, bsz, t):
    ch = RET_CHUNK
    n_ch = t // ch
    lg = _ret_log_decay()
    i = jnp.arange(ch, dtype=F32)
    diff = i[:, None] - i[None, :]
    dmask = jnp.where(diff >= 0, jnp.exp(lg[:, None, None] * jnp.maximum(diff, 0.0)), 0.0)
    d_in = jnp.exp((i[None, :] + 1.0) * lg[:, None])[:, :, None]
    d_rev = jnp.exp((ch - 1.0 - i)[None, :] * lg[:, None])[:, :, None]
    d_c = jnp.exp(ch * lg)[:, None, None]
    cos, sin = _rope_tables(jnp.arange(t))
    half = RET_DK // 2
    nh = RET_HEADS
    rows = lambda w: pl.BlockSpec((ch, w), lambda b, c: (b * n_ch + c, 0))
    whole = lambda a: pl.BlockSpec(a.shape, lambda b, c: (0,) * a.ndim)
    consts = (dmask, d_in, d_rev, d_c, gn_g.reshape(1, -1), gn_b.reshape(1, -1))
    return pl.pallas_call(
        _ret_prompt_body,
        grid=(bsz, n_ch),
        in_specs=[rows(nh * RET_DK), rows(nh * RET_DK), rows(nh * RET_DV), rows(nh * RET_DV),
                  pl.BlockSpec((ch, half), lambda b, c: (c, 0)), pl.BlockSpec((ch, half), lambda b, c: (c, 0))]
        + [whole(a) for a in consts],
        out_specs=[rows(nh * RET_DV), pl.BlockSpec((1, nh, RET_DK, RET_DV), lambda b, c: (b, 0, 0, 0))],
        out_shape=[jax.ShapeDtypeStruct((bsz * t, nh * RET_DV), BF16),
                   jax.ShapeDtypeStruct((bsz, nh, RET_DK, RET_DV), F32)],
        scratch_shapes=[pltpu.VMEM((nh, RET_DK, RET_DV), F32)],
        compiler_params=_params("arbitrary", "arbitrary"),
        name="retention_prompt",
    )(q, k, v, gate, cos, sin, *consts)


_RET_SPLITS = (RET_HEADS * RET_DK, RET_HEADS * RET_DK, RET_HEADS * RET_DV, RET_HEADS * RET_DV)
_RET_DTYPES = (F32, F32, BF16, F32)


def _ret_sample_body(q_ref, k_ref, v_ref, gt_ref, cos_ref, sin_ref, dec_ref, gg_ref, gb_ref, s_in_ref,
                     o_ref, s_out_ref):
    cos, sin = cos_ref[...], sin_ref[...]
    row0 = lax.broadcasted_iota(I32, (8, 1), 0) == 0
    for h in range(RET_HEADS):
        qh = _rope(q_ref[0, :, h * RET_DK:(h + 1) * RET_DK], cos, sin)
        kh = _rope(k_ref[0, :, h * RET_DK:(h + 1) * RET_DK], cos, sin) * RET_DK ** -0.5
        vh = v_ref[0, :, h * RET_DV:(h + 1) * RET_DV]
        qb, kb = qh.astype(BF16), kh.astype(BF16)
        dec = dec_ref[h]
        s_old = s_in_ref[0, h]
        q8 = jnp.broadcast_to(qb, (8, RET_DK))
        cross = jnp.dot(q8, s_old.astype(BF16), preferred_element_type=F32)[0:1] * dec
        inner = jnp.sum(qb.astype(F32) * kb.astype(F32), axis=1, keepdims=True)
        o = inner.astype(BF16).astype(F32) * vh.astype(F32) + cross
        k8 = jnp.where(row0, jnp.broadcast_to(kb.astype(F32), (8, RET_DK)), 0.0).astype(BF16)
        v8 = jnp.broadcast_to(vh, (8, RET_DV))
        s_out_ref[0, h] = dec * s_old + lax.dot_general(k8, v8, _TN, preferred_element_type=F32)
        sl = slice(h * RET_DV, (h + 1) * RET_DV)
        o_ref[0, :, sl] = _group_norm_gate(o, gt_ref[0, :, sl], gg_ref[:, sl], gb_ref[:, sl]).astype(o_ref.dtype)


def retention_sample(q, k, v, gate, gn_g, gn_b, state, pos0):
    n = q.shape[0]
    cos, sin = _rope_tables(jnp.full((1,), pos0))
    dec = jnp.exp(_ret_log_decay())
    r3 = lambda a: a.reshape(n, 1, a.shape[-1])
    w_qk, w_v = RET_HEADS * RET_DK, RET_HEADS * RET_DV
    half = RET_DK // 2
    vec = lambda w: pl.BlockSpec((1, 1, w), lambda b: (b, 0, 0))
    og, s_new = pl.pallas_call(
        _ret_sample_body,
        grid=(n,),
        in_specs=[vec(w_qk), vec(w_qk), vec(w_v), vec(w_v),
                  pl.BlockSpec((1, half), lambda b: (0, 0)), pl.BlockSpec((1, half), lambda b: (0, 0)),
                  pl.BlockSpec(memory_space=pltpu.SMEM),
                  pl.BlockSpec((1, w_v), lambda b: (0, 0)), pl.BlockSpec((1, w_v), lambda b: (0, 0)),
                  pl.BlockSpec((1, RET_HEADS, RET_DK, RET_DV), lambda b: (b, 0, 0, 0))],
        out_specs=[vec(w_v), pl.BlockSpec((1, RET_HEADS, RET_DK, RET_DV), lambda b: (b, 0, 0, 0))],
        out_shape=[jax.ShapeDtypeStruct((n, 1, w_v), BF16), jax.ShapeDtypeStruct(state.shape, F32)],
        compiler_params=_params("arbitrary"),
        name="retention_sample",
    )(r3(q), r3(k), r3(v), r3(gate), cos, sin, dec, gn_g.reshape(1, -1), gn_b.reshape(1, -1), state)
    return og.reshape(n, w_v), s_new


POOL_HALO = 16


def _pool_mix(win_sum_fn, x, pos, w_ref, sc_ref, g_ref, b_ref):
    ys = []
    for gi, w in enumerate(POOL_WINDOWS):
        sl = slice(gi * POOL_GROUP, (gi + 1) * POOL_GROUP)
        pooled = win_sum_fn(gi, w) / jnp.minimum(float(w), pos + 1.0) - x[:, sl]
        ys.append(jnp.dot(pooled.astype(BF16), w_ref[gi], preferred_element_type=F32))
    y = jnp.concatenate(ys, axis=1) * sc_ref[...]
    return _layer_norm(ALPHA * x + y, g_ref[...], b_ref[...])


def _pool_prompt_body(x_ref, w_ref, sc_ref, g_ref, b_ref, o_ref, xh_scr, *, tt):
    j = pl.program_id(1)

    @pl.when(j == 0)
    def _():
        xh_scr[0:POOL_HALO, :] = jnp.zeros((POOL_HALO, D_MODEL), F32)

    x = x_ref[0]
    xh_scr[POOL_HALO:, :] = x
    pos = (j * tt + lax.broadcasted_iota(I32, (tt, 1), 0)).astype(F32)

    def win_sum(gi, w):
        c0 = gi * POOL_GROUP
        acc = x[:, c0:c0 + POOL_GROUP]
        for u in range(1, w):
            acc = acc + xh_scr[POOL_HALO - u:POOL_HALO - u + tt, c0:c0 + POOL_GROUP]
        return acc

    o_ref[0] = _pool_mix(win_sum, x, pos, w_ref, sc_ref, g_ref, b_ref)
    xh_scr[0:POOL_HALO, :] = x[tt - POOL_HALO:, :]


def pool_prompt(x, w_pool, scale, g, b, tt=256):
    bsz, t, d = x.shape
    tt = min(tt, t)
    row = lambda b_, j: (0, 0)
    return pl.pallas_call(
        functools.partial(_pool_prompt_body, tt=tt),
        grid=(bsz, t // tt),
        in_specs=[pl.BlockSpec((1, tt, d), lambda b_, j: (b_, j, 0)),
                  pl.BlockSpec(w_pool.shape, lambda b_, j: (0, 0, 0)),
                  pl.BlockSpec((1, d), row), pl.BlockSpec((1, d), row), pl.BlockSpec((1, d), row)],
        out_specs=pl.BlockSpec((1, tt, d), lambda b_, j: (b_, j, 0)),
        out_shape=jax.ShapeDtypeStruct(x.shape, F32),
        scratch_shapes=[pltpu.VMEM((POOL_HALO + tt, d), F32)],
        compiler_params=_params("arbitrary", "arbitrary"),
        name="pool_prompt",
    )(x, w_pool, scale.reshape(1, d), g.reshape(1, d), b.reshape(1, d))


def _pool_sample_body(x_ref, st_ref, w_ref, sc_ref, g_ref, b_ref, o_ref, *, pos0):
    x = x_ref[...]
    n_hist = st_ref.shape[0]

    def win_sum(gi, w):
        c0 = gi * POOL_GROUP
        acc = x[:, c0:c0 + POOL_GROUP]
        for u in range(1, w):
            acc = acc + st_ref[n_hist - u, :, c0:c0 + POOL_GROUP]
        return acc

    pos = jnp.full((x.shape[0], 1), float(pos0), F32)
    o_ref[...] = _pool_mix(win_sum, x, pos, w_ref, sc_ref, g_ref, b_ref)


def pool_sample(x, hist, w_pool, scale, g, b, pos0):
    n, d = x.shape
    whole = lambda a: pl.BlockSpec(a.shape, lambda i: (0,) * a.ndim)
    args = (x, hist, w_pool, scale.reshape(1, d), g.reshape(1, d), b.reshape(1, d))
    return pl.pallas_call(
        functools.partial(_pool_sample_body, pos0=pos0),
        grid=(1,),
        in_specs=[whole(a) for a in args],
        out_specs=pl.BlockSpec((n, d), lambda i: (0, 0)),
        out_shape=jax.ShapeDtypeStruct((n, d), F32),
        compiler_params=_params("arbitrary"),
        name="pool_sample",
    )(*args)


CONV_HALO = 8


def _conv_prompt_body(bg_ref, cg_ref, h_ref, wc_ref, a_ref, tail_ref, uh_scr, *, tt):
    j = pl.program_id(1)

    @pl.when(j == 0)
    def _():
        uh_scr[0:CONV_HALO, :] = jnp.zeros((CONV_HALO, D_MODEL), F32)

    u = cg_ref[...] * h_ref[...]
    uh_scr[CONV_HALO:, :] = u
    conv = wc_ref[CONV_W - 1:CONV_W, :] * u
    for jj in range(CONV_W - 1):
        back = CONV_W - 1 - jj
        conv = conv + wc_ref[jj:jj + 1, :] * uh_scr[CONV_HALO - back:CONV_HALO - back + tt, :]
    a_ref[...] = (bg_ref[...] * conv).astype(a_ref.dtype)
    uh_scr[0:CONV_HALO, :] = u[tt - CONV_HALO:, :]
    tail_ref[0] = u[tt - CONV_HALO:, :]


def conv_prompt(bg, cg, h, w_conv, bsz, t, tt=256):
    d = bg.shape[1]
    tt = min(tt, t)
    nt = t // tt
    blk = pl.BlockSpec((tt, d), lambda b_, j: (b_ * nt + j, 0))
    return pl.pallas_call(
        functools.partial(_conv_prompt_body, tt=tt),
        grid=(bsz, nt),
        in_specs=[blk, blk, blk, pl.BlockSpec((CONV_W, d), lambda b_, j: (0, 0))],
        out_specs=[blk, pl.BlockSpec((1, CONV_HALO, d), lambda b_, j: (b_, 0, 0))],
        out_shape=[jax.ShapeDtypeStruct((bsz * t, d), BF16), jax.ShapeDtypeStruct((bsz, CONV_HALO, d), F32)],
        scratch_shapes=[pltpu.VMEM((CONV_HALO + tt, d), F32)],
        compiler_params=_params("arbitrary", "arbitrary"),
        name="conv_prompt",
    )(bg, cg, h, w_conv)


def _conv_sample_body(bg_ref, cg_ref, h_ref, prev_ref, wc_ref, a_ref, u_ref):
    u = cg_ref[...] * h_ref[...]
    conv = wc_ref[CONV_W - 1:CONV_W, :] * u
    for jj in range(CONV_W - 1):
        conv = conv + wc_ref[jj:jj + 1, :] * prev_ref[jj]
    a_ref[...] = (bg_ref[...] * conv).astype(a_ref.dtype)
    u_ref[...] = u


def conv_sample(bg, cg, h, prev, w_conv):
    n, d = bg.shape
    whole = lambda a: pl.BlockSpec(a.shape, lambda i: (0,) * a.ndim)
    args = (bg, cg, h, prev, w_conv)
    return pl.pallas_call(
        _conv_sample_body,
        grid=(1,),
        in_specs=[whole(a) for a in args],
        out_specs=[pl.BlockSpec((n, d), lambda i: (0, 0)), pl.BlockSpec((n, d), lambda i: (0, 0))],
        out_shape=[jax.ShapeDtypeStruct((n, d), BF16), jax.ShapeDtypeStruct((n, d), F32)],
        compiler_params=_params("arbitrary"),
        name="conv_sample",
    )(*args)


_CONV_SPLITS = (D_MODEL, D_MODEL, D_MODEL)
_CONV_DTYPES = (F32, F32, F32)


ROUTER_LANES = LANES
EXPERT_LANE0 = MOE_GROUPS


COMB_LANE0 = 8


def _router_body(x_ref, w_ref, b_ref, o_ref, cnt_ref, run_scr):
    i = pl.program_id(0)

    @pl.when(i == 0)
    def _():
        run_scr[...] = jnp.zeros_like(run_scr)

    x = x_ref[...]
    x_hi = x.astype(BF16)
    x_lo = (x - x_hi.astype(F32)).astype(BF16)
    logits = (jnp.dot(x_hi, w_ref[0], preferred_element_type=F32) + jnp.dot(x_lo, w_ref[0], preferred_element_type=F32)
              + jnp.dot(x_hi, w_ref[1], preferred_element_type=F32) + b_ref[...])
    tm = logits.shape[0]
    lane = lax.broadcasted_iota(I32, logits.shape, 1)
    big = ROUTER_LANES

    def top1(mask):
        v = jnp.max(jnp.where(mask, logits, -jnp.inf), axis=1, keepdims=True)
        idx = jnp.min(jnp.where(mask & (logits == v), lane, big), axis=1, keepdims=True)
        return v, idx

    is_g = lane < MOE_GROUPS
    vg, gsel = top1(is_g)
    pg_sel = 1.0 / jnp.sum(jnp.where(is_g, jnp.exp(logits - vg), 0.0), axis=1, keepdims=True)
    e0 = EXPERT_LANE0 + gsel * MOE_EPG
    is_e = (lane >= e0) & (lane < e0 + MOE_EPG)
    v1, i1 = top1(is_e)
    v2, i2 = top1(is_e & (lane != i1))
    r = jnp.exp(v2 - v1)
    pe1 = pg_sel / (1.0 + r)
    pe2 = pg_sel * r / (1.0 + r)
    local = lane - COMB_LANE0
    comb = jnp.where(local == i1 - e0, pe1, 0.0) + jnp.where(local == i2 - e0, pe2, 0.0)

    onehot = jnp.where(lane == gsel, 1.0, 0.0)
    rows = lax.broadcasted_iota(I32, (tm, tm), 0)
    cols = lax.broadcasted_iota(I32, (tm, tm), 1)
    before = jnp.where(cols < rows, 1.0, 0.0).astype(BF16)
    prefix = jnp.dot(before, onehot.astype(BF16), preferred_element_type=F32)
    run = run_scr[0:1, :]
    rank = jnp.sum(jnp.where(lane == gsel, prefix + run, 0.0), axis=1, keepdims=True)
    run = run + jnp.sum(onehot, axis=0, keepdims=True)
    run_scr[0:1, :] = run
    cnt_ref[...] = jnp.broadcast_to(run, cnt_ref.shape)
    o_ref[...] = jnp.where(lane == 0, gsel.astype(F32), jnp.where(lane == 1, rank, comb))


def moe_router(x, w_r, b_r, tm):
    m, d = x.shape
    return pl.pallas_call(
        _router_body,
        grid=(m // tm,),
        in_specs=[pl.BlockSpec((tm, d), lambda i: (i, 0)), pl.BlockSpec((2, d, ROUTER_LANES), lambda i: (0, 0, 0)),
                  pl.BlockSpec((1, ROUTER_LANES), lambda i: (0, 0))],
        out_specs=[pl.BlockSpec((tm, ROUTER_LANES), lambda i: (i, 0)),
                   pl.BlockSpec((8, ROUTER_LANES), lambda i: (0, 0))],
        out_shape=[jax.ShapeDtypeStruct((m, ROUTER_LANES), F32), jax.ShapeDtypeStruct((8, ROUTER_LANES), F32)],
        scratch_shapes=[pltpu.VMEM((8, ROUTER_LANES), F32)],
        compiler_params=_params("arbitrary"),
        name="moe_router",
    )(x, w_r, b_r)


def _router_weights(w_rg, b_rg, w_re, b_re):
    d = w_rg.shape[0]
    pad = ROUTER_LANES - MOE_GROUPS - MOE_EXPERTS
    w = jnp.concatenate([w_rg, w_re, jnp.zeros((d, pad), F32)], axis=1)
    b = jnp.concatenate([b_rg, b_re, jnp.zeros((pad,), F32)]).reshape(1, ROUTER_LANES)
    w_hi = w.astype(BF16)
    w_lo = (w - w_hi.astype(F32)).astype(BF16)
    return jnp.stack([w_hi, w_lo]), b


def _moe_ffn_body(gid_ref, src_ref, x_hbm, c_ref, w1_ref, w3_ref, w2_ref, g_ref, b_ref, o_ref,
                  w1b, w3b, w2b, xbuf, sem_in, *, tm, n_tiles):
    i = pl.program_id(0)
    slot = i % 2
    other = 1 - slot

    def fetch_rows(tile, buf_slot):
        for r in range(tm):
            row = src_ref[tile * tm + r]
            pltpu.make_async_copy(x_hbm.at[pl.ds(row, 1)], xbuf.at[buf_slot, pl.ds(r, 1)],
                                  sem_in.at[buf_slot]).start(priority=r % 2)

    def wait_fetch(buf_slot):
        pltpu.make_async_copy(x_hbm.at[pl.ds(0, tm)], xbuf.at[buf_slot], sem_in.at[buf_slot]).wait()

    @pl.when(i == 0)
    def _():
        fetch_rows(0, 0)

    fetch_rows(jnp.minimum(i + 1, n_tiles - 1), other)
    wait_fetch(slot)

    @pl.when((i == 0) | (gid_ref[i] != gid_ref[jnp.maximum(i - 1, 0)]))
    def _():
        for e in range(MOE_EPG):
            w1b[e] = w1_ref[0, e].astype(BF16)
            w3b[e] = w3_ref[0, e].astype(BF16)
            w2b[e] = w2_ref[0, e].astype(BF16)

    x = xbuf[slot]
    xb = x.astype(BF16)
    comb = c_ref[...]
    y = jnp.zeros(x.shape, F32)
    for e in range(MOE_EPG):
        a = jnp.dot(xb, w1b[e], preferred_element_type=F32)
        gte = jnp.dot(xb, w3b[e], preferred_element_type=F32)
        hcol = a * (1.0 / (1.0 + jnp.exp(-a))) * gte * comb[:, e:e + 1]
        y = y + jnp.dot(hcol.astype(BF16), w2b[e], preferred_element_type=F32)
    o_ref[...] = _layer_norm(ALPHA * x + y, g_ref[...], b_ref[...])

    @pl.when(i == n_tiles - 1)
    def _():
        wait_fetch(other)


def moe_layer(x, router_w, router_b, w1, w3, w2, layer, g, b, tm, router_tm):
    m, d = x.shape
    r, cnt = moe_router(x, router_w, router_b, tm=router_tm)
    gid, rank = r[:, 0].astype(I32), r[:, 1].astype(I32)
    comb = r[:, COMB_LANE0:COMB_LANE0 + MOE_EPG]
    counts = cnt[0, :MOE_GROUPS].astype(I32)
    tiles = (counts + tm - 1) // tm
    tile_end = jnp.cumsum(tiles)
    start = (tile_end - tiles) * tm
    dest = rank
    for grp in range(MOE_GROUPS):
        dest = dest + jnp.where(gid == grp, start[grp], 0)
    n_tiles = -(-m // tm) + MOE_GROUPS
    mp = n_tiles * tm
    src = jnp.full((mp,), -1, I32).at[dest].set(jnp.arange(m, dtype=I32))
    live = (src >= 0).astype(F32)
    src = jnp.maximum(src, 0)
    cs = jnp.take(comb, src, axis=0, mode="clip") * live[:, None]
    tile_gid = jnp.minimum((jnp.arange(n_tiles, dtype=I32)[:, None] >= tile_end[None, :]).sum(axis=1),
                           MOE_GROUPS - 1).astype(I32)
    wspec = lambda shp: pl.BlockSpec((1,) + shp, lambda i, gid_, src_: (layer, gid_[i], 0, 0),
                                     pipeline_mode=pl.Buffered(1))
    const = lambda i, gid_, src_: (0, 0)
    out = pl.pallas_call(
        functools.partial(_moe_ffn_body, tm=tm, n_tiles=n_tiles),
        grid_spec=pltpu.PrefetchScalarGridSpec(
            num_scalar_prefetch=2,
            grid=(n_tiles,),
            in_specs=[pl.BlockSpec(memory_space=pl.ANY),
                      pl.BlockSpec((tm, MOE_EPG), lambda i, gid_, src_: (i, 0)),
                      wspec((MOE_EPG, d, MOE_FF)), wspec((MOE_EPG, d, MOE_FF)), wspec((MOE_EPG, MOE_FF, d)),
                      pl.BlockSpec((1, d), const), pl.BlockSpec((1, d), const)],
            out_specs=pl.BlockSpec((tm, d), lambda i, gid_, src_: (i, 0)),
            scratch_shapes=[pltpu.VMEM((MOE_EPG, d, MOE_FF), BF16), pltpu.VMEM((MOE_EPG, d, MOE_FF), BF16),
                            pltpu.VMEM((MOE_EPG, MOE_FF, d), BF16),
                            pltpu.VMEM((2, tm, d), F32), pltpu.SemaphoreType.DMA((2,))]),
        out_shape=jax.ShapeDtypeStruct((mp, d), F32),
        compiler_params=_params("arbitrary"),
        name="moe_ffn",
    )(tile_gid, src, x, cs, w1, w3, w2, g.reshape(1, d), b.reshape(1, d))
    return out, dest


NSA_ROWS = NSA_HEADS
SEL_SAMPLES = 8


def _softmax_rows(s, ok):
    s = jnp.where(ok, s, NEG)
    m = jnp.max(s, axis=1, keepdims=True)
    p = jnp.where(ok, jnp.exp(s - m), 0.0)
    l = jnp.sum(p, axis=1, keepdims=True)
    return p / jnp.where(l > 0.0, l, 1.0)


def _nsa_sample_select_body(qx_ref, kvc_ref, pool_ref, idx_ref, *, n_cmp, n_sel, q_pos):
    n_rows = kvc_ref.shape[1]
    lane = lax.broadcasted_iota(I32, (NSA_ROWS, n_rows), 1)
    ok_c = (lane * CMP_STRIDE + (CMP_LEN - 1) <= q_pos) & (lane < n_cmp)
    for s_i in range(SEL_SAMPLES):
        kc = kvc_ref[s_i, :, 0:NSA_KV_W].astype(BF16)
        s = lax.dot_general(qx_ref[s_i], kc, _NT, preferred_element_type=F32)
        p = _softmax_rows(s, ok_c)
        imp = p
        for g in range(1, NSA_GROUP):
            imp = imp + pltpu.roll(p, g * NSA_KV_HEADS, 0)
        blk = jnp.dot(imp, pool_ref[...], preferred_element_type=F32, precision=lax.Precision.HIGHEST)
        score = jnp.where(lane == q_pos // SEL_BLOCK, NSA_GROUP + 2.0,
                          jnp.where(lane == 0, NSA_GROUP + 1.0,
                                    jnp.where(lane * SEL_BLOCK <= q_pos, blk, -1.0)))
        score = jnp.where(lane < n_sel, score, -3.0)
        idx = jnp.full(score.shape, -1, I32)
        for k in range(min(TOP_N, n_sel)):
            mx = jnp.max(score, axis=1, keepdims=True)
            first = jnp.min(jnp.where(score == mx, lane, n_rows), axis=1, keepdims=True)
            idx = jnp.where(lane == k, jnp.where(mx >= 0.0, first, -1), idx)
            score = jnp.where(lane == first, -2.0, score)
        idx_ref[s_i] = idx[0:8, :]


def nsa_sample_select(qx, kvc, n_cmp, n_sel, q_pos):
    n, n_rows, w = kvc.shape
    r_sel = SEL_BLOCK // CMP_STRIDE
    pool = (jnp.arange(n_rows)[:, None] // r_sel == jnp.arange(n_rows)[None, :]).astype(F32)
    return pl.pallas_call(
        functools.partial(_nsa_sample_select_body, n_cmp=n_cmp, n_sel=n_sel, q_pos=q_pos),
        grid=(n // SEL_SAMPLES,),
        in_specs=[pl.BlockSpec((SEL_SAMPLES, NSA_ROWS, NSA_KV_W), lambda i: (i, 0, 0)),
                  pl.BlockSpec((SEL_SAMPLES, n_rows, w), lambda i: (i, 0, 0)),
                  pl.BlockSpec((n_rows, n_rows), lambda i: (0, 0))],
        out_specs=pl.BlockSpec((SEL_SAMPLES, 8, n_rows), lambda i: (i, 0, 0)),
        out_shape=jax.ShapeDtypeStruct((n, 8, n_rows), I32),
        compiler_params=_params("arbitrary"),
        name="nsa_sample_select",
    )(qx, kvc, pool)


Q_ROWS = 8


def _nsa_sample_attend_body(page_ref, half_ref, qh_ref, kvc_ref, win_ref, ksn_ref, kwn_ref, g_ref, *rest, n_cmp, q_pos):
    blk_refs, o_ref = rest[:-1], rest[-1]
    b = pl.program_id(0)
    dh = NSA_HEAD_DIM
    rnd = lambda a: a.astype(BF16).astype(F32)
    n_top = len(blk_refs) // NSA_KV_HEADS
    page = blk_refs[0].shape[-1]
    n_rows = kvc_ref.shape[1]
    lane_c = lax.broadcasted_iota(I32, (Q_ROWS, n_rows), 1)
    ok_c = (lane_c * CMP_STRIDE + (CMP_LEN - 1) <= q_pos) & (lane_c < n_cmp)
    lane_s = lax.broadcasted_iota(I32, (Q_ROWS, n_top * page), 1)
    gate = 1.0 / (1.0 + jnp.exp(-g_ref[0]))

    for h in range(NSA_KV_HEADS):
        q = qh_ref[0, h]
        qf = q.astype(F32)

        def with_new_key(k_t, v_t, ok, new_ref):
            s = jnp.dot(q, k_t, preferred_element_type=F32)
            s_new = jnp.sum(qf * rnd(new_ref[0, :, h * dh:(h + 1) * dh]), axis=1, keepdims=True)
            if ok is not None:
                s = jnp.where(ok, s, NEG)
            m = jnp.maximum(jnp.max(s, axis=1, keepdims=True), s_new)
            p = jnp.exp(s - m) if ok is None else jnp.where(ok, jnp.exp(s - m), 0.0)
            p_new = jnp.exp(s_new - m)
            l = jnp.sum(p, axis=1, keepdims=True) + p_new
            v_new = new_ref[0, :, NSA_KV_W + h * dh:NSA_KV_W + (h + 1) * dh]
            o = lax.dot_general(p.astype(BF16), v_t, _NT, preferred_element_type=F32)
            return (o + rnd(p_new) * rnd(v_new)) / l

        kc = kvc_ref[0, :, h * dh:(h + 1) * dh].astype(BF16)
        vc = kvc_ref[0, :, NSA_KV_W + h * dh:NSA_KV_W + (h + 1) * dh].astype(BF16)
        s = lax.dot_general(q, kc, _NT, preferred_element_type=F32)
        o_c = jnp.dot(_softmax_rows(s, ok_c).astype(BF16), vc, preferred_element_type=F32)

        o_w = with_new_key(win_ref[0, 0, h].astype(BF16), win_ref[0, 1, h].astype(BF16), None, kwn_ref)

        refs = blk_refs[h * n_top:(h + 1) * n_top]
        k_t = jnp.concatenate([r[0, 0, 0] for r in refs], axis=1).astype(BF16)
        v_t = jnp.concatenate([r[0, 1, 0] for r in refs], axis=1).astype(BF16)
        want = jnp.full(lane_s.shape, -1, I32)
        for k in range(n_top):
            want = jnp.where(lane_s // page == k, half_ref[(b * NSA_KV_HEADS + h) * n_top + k], want)
        o_s = with_new_key(k_t, v_t, (lane_s % page) // SEL_BLOCK == want, ksn_ref)

        o = gate[h, :, 0:1] * o_c + gate[h, :, 1:2] * o_s + gate[h, :, 2:3] * o_w
        o_ref[0, h] = o.astype(o_ref.dtype)


def nsa_sample_mixer(x, cache_cmp_t, cache_sel_t, win_t, page_table, w_in_p, r_cmp, b_cmp):
    n, d = x.shape
    n_pages = page_table.shape[1]
    past = n_pages * PAGE_SIZE
    dh, kvh, grp = NSA_HEAD_DIM, NSA_KV_HEADS, NSA_GROUP
    q, kv_c, kv_s, kv_w, gates = matmul_split(x, w_in_p, _NSA_SPLITS, _NSA_DTYPES, tm=n)
    n_cmp = (past + 1 - CMP_LEN) // CMP_STRIDE + 1
    n_sel = -(-(past + 1) // SEL_BLOCK)
    n_past_blk = past // SEL_BLOCK
    kvc = nsa_compress_paged(cache_cmp_t, page_table, r_cmp, b_cmp)
    q4 = q.reshape(n, kvh, grp, dh)
    q4t = q4.transpose(0, 2, 1, 3)
    qx = (q4t[:, :, :, None, :] * jnp.eye(kvh, dtype=q.dtype)[None, None, :, :, None]).reshape(n, NSA_ROWS, kvh * dh)
    idx = nsa_sample_select(qx, kvc, n_cmp, n_sel, past)[:, :kvh, :TOP_N]
    sub = PAGE_SIZE // SEL_BLOCK
    is_past = (idx >= 0) & (idx < n_past_blk)
    phys = jnp.take_along_axis(page_table, jnp.clip(idx // sub, 0, n_pages - 1).reshape(n, -1), axis=1).reshape(idx.shape)
    page_idx = jnp.where(is_past, phys, 0).astype(I32).reshape(-1)
    half = jnp.where(is_past, idx % sub, -1).astype(I32).reshape(-1)
    pad_rows = lambda a: jnp.pad(a, ((0, 0), (0, 0), (0, Q_ROWS - grp), (0, 0)))
    qh = pad_rows(q4)
    gt = pad_rows(gates[:, :NSA_HEADS * 3].reshape(n, kvh, grp, 3))
    n_top = idx.shape[2]
    per_seq = kvh * n_top
    vec = lambda a: a.reshape(n, 1, a.shape[-1])
    blk_spec = lambda k: pl.BlockSpec((1, 2, 1, dh, PAGE_SIZE),
                                      lambda b, p_, h_: (p_[b * per_seq + k], 0, k // n_top, 0, 0))
    seq = lambda a: pl.BlockSpec((1,) + a.shape[1:], lambda b, p_, h_: (b,) + (0,) * (a.ndim - 1))
    ins = (qh, kvc, win_t, vec(kv_s), vec(kv_w), gt)
    out = pl.pallas_call(
        functools.partial(_nsa_sample_attend_body, n_cmp=n_cmp, q_pos=past),
        grid_spec=pltpu.PrefetchScalarGridSpec(
            num_scalar_prefetch=2,
            grid=(n,),
            in_specs=[seq(a) for a in ins] + [blk_spec(k) for k in range(per_seq)],
            out_specs=pl.BlockSpec((1, kvh, Q_ROWS, dh), lambda b, p_, h_: (b, 0, 0, 0))),
        out_shape=jax.ShapeDtypeStruct((n, kvh, Q_ROWS, dh), BF16),
        compiler_params=_params("arbitrary"),
        name="nsa_sample_attend",
    )(page_idx, half, *ins, *([cache_sel_t] * per_seq))
    return out[:, :, :grp].reshape(n, NSA_Q_W), kv_c, kv_s, kv_w


PROMPT_TM = 512
ROUTER_TM = 384


def kernel(x_prompt, x_sample, cache_nsa_cmp, cache_nsa_sel, state_nsa_win, state_ret, state_pool, state_conv, page_table, nsa_w_in, nsa_w_cmp, nsa_b_cmp, nsa_w_o, ret_w_in, ret_gn_g, ret_gn_b, ret_w_o, pool_w, pool_scale, conv_w_in, conv_w, conv_w_out, ln_g, ln_b, moe_w_rg, moe_b_rg, moe_w_re, moe_b_re, moe_w1, moe_w3, moe_w2):
    bp, t, d = x_prompt.shape
    ns = x_sample.shape[0]
    assert x_sample.shape[1] == 1 and (bp * t) % PROMPT_TM == 0 and (bp * t + ns) % ROUTER_TM == 0
    past = page_table.shape[1] * PAGE_SIZE
    xp, xs = x_prompt.reshape(bp * t, d), x_sample.reshape(ns, d)
    kv5 = lambda a, n, rows: a.reshape(n, rows, 2, NSA_KV_HEADS, NSA_HEAD_DIM)
    shift_in = lambda old, new: jnp.concatenate([old[:, 1:], new[:, None]], axis=1)
    cmp_p, sel_p, win_p, ret_p, pool_p, conv_p = [], [], [], [], [], []
    cmp_s, sel_s, win_s, ret_s, pool_s, conv_s = [], [], [], [], [], []
    n_mix = 4
    for i in range(DEPTH):
        kind, j = i % n_mix, i // n_mix
        g0, b0 = ln_g[i, 0], ln_b[i, 0]
        if kind == 0:
            w_in_p, r_cmp, w_o = _nsa_w_in_padded(nsa_w_in[j]), _cmp_weights(nsa_w_cmp[j]), nsa_w_o[j].astype(BF16)
            o, a, b, c = nsa_prompt_mixer(xp.reshape(bp, t, d), w_in_p, r_cmp, nsa_b_cmp[j])
            cmp_p.append(kv5(a, bp, t)); sel_p.append(kv5(b, bp, t)); win_p.append(kv5(c, bp, t)[:, -min(WINDOW, t):])
            xp = matmul_res_ln(o, w_o, xp, g0, b0, PROMPT_TM)
            win = state_nsa_win[j]
            rows_minor = lambda a: jnp.transpose(a, (0, 2, 3, 4, 1))
            o, a, b, c = nsa_sample_mixer(xs, rows_minor(cache_nsa_cmp[j]), rows_minor(cache_nsa_sel[j]), rows_minor(win),
                                          page_table, w_in_p, r_cmp, nsa_b_cmp[j])
            cmp_s.append(kv5(a, ns, 1)); sel_s.append(kv5(b, ns, 1)); win_s.append(shift_in(win, kv5(c, ns, 1)[:, 0]))
            xs = matmul_res_ln(o, w_o, xs, g0, b0, ns)
        elif kind == 1:
            w_in, w_o = ret_w_in[j].astype(BF16), ret_w_o[j].astype(BF16)
            q, k, v, g = matmul_split(xp, w_in, _RET_SPLITS, _RET_DTYPES, tm=256)
            og, s_fin = retention_prompt(q, k, v, g, ret_gn_g[j], ret_gn_b[j], bp, t)
            ret_p.append(s_fin)
            xp = matmul_res_ln(og, w_o, xp, g0, b0, PROMPT_TM)
            q, k, v, g = matmul_split(xs, w_in, _RET_SPLITS, _RET_DTYPES, tm=ns)
            og, s_new = retention_sample(q, k, v, g, ret_gn_g[j], ret_gn_b[j], state_ret[j], past)
            ret_s.append(s_new)
            xs = matmul_res_ln(og, w_o, xs, g0, b0, ns)
        elif kind == 2:
            wp = pool_w[j].astype(BF16)
            pool_p.append(xp.reshape(bp, t, d)[:, -(POOL_MAX - 1):])
            xp = pool_prompt(xp.reshape(bp, t, d), wp, pool_scale[j], g0, b0).reshape(bp * t, d)
            pool_s.append(shift_in(state_pool[j], xs))
            xs = pool_sample(xs, state_pool[j].transpose(1, 0, 2), wp, pool_scale[j], g0, b0, past)
        else:
            w_in, w_out = conv_w_in[j].astype(BF16), conv_w_out[j].astype(BF16)
            bg, cg, h = matmul_split(xp, w_in, _CONV_SPLITS, _CONV_DTYPES, tm=PROMPT_TM)
            a, tail = conv_prompt(bg, cg, h, conv_w[j], bp, t)
            conv_p.append(tail[:, -(CONV_W - 1):])
            xp = matmul_res_ln(a, w_out, xp, g0, b0, PROMPT_TM)
            bg, cg, h = matmul_split(xs, w_in, _CONV_SPLITS, _CONV_DTYPES, tm=ns)
            a, u = conv_sample(bg, cg, h, state_conv[j].transpose(1, 0, 2), conv_w[j])
            conv_s.append(shift_in(state_conv[j], u))
            xs = matmul_res_ln(a, w_out, xs, g0, b0, ns)
        rw, rb = _router_weights(moe_w_rg[i], moe_b_rg[i], moe_w_re[i], moe_b_re[i])
        out, dest = moe_layer(jnp.concatenate([xp, xs], axis=0), rw, rb, moe_w1, moe_w3, moe_w2, i,
                              ln_g[i, 1], ln_b[i, 1], tm=PROMPT_TM, router_tm=ROUTER_TM)
        xp = jnp.take(out, dest[:bp * t], axis=0, mode="clip")
        xs = jnp.take(out, dest[bp * t:], axis=0, mode="clip")
    st = jnp.stack
    return (xp.reshape(bp, t, d), xs.reshape(ns, 1, d), st(cmp_p), st(sel_p), st(win_p), st(ret_p), st(pool_p), st(conv_p),
            st(cmp_s), st(sel_s), st(win_s), st(ret_s), st(pool_s), st(conv_s))
```

```python
import functools

import jax
import jax.numpy as jnp
import numpy as np
from jax import lax
from jax.experimental import pallas as pl
from jax.experimental.pallas import tpu as pltpu

F32, BF16, I32 = jnp.float32, jnp.bfloat16, jnp.int32

D_MODEL = 1024
DEPTH = 4
PAGE_SIZE = 128
NSA_HEADS = 16
NSA_HEAD_DIM = 64
NSA_KV_HEADS = 4
NSA_GROUP = NSA_HEADS // NSA_KV_HEADS
NSA_KV_W = NSA_KV_HEADS * NSA_HEAD_DIM
NSA_Q_W = NSA_HEADS * NSA_HEAD_DIM
CMP_LEN = 32
CMP_STRIDE = 16
SEL_BLOCK = 64
TOP_N = 8
WINDOW = 512
RET_HEADS = 4
RET_DK = D_MODEL // RET_HEADS
RET_DV = 2 * D_MODEL // RET_HEADS
RET_CHUNK = 128
ROPE_BASE = 10000.0
POOL_WINDOWS = (2, 4, 8, 16)
POOL_GROUP = D_MODEL // len(POOL_WINDOWS)
POOL_MAX = max(POOL_WINDOWS)
CONV_W = 3
MOE_GROUPS = 4
MOE_EPG = 8
MOE_EXPERTS = MOE_GROUPS * MOE_EPG
MOE_FF = D_MODEL // 4
ALPHA = (2.0 * DEPTH) ** 0.25
LN_EPS = 1e-5
NEG = -1e30

LANES = 128
Q_TILE = 128
KV_TILE = 256
Q_TILES_PER_CALL = 4
VMEM_LIMIT = 56 * 1024 * 1024

_NT = (((1,), (1,)), ((), ()))
_TN = (((0,), (0,)), ((), ()))


def _params(*sem):
    return pltpu.CompilerParams(dimension_semantics=sem, vmem_limit_bytes=VMEM_LIMIT)


def _mm_body(x_ref, w_ref, *o_refs, splits, chunk):
    x = x_ref[...].astype(BF16)
    col = 0
    for o_ref, width in zip(o_refs, splits):
        for j in range(0, width, chunk):
            c = min(chunk, width - j)
            y = jnp.dot(x, w_ref[:, col + j:col + j + c], preferred_element_type=F32)
            o_ref[:, j:j + c] = y.astype(o_ref.dtype)
        col += width


def matmul_split(x, w, splits, dtypes, tm, chunk=512):
    m, k = x.shape
    n = w.shape[1]
    assert n == sum(splits) and m % tm == 0
    return pl.pallas_call(
        functools.partial(_mm_body, splits=tuple(splits), chunk=chunk),
        grid=(m // tm,),
        in_specs=[pl.BlockSpec((tm, k), lambda i: (i, 0)), pl.BlockSpec((k, n), lambda i: (0, 0))],
        out_specs=[pl.BlockSpec((tm, s), lambda i: (i, 0)) for s in splits],
        out_shape=[jax.ShapeDtypeStruct((m, s), d) for s, d in zip(splits, dtypes)],
        compiler_params=_params("arbitrary"),
        name="matmul_split",
    )(x, w)


def _layer_norm(v, g, b):
    mu = jnp.mean(v, axis=-1, keepdims=True)
    c = v - mu
    var = jnp.mean(c * c, axis=-1, keepdims=True)
    return c * lax.rsqrt(var + LN_EPS) * g + b


def _mm_res_ln_body(a_ref, w_ref, x_ref, g_ref, b_ref, o_ref):
    y = jnp.dot(a_ref[...].astype(BF16), w_ref[...], preferred_element_type=F32)
    o_ref[...] = _layer_norm(ALPHA * x_ref[...] + y, g_ref[...], b_ref[...])


def matmul_res_ln(a, w, x, g, b, tm):
    m, k = a.shape
    d = x.shape[1]
    return pl.pallas_call(
        _mm_res_ln_body,
        grid=(m // tm,),
        in_specs=[pl.BlockSpec((tm, k), lambda i: (i, 0)), pl.BlockSpec((k, d), lambda i: (0, 0)),
                  pl.BlockSpec((tm, d), lambda i: (i, 0)), pl.BlockSpec((1, d), lambda i: (0, 0)),
                  pl.BlockSpec((1, d), lambda i: (0, 0))],
        out_specs=pl.BlockSpec((tm, d), lambda i: (i, 0)),
        out_shape=jax.ShapeDtypeStruct((m, d), F32),
        compiler_params=_params("arbitrary"),
        name="matmul_res_ln",
    )(a, w, x, g.reshape(1, d), b.reshape(1, d))


def _cmp_weights(w_cmp):
    dh = NSA_HEAD_DIM
    w6 = w_cmp.reshape(2, 2, CMP_STRIDE // 2, 2, dh, dh)
    eye = jnp.eye(2, dtype=w_cmp.dtype)
    r = jnp.einsum("chpsde,xy->cpsxdhye", w6, eye)
    return r.reshape(2, CMP_STRIDE // 2, 4 * dh, 4 * dh).astype(BF16)


def _cmp_body(*refs):
    x_refs, (r_ref, b_ref, o_ref) = refs[:-3], refs[-3:]
    row_w = 2 * NSA_KV_W

    def cols(c0):
        return jnp.concatenate([x_ref[0, :, c0:c0 + LANES] for x_ref in x_refs], axis=0)

    for c in range(2):
        for hp in range(2):
            acc = None
            for sp in range(CMP_STRIDE // 2):
                c0 = (2 * sp) * row_w + c * NSA_KV_W + hp * LANES
                lhs = jnp.concatenate([cols(c0), cols(c0 + row_w)], axis=1).astype(BF16)
                part = jnp.dot(lhs, r_ref[c, sp], preferred_element_type=F32)
                acc = part if acc is None else acc + part
            lo, hi = acc[:, :LANES], acc[:, LANES:]
            nxt = pltpu.roll(hi, hi.shape[0] - 1, 0)
            o0 = c * NSA_KV_W + hp * LANES
            o_ref[0, :, o0:o0 + LANES] = lo + nxt + b_ref[:, o0:o0 + LANES]


def nsa_compress(rows, r_w, b_cmp):
    bsz, t, w = rows.shape
    n_ch = t // CMP_STRIDE
    x = rows.reshape(bsz, n_ch, CMP_STRIDE * w)
    bias = jnp.broadcast_to(b_cmp[:, None, :], (2, NSA_KV_HEADS, NSA_HEAD_DIM)).reshape(1, w)
    return pl.pallas_call(
        _cmp_body,
        grid=(bsz,),
        in_specs=[pl.BlockSpec((1, n_ch, CMP_STRIDE * w), lambda b: (b, 0, 0)),
                  pl.BlockSpec(r_w.shape, lambda b: (0, 0, 0, 0)),
                  pl.BlockSpec((1, w), lambda b: (0, 0))],
        out_specs=pl.BlockSpec((1, n_ch, w), lambda b: (b, 0, 0)),
        out_shape=jax.ShapeDtypeStruct((bsz, n_ch, w), F32),
        compiler_params=_params("arbitrary"),
        name="nsa_compress",
    )(x, r_w, bias)


def _cmp_paged_body(pt_ref, *refs):
    page_refs, (r_ref, b_ref, o_ref, rows_scr) = refs[:-4], refs[-4:]
    page = page_refs[0].shape[-1]
    for pi, p_ref in enumerate(page_refs):
        for c in range(2):
            for hp in range(2):
                pair = p_ref[0, c, 2 * hp:2 * hp + 2].reshape(2 * NSA_HEAD_DIM, page)
                rows_scr[c * 2 + hp, pi * page:(pi + 1) * page, :] = pair.T
    n_ch = o_ref.shape[1]
    for c in range(2):
        for hp in range(2):
            acc = None
            for sp in range(CMP_STRIDE // 2):
                pos = [rows_scr[c * 2 + hp, pl.ds(2 * sp + sl, n_ch, stride=CMP_STRIDE), :] for sl in range(2)]
                part = jnp.dot(jnp.concatenate(pos, axis=1).astype(BF16), r_ref[c, sp], preferred_element_type=F32)
                acc = part if acc is None else acc + part
            lo, hi = acc[:, :LANES], acc[:, LANES:]
            nxt = pltpu.roll(hi, hi.shape[0] - 1, 0)
            o0 = c * NSA_KV_W + hp * LANES
            o_ref[0, :, o0:o0 + LANES] = lo + nxt + b_ref[:, o0:o0 + LANES]


def nsa_compress_paged(cache_t, page_table, r_w, b_cmp):
    n_phys, _, kvh, dh, page = cache_t.shape
    n, n_pages = page_table.shape
    w = 2 * kvh * dh
    n_ch = n_pages * page // CMP_STRIDE
    bias = jnp.broadcast_to(b_cmp[:, None, :], (2, kvh, dh)).reshape(1, w)
    page_spec = lambda k: pl.BlockSpec((1, 2, kvh, dh, page), lambda b, pt: (pt[b * n_pages + k], 0, 0, 0, 0))
    return pl.pallas_call(
        _cmp_paged_body,
        grid_spec=pltpu.PrefetchScalarGridSpec(
            num_scalar_prefetch=1,
            grid=(n,),
            in_specs=[page_spec(k) for k in range(n_pages)]
            + [pl.BlockSpec(r_w.shape, lambda b, pt: (0, 0, 0, 0)), pl.BlockSpec((1, w), lambda b, pt: (0, 0))],
            out_specs=pl.BlockSpec((1, n_ch, w), lambda b, pt: (b, 0, 0)),
            scratch_shapes=[pltpu.VMEM((4, n_pages * page, LANES), F32)]),
        out_shape=jax.ShapeDtypeStruct((n, n_ch, w), F32),
        compiler_params=_params("arbitrary"),
        name="nsa_compress_paged",
    )(page_table.reshape(-1), *([cache_t] * n_pages), r_w, bias)


def _softmax_cols(s, ok):
    s = jnp.where(ok, s, NEG)
    m = jnp.max(s, axis=0, keepdims=True)
    p = jnp.where(ok, jnp.exp(s - m), 0.0)
    l = jnp.sum(p, axis=0, keepdims=True)
    return p / jnp.where(l > 0.0, l, 1.0)


def _nsa_prompt_body(q_ref, kc_ref, vct_ref, ks_ref, vst_ref, kw_ref, vwt_ref, g_ref, o_ref, imp_ref, bias_ref, s_ref,
                     *, n_cmp, n_sel, i0, n_sel_tiles):
    i = i0 + pl.program_id(2)
    t0 = i * Q_TILE
    gq = NSA_GROUP * Q_TILE
    tq = t0 + lax.broadcasted_iota(I32, (1, Q_TILE), 1)
    tq4 = t0 + (lax.broadcasted_iota(I32, (1, gq), 1) & (Q_TILE - 1))
    q = q_ref[0, 0, 0]

    n_rows = kc_ref.shape[2]
    s = lax.dot_general(kc_ref[0, 0], q, _NT, preferred_element_type=F32)
    n_idx = lax.broadcasted_iota(I32, (n_rows, 1), 0)
    ok_c = (n_idx * CMP_STRIDE + (CMP_LEN - 1) <= tq4) & (n_idx < n_cmp)
    p_c = _softmax_cols(s, ok_c)
    o_c = jnp.dot(vct_ref[0, 0], p_c.astype(BF16), preferred_element_type=F32)

    imp = p_c[:, 0:Q_TILE]
    for g in range(1, NSA_GROUP):
        imp = imp + p_c[:, g * Q_TILE:(g + 1) * Q_TILE]
    imp_ref[...] = imp
    r_sel = SEL_BLOCK // CMP_STRIDE
    n_blk = n_rows // r_sel
    blk_imp = imp_ref[pl.ds(0, n_blk, stride=r_sel), :]
    for r in range(1, r_sel):
        blk_imp = blk_imp + imp_ref[pl.ds(r, n_blk, stride=r_sel), :]
    j_idx = lax.broadcasted_iota(I32, (n_blk, 1), 0)
    score = jnp.where(j_idx == tq // SEL_BLOCK, NSA_GROUP + 2.0,
                      jnp.where(j_idx == 0, NSA_GROUP + 1.0,
                                jnp.where(j_idx * SEL_BLOCK <= tq, blk_imp, -1.0)))
    score = jnp.where(j_idx < n_sel, score, -3.0)
    sel = jnp.zeros((n_blk, Q_TILE), F32)
    for _ in range(min(TOP_N, n_sel)):
        mx = jnp.max(score, axis=0, keepdims=True)
        first = jnp.min(jnp.where(score == mx, j_idx, n_blk), axis=0, keepdims=True)
        pick = j_idx == first
        sel = jnp.where(pick & (mx >= 0.0), 1.0, sel)
        score = jnp.where(pick, -2.0, score)
    blk_per_tile = KV_TILE // SEL_BLOCK
    for j in range(min(n_sel, n_sel_tiles * blk_per_tile)):
        bias_ref[j * SEL_BLOCK:(j + 1) * SEL_BLOCK, :] = jnp.broadcast_to(
            jnp.where(sel[j:j + 1, :] > 0.0, 0.0, NEG), (SEL_BLOCK, Q_TILE))

    def attend(k_ref, vt_ref, tiles, bias_fn):
        m = jnp.full((1, gq), NEG, F32)
        for n, kt in enumerate(tiles):
            k0 = kt * KV_TILE if isinstance(kt, int) else pl.multiple_of(kt * KV_TILE, KV_TILE)
            s = lax.dot_general(k_ref[0, 0, pl.ds(k0, KV_TILE), :], q, _NT, preferred_element_type=F32)
            kpos = k0 + lax.broadcasted_iota(I32, (KV_TILE, 1), 0)
            s = s + jnp.concatenate([bias_fn(kt, k0, kpos)] * NSA_GROUP, axis=1)
            s_ref[n * KV_TILE:(n + 1) * KV_TILE, :] = s
            m = jnp.maximum(m, jnp.max(s, axis=0, keepdims=True))
        acc = jnp.zeros((VT_ROWS, gq), F32)
        for n, kt in enumerate(tiles):
            p = jnp.exp(s_ref[n * KV_TILE:(n + 1) * KV_TILE, :] - m).astype(BF16)
            acc = acc + jnp.dot(vt_ref[0, 0, kt], p, preferred_element_type=F32)
        return acc[0:NSA_HEAD_DIM] / acc[NSA_HEAD_DIM:NSA_HEAD_DIM + 1]

    first_diag = (i0 * Q_TILE) // KV_TILE

    def sel_bias(kt, k0, kpos):
        bias = bias_ref[pl.ds(k0, KV_TILE), :]
        return bias if kt < first_diag else jnp.where(kpos <= tq, bias, NEG)

    def win_bias(kt, k0, kpos):
        dist = tq - kpos
        return jnp.where((dist >= 0) & (dist <= WINDOW), 0.0, NEG)

    o_s = attend(ks_ref, vst_ref, list(range(n_sel_tiles)), sel_bias)
    last = (t0 + Q_TILE - 1) // KV_TILE
    n_win_tiles = WINDOW // KV_TILE + 1
    if (i0 * Q_TILE + Q_TILE - 1) // KV_TILE < n_win_tiles - 1:
        win_tiles = list(range(n_sel_tiles))
    else:
        win_tiles = [last - (n_win_tiles - 1) + r for r in range(n_win_tiles)]
    o_w = attend(kw_ref, vwt_ref, win_tiles, win_bias)

    gate = 1.0 / (1.0 + jnp.exp(-g_ref[0, 0]))
    for g in range(NSA_GROUP):
        sl = slice(g * Q_TILE, (g + 1) * Q_TILE)
        o = gate[g, 0:1, :] * o_c[:, sl] + gate[g, 1:2, :] * o_s[:, sl] + gate[g, 2:3, :] * o_w[:, sl]
        o_ref[0, 0, g] = o.astype(o_ref.dtype)


VT_ROWS = NSA_HEAD_DIM + 16


def _head_major(kv):
    bsz, t, _ = kv.shape
    kv = kv.astype(BF16).reshape(bsz, t, 2, NSA_KV_HEADS, NSA_HEAD_DIM)
    k = kv[:, :, 0].transpose(0, 2, 1, 3)
    vt = kv[:, :, 1].reshape(bsz, t // KV_TILE, KV_TILE, NSA_KV_HEADS, NSA_HEAD_DIM).transpose(0, 3, 1, 4, 2)
    lead = vt.shape[:3]
    ones = jnp.ones(lead + (1, KV_TILE), BF16)
    zeros = jnp.zeros(lead + (VT_ROWS - NSA_HEAD_DIM - 1, KV_TILE), BF16)
    return k, jnp.concatenate([vt, ones, zeros], axis=3)


def nsa_prompt_attention(q, kvc, kv_s, kv_w, gates, n_cmp):
    bsz, t, _ = q.shape
    nq = t // Q_TILE
    n_sel = -(-t // SEL_BLOCK)
    n_rows = kvc.shape[1]
    dh, kvh, grp = NSA_HEAD_DIM, NSA_KV_HEADS, NSA_GROUP
    qh = q.reshape(bsz, nq, Q_TILE, kvh, grp, dh).transpose(0, 3, 1, 4, 2, 5).reshape(bsz, kvh, nq, grp * Q_TILE, dh)
    kc5 = kvc.astype(BF16).reshape(bsz, n_rows, 2, kvh, dh)
    kc = kc5[:, :, 0].transpose(0, 2, 1, 3)
    vct = kc5[:, :, 1].transpose(0, 2, 3, 1)
    ks, vst = _head_major(kv_s)
    kw, vwt = _head_major(kv_w)
    gt = gates[:, :, :NSA_HEADS * 3].reshape(bsz, t, kvh, grp, 3).transpose(0, 2, 3, 4, 1)
    nkt = t // KV_TILE
    assert WINDOW % KV_TILE == 0 and KV_TILE % Q_TILE == 0 and nq % Q_TILES_PER_CALL == 0
    bh = lambda b, h, i: (b, h, 0, 0)
    outs = []
    for i0 in range(0, nq, Q_TILES_PER_CALL):
        n_sel_tiles = ((i0 + Q_TILES_PER_CALL) * Q_TILE - 1) // KV_TILE + 1
        outs.append(pl.pallas_call(
            functools.partial(_nsa_prompt_body, n_cmp=n_cmp, n_sel=n_sel, i0=i0, n_sel_tiles=n_sel_tiles),
            grid=(bsz, kvh, Q_TILES_PER_CALL),
            in_specs=[pl.BlockSpec((1, 1, 1, grp * Q_TILE, dh), lambda b, h, i, i0=i0: (b, h, i0 + i, 0, 0)),
                      pl.BlockSpec((1, 1, n_rows, dh), bh),
                      pl.BlockSpec((1, 1, dh, n_rows), bh),
                      pl.BlockSpec((1, 1, t, dh), bh),
                      pl.BlockSpec((1, 1, nkt, VT_ROWS, KV_TILE), lambda b, h, i: (b, h, 0, 0, 0)),
                      pl.BlockSpec((1, 1, t, dh), bh),
                      pl.BlockSpec((1, 1, nkt, VT_ROWS, KV_TILE), lambda b, h, i: (b, h, 0, 0, 0)),
                      pl.BlockSpec((1, 1, grp, 3, Q_TILE), lambda b, h, i, i0=i0: (b, h, 0, 0, i0 + i))],
            out_specs=pl.BlockSpec((1, 1, grp, dh, Q_TILE), lambda b, h, i: (b, h, 0, 0, i)),
            out_shape=jax.ShapeDtypeStruct((bsz, kvh, grp, dh, Q_TILES_PER_CALL * Q_TILE), BF16),
            scratch_shapes=[pltpu.VMEM((n_rows, Q_TILE), F32), pltpu.VMEM((n_sel * SEL_BLOCK, Q_TILE), F32),
                            pltpu.VMEM((max(n_sel_tiles, WINDOW // KV_TILE + 1) * KV_TILE, grp * Q_TILE), F32)],
            compiler_params=_params("arbitrary", "arbitrary", "arbitrary"),
            name=f"nsa_prompt_attention_q{i0}",
        )(qh, kc, vct, ks, vst, kw, vwt, gt))
    out = jnp.concatenate(outs, axis=-1)
    return out.transpose(0, 4, 1, 2, 3).reshape(bsz, t, kvh * grp * dh)


def _nsa_w_in_padded(w_in):
    wq = w_in[:, :NSA_Q_W] * NSA_HEAD_DIM ** -0.5
    pad = jnp.zeros((w_in.shape[0], LANES - 3 * NSA_HEADS), w_in.dtype)
    return jnp.concatenate([wq, w_in[:, NSA_Q_W:], pad], axis=1).astype(BF16)


_NSA_SPLITS = (NSA_Q_W, 2 * NSA_KV_W, 2 * NSA_KV_W, 2 * NSA_KV_W, LANES)
_NSA_DTYPES = (BF16, F32, F32, F32, F32)


def nsa_prompt_mixer(x, w_in_p, r_cmp, b_cmp):
    bsz, t, d = x.shape
    q, kv_c, kv_s, kv_w, gates = matmul_split(x.reshape(bsz * t, d), w_in_p, _NSA_SPLITS, _NSA_DTYPES, tm=min(512, bsz * t))
    r3 = lambda a: a.reshape(bsz, t, a.shape[-1])
    kv_c, kv_s, kv_w = r3(kv_c), r3(kv_s), r3(kv_w)
    n_cmp = (t - CMP_LEN) // CMP_STRIDE + 1
    kvc = nsa_compress(kv_c, r_cmp, b_cmp)
    o = nsa_prompt_attention(r3(q), kvc, kv_s, kv_w, r3(gates), n_cmp)
    return o.reshape(bsz * t, NSA_Q_W), kv_c, kv_s, kv_w


def _ret_log_decay():
    return jnp.log(1.0 - 2.0 ** (-5.0 - jnp.arange(RET_HEADS, dtype=F32)))


def _rope_tables(pos):
    half = RET_DK // 2
    inv = ROPE_BASE ** (-jnp.linspace(0.0, 1.0, half, dtype=F32))
    ang = pos.astype(F32)[:, None] * inv[None, :]
    return jnp.cos(ang), jnp.sin(ang)


def _rope(x, cos, sin):
    half = RET_DK // 2
    x1, x2 = x[:, :half], x[:, half:]
    return jnp.concatenate([x1 * cos - x2 * sin, x2 * cos + x1 * sin], axis=1)


def _group_norm_gate(o, gate, g, b):
    mu = jnp.mean(o, axis=-1, keepdims=True)
    c = o - mu
    var = jnp.mean(c * c, axis=-1, keepdims=True)
    on = c * lax.rsqrt(var + LN_EPS) * g + b
    return gate * (1.0 / (1.0 + jnp.exp(-gate))) * on


def _ret_prompt_body(q_ref, k_ref, v_ref, gt_ref, cos_ref, sin_ref, dm_ref, di_ref, dr_ref, dc_ref, gg_ref, gb_ref,
                     o_ref, s_ref, s_scr):
    c = pl.program_id(1)

    @pl.when(c == 0)
    def _():
        s_scr[...] = jnp.zeros_like(s_scr)

    cos, sin = cos_ref[...], sin_ref[...]
    for h in range(RET_HEADS):
        qk = slice(h * RET_DK, (h + 1) * RET_DK)
        vv = slice(h * RET_DV, (h + 1) * RET_DV)
        q = _rope(q_ref[:, qk], cos, sin)
        k = _rope(k_ref[:, qk], cos, sin) * RET_DK ** -0.5
        qb, v = q.astype(BF16), v_ref[:, vv]
        inner = lax.dot_general(qb, k.astype(BF16), _NT, preferred_element_type=F32) * dm_ref[h]
        o = jnp.dot(inner.astype(BF16), v, preferred_element_type=F32)
        s_old = s_scr[h]
        o = o + jnp.dot(qb, s_old.astype(BF16), preferred_element_type=F32) * di_ref[h]
        kd = (k * dr_ref[h]).astype(BF16)
        s_new = dc_ref[h] * s_old + lax.dot_general(kd, v, _TN, preferred_element_type=F32)
        s_scr[h] = s_new
        s_ref[0, h] = s_new
        o_ref[:, vv] = _group_norm_gate(o, gt_ref[:, vv], gg_ref[:, vv], gb_ref[:, vv]).astype(o_ref.dtype)


def retention_prompt(q, k, v, gate, gn_g, gn_b, bsz, t):
    ch = RET_CHUNK
    n_ch = t // ch
    lg = _ret_log_decay()
    i = jnp.arange(ch, dtype=F32)
    diff = i[:, None] - i[None, :]
    dmask = jnp.where(diff >= 0, jnp.exp(lg[:, None, None] * jnp.maximum(diff, 0.0)), 0.0)
    d_in = jnp.exp((i[None, :] + 1.0) * lg[:, None])[:, :, None]
    d_rev = jnp.exp((ch - 1.0 - i)[None, :] * lg[:, None])[:, :, None]
    d_c = jnp.exp(ch * lg)[:, None, None]
    cos, sin = _rope_tables(jnp.arange(t))
    half = RET_DK // 2
    nh = RET_HEADS
    rows = lambda w: pl.BlockSpec((ch, w), lambda b, c: (b * n_ch + c, 0))
    whole = lambda a: pl.BlockSpec(a.shape, lambda b, c: (0,) * a.ndim)
    consts = (dmask, d_in, d_rev, d_c, gn_g.reshape(1, -1), gn_b.reshape(1, -1))
    return pl.pallas_call(
        _ret_prompt_body,
        grid=(bsz, n_ch),
        in_specs=[rows(nh * RET_DK), rows(nh * RET_DK), rows(nh * RET_DV), rows(nh * RET_DV),
                  pl.BlockSpec((ch, half), lambda b, c: (c, 0)), pl.BlockSpec((ch, half), lambda b, c: (c, 0))]
        + [whole(a) for a in consts],
        out_specs=[rows(nh * RET_DV), pl.BlockSpec((1, nh, RET_DK, RET_DV), lambda b, c: (b, 0, 0, 0))],
        out_shape=[jax.ShapeDtypeStruct((bsz * t, nh * RET_DV), BF16),
                   jax.ShapeDtypeStruct((bsz, nh, RET_DK, RET_DV), F32)],
        scratch_shapes=[pltpu.VMEM((nh, RET_DK, RET_DV), F32)],
        compiler_params=_params("arbitrary", "arbitrary"),
        name="retention_prompt",
    )(q, k, v, gate, cos, sin, *consts)


_RET_SPLITS = (RET_HEADS * RET_DK, RET_HEADS * RET_DK, RET_HEADS * RET_DV, RET_HEADS * RET_DV)
_RET_DTYPES = (F32, F32, BF16, F32)


def _ret_sample_body(q_ref, k_ref, v_ref, gt_ref, cos_ref, sin_ref, dec_ref, gg_ref, gb_ref, s_in_ref,
                     o_ref, s_out_ref):
    cos, sin = cos_ref[...], sin_ref[...]
    row0 = lax.broadcasted_iota(I32, (8, 1), 0) == 0
    for h in range(RET_HEADS):
        qh = _rope(q_ref[0, :, h * RET_DK:(h + 1) * RET_DK], cos, sin)
        kh = _rope(k_ref[0, :, h * RET_DK:(h + 1) * RET_DK], cos, sin) * RET_DK ** -0.5
        vh = v_ref[0, :, h * RET_DV:(h + 1) * RET_DV]
        qb, kb = qh.astype(BF16), kh.astype(BF16)
        dec = dec_ref[h]
        s_old = s_in_ref[0, h]
        q8 = jnp.broadcast_to(qb, (8, RET_DK))
        cross = jnp.dot(q8, s_old.astype(BF16), preferred_element_type=F32)[0:1] * dec
        inner = jnp.sum(qb.astype(F32) * kb.astype(F32), axis=1, keepdims=True)
        o = inner.astype(BF16).astype(F32) * vh.astype(F32) + cross
        k8 = jnp.where(row0, jnp.broadcast_to(kb.astype(F32), (8, RET_DK)), 0.0).astype(BF16)
        v8 = jnp.broadcast_to(vh, (8, RET_DV))
        s_out_ref[0, h] = dec * s_old + lax.dot_general(k8, v8, _TN, preferred_element_type=F32)
        sl = slice(h * RET_DV, (h + 1) * RET_DV)
        o_ref[0, :, sl] = _group_norm_gate(o, gt_ref[0, :, sl], gg_ref[:, sl], gb_ref[:, sl]).astype(o_ref.dtype)


def retention_sample(q, k, v, gate, gn_g, gn_b, state, pos0):
    n = q.shape[0]
    cos, sin = _rope_tables(jnp.full((1,), pos0))
    dec = jnp.exp(_ret_log_decay())
    r3 = lambda a: a.reshape(n, 1, a.shape[-1])
    w_qk, w_v = RET_HEADS * RET_DK, RET_HEADS * RET_DV
    half = RET_DK // 2
    vec = lambda w: pl.BlockSpec((1, 1, w), lambda b: (b, 0, 0))
    og, s_new = pl.pallas_call(
        _ret_sample_body,
        grid=(n,),
        in_specs=[vec(w_qk), vec(w_qk), vec(w_v), vec(w_v),
                  pl.BlockSpec((1, half), lambda b: (0, 0)), pl.BlockSpec((1, half), lambda b: (0, 0)),
                  pl.BlockSpec(memory_space=pltpu.SMEM),
                  pl.BlockSpec((1, w_v), lambda b: (0, 0)), pl.BlockSpec((1, w_v), lambda b: (0, 0)),
                  pl.BlockSpec((1, RET_HEADS, RET_DK, RET_DV), lambda b: (b, 0, 0, 0))],
        out_specs=[vec(w_v), pl.BlockSpec((1, RET_HEADS, RET_DK, RET_DV), lambda b: (b, 0, 0, 0))],
        out_shape=[jax.ShapeDtypeStruct((n, 1, w_v), BF16), jax.ShapeDtypeStruct(state.shape, F32)],
        compiler_params=_params("arbitrary"),
        name="retention_sample",
    )(r3(q), r3(k), r3(v), r3(gate), cos, sin, dec, gn_g.reshape(1, -1), gn_b.reshape(1, -1), state)
    return og.reshape(n, w_v), s_new


POOL_HALO = 16


def _pool_mix(win_sum_fn, x, pos, w_ref, sc_ref, g_ref, b_ref):
    ys = []
    for gi, w in enumerate(POOL_WINDOWS):
        sl = slice(gi * POOL_GROUP, (gi + 1) * POOL_GROUP)
        pooled = win_sum_fn(gi, w) / jnp.minimum(float(w), pos + 1.0) - x[:, sl]
        ys.append(jnp.dot(pooled.astype(BF16), w_ref[gi], preferred_element_type=F32))
    y = jnp.concatenate(ys, axis=1) * sc_ref[...]
    return _layer_norm(ALPHA * x + y, g_ref[...], b_ref[...])


def _pool_prompt_body(x_ref, w_ref, sc_ref, g_ref, b_ref, o_ref, xh_scr, *, tt):
    j = pl.program_id(1)

    @pl.when(j == 0)
    def _():
        xh_scr[0:POOL_HALO, :] = jnp.zeros((POOL_HALO, D_MODEL), F32)

    x = x_ref[0]
    xh_scr[POOL_HALO:, :] = x
    pos = (j * tt + lax.broadcasted_iota(I32, (tt, 1), 0)).astype(F32)

    def win_sum(gi, w):
        c0 = gi * POOL_GROUP
        acc = x[:, c0:c0 + POOL_GROUP]
        for u in range(1, w):
            acc = acc + xh_scr[POOL_HALO - u:POOL_HALO - u + tt, c0:c0 + POOL_GROUP]
        return acc

    o_ref[0] = _pool_mix(win_sum, x, pos, w_ref, sc_ref, g_ref, b_ref)
    xh_scr[0:POOL_HALO, :] = x[tt - POOL_HALO:, :]


def pool_prompt(x, w_pool, scale, g, b, tt=256):
    bsz, t, d = x.shape
    tt = min(tt, t)
    row = lambda b_, j: (0, 0)
    return pl.pallas_call(
        functools.partial(_pool_prompt_body, tt=tt),
        grid=(bsz, t // tt),
        in_specs=[pl.BlockSpec((1, tt, d), lambda b_, j: (b_, j, 0)),
                  pl.BlockSpec(w_pool.shape, lambda b_, j: (0, 0, 0)),
                  pl.BlockSpec((1, d), row), pl.BlockSpec((1, d), row), pl.BlockSpec((1, d), row)],
        out_specs=pl.BlockSpec((1, tt, d), lambda b_, j: (b_, j, 0)),
        out_shape=jax.ShapeDtypeStruct(x.shape, F32),
        scratch_shapes=[pltpu.VMEM((POOL_HALO + tt, d), F32)],
        compiler_params=_params("arbitrary", "arbitrary"),
        name="pool_prompt",
    )(x, w_pool, scale.reshape(1, d), g.reshape(1, d), b.reshape(1, d))


def _pool_sample_body(x_ref, st_ref, w_ref, sc_ref, g_ref, b_ref, o_ref, *, pos0):
    x = x_ref[...]
    n_hist = st_ref.shape[0]

    def win_sum(gi, w):
        c0 = gi * POOL_GROUP
        acc = x[:, c0:c0 + POOL_GROUP]
        for u in range(1, w):
            acc = acc + st_ref[n_hist - u, :, c0:c0 + POOL_GROUP]
        return acc

    pos = jnp.full((x.shape[0], 1), float(pos0), F32)
    o_ref[...] = _pool_mix(win_sum, x, pos, w_ref, sc_ref, g_ref, b_ref)


def pool_sample(x, hist, w_pool, scale, g, b, pos0):
    n, d = x.shape
    whole = lambda a: pl.BlockSpec(a.shape, lambda i: (0,) * a.ndim)
    args = (x, hist, w_pool, scale.reshape(1, d), g.reshape(1, d), b.reshape(1, d))
    return pl.pallas_call(
        functools.partial(_pool_sample_body, pos0=pos0),
        grid=(1,),
        in_specs=[whole(a) for a in args],
        out_specs=pl.BlockSpec((n, d), lambda i: (0, 0)),
        out_shape=jax.ShapeDtypeStruct((n, d), F32),
        compiler_params=_params("arbitrary"),
        name="pool_sample",
    )(*args)


CONV_HALO = 8


def _conv_prompt_body(bg_ref, cg_ref, h_ref, wc_ref, a_ref, tail_ref, uh_scr, *, tt):
    j = pl.program_id(1)

    @pl.when(j == 0)
    def _():
        uh_scr[0:CONV_HALO, :] = jnp.zeros((CONV_HALO, D_MODEL), F32)

    u = cg_ref[...] * h_ref[...]
    uh_scr[CONV_HALO:, :] = u
    conv = wc_ref[CONV_W - 1:CONV_W, :] * u
    for jj in range(CONV_W - 1):
        back = CONV_W - 1 - jj
        conv = conv + wc_ref[jj:jj + 1, :] * uh_scr[CONV_HALO - back:CONV_HALO - back + tt, :]
    a_ref[...] = (bg_ref[...] * conv).astype(a_ref.dtype)
    uh_scr[0:CONV_HALO, :] = u[tt - CONV_HALO:, :]
    tail_ref[0] = u[tt - CONV_HALO:, :]


def conv_prompt(bg, cg, h, w_conv, bsz, t, tt=256):
    d = bg.shape[1]
    tt = min(tt, t)
    nt = t // tt
    blk = pl.BlockSpec((tt, d), lambda b_, j: (b_ * nt + j, 0))
    return pl.pallas_call(
        functools.partial(_conv_prompt_body, tt=tt),
        grid=(bsz, nt),
        in_specs=[blk, blk, blk, pl.BlockSpec((CONV_W, d), lambda b_, j: (0, 0))],
        out_specs=[blk, pl.BlockSpec((1, CONV_HALO, d), lambda b_, j: (b_, 0, 0))],
        out_shape=[jax.ShapeDtypeStruct((bsz * t, d), BF16), jax.ShapeDtypeStruct((bsz, CONV_HALO, d), F32)],
        scratch_shapes=[pltpu.VMEM((CONV_HALO + tt, d), F32)],
        compiler_params=_params("arbitrary", "arbitrary"),
        name="conv_prompt",
    )(bg, cg, h, w_conv)


def _conv_sample_body(bg_ref, cg_ref, h_ref, prev_ref, wc_ref, a_ref, u_ref):
    u = cg_ref[...] * h_ref[...]
    conv = wc_ref[CONV_W - 1:CONV_W, :] * u
    for jj in range(CONV_W - 1):
        conv = conv + wc_ref[jj:jj + 1, :] * prev_ref[jj]
    a_ref[...] = (bg_ref[...] * conv).astype(a_ref.dtype)
    u_ref[...] = u


def conv_sample(bg, cg, h, prev, w_conv):
    n, d = bg.shape
    whole = lambda a: pl.BlockSpec(a.shape, lambda i: (0,) * a.ndim)
    args = (bg, cg, h, prev, w_conv)
    return pl.pallas_call(
        _conv_sample_body,
        grid=(1,),
        in_specs=[whole(a) for a in args],
        out_specs=[pl.BlockSpec((n, d), lambda i: (0, 0)), pl.BlockSpec((n, d), lambda i: (0, 0))],
        out_shape=[jax.ShapeDtypeStruct((n, d), BF16), jax.ShapeDtypeStruct((n, d), F32)],
        compiler_params=_params("arbitrary"),
        name="conv_sample",
    )(*args)


_CONV_SPLITS = (D_MODEL, D_MODEL, D_MODEL)
_CONV_DTYPES = (F32, F32, F32)


ROUTER_LANES = LANES
EXPERT_LANE0 = MOE_GROUPS


COMB_LANE0 = 8


def _router_body(x_ref, w_ref, b_ref, o_ref, cnt_ref, run_scr):
    i = pl.program_id(0)

    @pl.when(i == 0)
    def _():
        run_scr[...] = jnp.zeros_like(run_scr)

    x = x_ref[...]
    x_hi = x.astype(BF16)
    x_lo = (x - x_hi.astype(F32)).astype(BF16)
    logits = (jnp.dot(x_hi, w_ref[0], preferred_element_type=F32) + jnp.dot(x_lo, w_ref[0], preferred_element_type=F32)
              + jnp.dot(x_hi, w_ref[1], preferred_element_type=F32) + b_ref[...])
    tm = logits.shape[0]
    lane = lax.broadcasted_iota(I32, logits.shape, 1)
    big = ROUTER_LANES

    def top1(mask):
        v = jnp.max(jnp.where(mask, logits, -jnp.inf), axis=1, keepdims=True)
        idx = jnp.min(jnp.where(mask & (logits == v), lane, big), axis=1, keepdims=True)
        return v, idx

    is_g = lane < MOE_GROUPS
    vg, gsel = top1(is_g)
    pg_sel = 1.0 / jnp.sum(jnp.where(is_g, jnp.exp(logits - vg), 0.0), axis=1, keepdims=True)
    e0 = EXPERT_LANE0 + gsel * MOE_EPG
    is_e = (lane >= e0) & (lane < e0 + MOE_EPG)
    v1, i1 = top1(is_e)
    v2, i2 = top1(is_e & (lane != i1))
    r = jnp.exp(v2 - v1)
    pe1 = pg_sel / (1.0 + r)
    pe2 = pg_sel * r / (1.0 + r)
    local = lane - COMB_LANE0
    comb = jnp.where(local == i1 - e0, pe1, 0.0) + jnp.where(local == i2 - e0, pe2, 0.0)

    onehot = jnp.where(lane == gsel, 1.0, 0.0)
    rows = lax.broadcasted_iota(I32, (tm, tm), 0)
    cols = lax.broadcasted_iota(I32, (tm, tm), 1)
    before = jnp.where(cols < rows, 1.0, 0.0).astype(BF16)
    prefix = jnp.dot(before, onehot.astype(BF16), preferred_element_type=F32)
    run = run_scr[0:1, :]
    rank = jnp.sum(jnp.where(lane == gsel, prefix + run, 0.0), axis=1, keepdims=True)
    run = run + jnp.sum(onehot, axis=0, keepdims=True)
    run_scr[0:1, :] = run
    cnt_ref[...] = jnp.broadcast_to(run, cnt_ref.shape)
    o_ref[...] = jnp.where(lane == 0, gsel.astype(F32), jnp.where(lane == 1, rank, comb))


def moe_router(x, w_r, b_r, tm):
    m, d = x.shape
    return pl.pallas_call(
        _router_body,
        grid=(m // tm,),
        in_specs=[pl.BlockSpec((tm, d), lambda i: (i, 0)), pl.BlockSpec((2, d, ROUTER_LANES), lambda i: (0, 0, 0)),
                  pl.BlockSpec((1, ROUTER_LANES), lambda i: (0, 0))],
        out_specs=[pl.BlockSpec((tm, ROUTER_LANES), lambda i: (i, 0)),
                   pl.BlockSpec((8, ROUTER_LANES), lambda i: (0, 0))],
        out_shape=[jax.ShapeDtypeStruct((m, ROUTER_LANES), F32), jax.ShapeDtypeStruct((8, ROUTER_LANES), F32)],
        scratch_shapes=[pltpu.VMEM((8, ROUTER_LANES), F32)],
        compiler_params=_params("arbitrary"),
        name="moe_router",
    )(x, w_r, b_r)


def _router_weights(w_rg, b_rg, w_re, b_re):
    d = w_rg.shape[0]
    pad = ROUTER_LANES - MOE_GROUPS - MOE_EXPERTS
    w = jnp.concatenate([w_rg, w_re, jnp.zeros((d, pad), F32)], axis=1)
    b = jnp.concatenate([b_rg, b_re, jnp.zeros((pad,), F32)]).reshape(1, ROUTER_LANES)
    w_hi = w.astype(BF16)
    w_lo = (w - w_hi.astype(F32)).astype(BF16)
    return jnp.stack([w_hi, w_lo]), b


def _moe_ffn_body(gid_ref, src_ref, x_hbm, c_ref, w1_ref, w3_ref, w2_ref, g_ref, b_ref, o_ref,
                  w1b, w3b, w2b, xbuf, sem_in, *, tm, n_tiles):
    i = pl.program_id(0)
    slot = i % 2
    other = 1 - slot

    def fetch_rows(tile, buf_slot):
        for r in range(tm):
            row = src_ref[tile * tm + r]
            pltpu.make_async_copy(x_hbm.at[pl.ds(row, 1)], xbuf.at[buf_slot, pl.ds(r, 1)],
                                  sem_in.at[buf_slot]).start(priority=r % 2)

    def wait_fetch(buf_slot):
        pltpu.make_async_copy(x_hbm.at[pl.ds(0, tm)], xbuf.at[buf_slot], sem_in.at[buf_slot]).wait()

    @pl.when(i == 0)
    def _():
        fetch_rows(0, 0)

    fetch_rows(jnp.minimum(i + 1, n_tiles - 1), other)
    wait_fetch(slot)

    @pl.when((i == 0) | (gid_ref[i] != gid_ref[jnp.maximum(i - 1, 0)]))
    def _():
        for e in range(MOE_EPG):
            w1b[e] = w1_ref[0, e].astype(BF16)
            w3b[e] = w3_ref[0, e].astype(BF16)
            w2b[e] = w2_ref[0, e].astype(BF16)

    x = jnp.concatenate([xbuf[slot, :, k, :] for k in range(xbuf.shape[2])], axis=1)
    xb = x.astype(BF16)
    comb = c_ref[...]
    y = jnp.zeros(x.shape, F32)
    for e in range(MOE_EPG):
        a = jnp.dot(xb, w1b[e], preferred_element_type=F32)
        gte = jnp.dot(xb, w3b[e], preferred_element_type=F32)
        hcol = a * (1.0 / (1.0 + jnp.exp(-a))) * gte * comb[:, e:e + 1]
        y = y + jnp.dot(hcol.astype(BF16), w2b[e], preferred_element_type=F32)
    o_ref[...] = _layer_norm(ALPHA * x + y, g_ref[...], b_ref[...])

    @pl.when(i == n_tiles - 1)
    def _():
        wait_fetch(other)


def moe_layer(x, router_w, router_b, w1, w3, w2, layer, g, b, tm, router_tm):
    m, d = x.shape
    r, cnt = moe_router(x, router_w, router_b, tm=router_tm)
    gid, rank = r[:, 0].astype(I32), r[:, 1].astype(I32)
    comb = r[:, COMB_LANE0:COMB_LANE0 + MOE_EPG]
    counts = cnt[0, :MOE_GROUPS].astype(I32)
    tiles = (counts + tm - 1) // tm
    tile_end = jnp.cumsum(tiles)
    start = (tile_end - tiles) * tm
    dest = rank
    for grp in range(MOE_GROUPS):
        dest = dest + jnp.where(gid == grp, start[grp], 0)
    n_tiles = -(-m // tm) + MOE_GROUPS
    mp = n_tiles * tm
    src = jnp.full((mp,), -1, I32).at[dest].set(jnp.arange(m, dtype=I32))
    live = (src >= 0).astype(F32)
    src = jnp.maximum(src, 0)
    cs = jnp.take(comb, src, axis=0, mode="clip") * live[:, None]
    tile_gid = jnp.minimum((jnp.arange(n_tiles, dtype=I32)[:, None] >= tile_end[None, :]).sum(axis=1),
                           MOE_GROUPS - 1).astype(I32)
    wspec = lambda shp: pl.BlockSpec((1,) + shp, lambda i, gid_, src_: (layer, gid_[i], 0, 0),
                                     pipeline_mode=pl.Buffered(1))
    const = lambda i, gid_, src_: (0, 0)
    out = pl.pallas_call(
        functools.partial(_moe_ffn_body, tm=tm, n_tiles=n_tiles),
        grid_spec=pltpu.PrefetchScalarGridSpec(
            num_scalar_prefetch=2,
            grid=(n_tiles,),
            in_specs=[pl.BlockSpec(memory_space=pl.ANY),
                      pl.BlockSpec((tm, MOE_EPG), lambda i, gid_, src_: (i, 0)),
                      wspec((MOE_EPG, d, MOE_FF)), wspec((MOE_EPG, d, MOE_FF)), wspec((MOE_EPG, MOE_FF, d)),
                      pl.BlockSpec((1, d), const), pl.BlockSpec((1, d), const)],
            out_specs=pl.BlockSpec((tm, d), lambda i, gid_, src_: (i, 0)),
            scratch_shapes=[pltpu.VMEM((MOE_EPG, d, MOE_FF), BF16), pltpu.VMEM((MOE_EPG, d, MOE_FF), BF16),
                            pltpu.VMEM((MOE_EPG, MOE_FF, d), BF16),
                            pltpu.VMEM((2, tm, d // LANES, LANES), F32), pltpu.SemaphoreType.DMA((2,))]),
        out_shape=jax.ShapeDtypeStruct((mp, d), F32),
        compiler_params=_params("arbitrary"),
        name="moe_ffn",
    )(tile_gid, src, x.reshape(m, d // LANES, LANES), cs, w1, w3, w2, g.reshape(1, d), b.reshape(1, d))
    return out, dest


NSA_ROWS = NSA_HEADS
SEL_SAMPLES = 8


def _softmax_rows(s, ok):
    s = jnp.where(ok, s, NEG)
    m = jnp.max(s, axis=1, keepdims=True)
    p = jnp.where(ok, jnp.exp(s - m), 0.0)
    l = jnp.sum(p, axis=1, keepdims=True)
    return p / jnp.where(l > 0.0, l, 1.0)


def _nsa_sample_select_body(qx_ref, kvc_ref, pool_ref, idx_ref, *, n_cmp, n_sel, q_pos):
    n_rows = kvc_ref.shape[1]
    lane = lax.broadcasted_iota(I32, (NSA_ROWS, n_rows), 1)
    ok_c = (lane * CMP_STRIDE + (CMP_LEN - 1) <= q_pos) & (lane < n_cmp)
    for s_i in range(SEL_SAMPLES):
        kc = kvc_ref[s_i, :, 0:NSA_KV_W].astype(BF16)
        s = lax.dot_general(qx_ref[s_i], kc, _NT, preferred_element_type=F32)
        p = _softmax_rows(s, ok_c)
        imp = p
        for g in range(1, NSA_GROUP):
            imp = imp + pltpu.roll(p, g * NSA_KV_HEADS, 0)
        blk = jnp.dot(imp, pool_ref[...], preferred_element_type=F32, precision=lax.Precision.HIGHEST)
        score = jnp.where(lane == q_pos // SEL_BLOCK, NSA_GROUP + 2.0,
                          jnp.where(lane == 0, NSA_GROUP + 1.0,
                                    jnp.where(lane * SEL_BLOCK <= q_pos, blk, -1.0)))
        score = jnp.where(lane < n_sel, score, -3.0)
        idx = jnp.full(score.shape, -1, I32)
        for k in range(min(TOP_N, n_sel)):
            mx = jnp.max(score, axis=1, keepdims=True)
            first = jnp.min(jnp.where(score == mx, lane, n_rows), axis=1, keepdims=True)
            idx = jnp.where(lane == k, jnp.where(mx >= 0.0, first, -1), idx)
            score = jnp.where(lane == first, -2.0, score)
        idx_ref[s_i] = idx[0:8, :]


def nsa_sample_select(qx, kvc, n_cmp, n_sel, q_pos):
    n, n_rows, w = kvc.shape
    r_sel = SEL_BLOCK // CMP_STRIDE
    pool = (jnp.arange(n_rows)[:, None] // r_sel == jnp.arange(n_rows)[None, :]).astype(F32)
    return pl.pallas_call(
        functools.partial(_nsa_sample_select_body, n_cmp=n_cmp, n_sel=n_sel, q_pos=q_pos),
        grid=(n // SEL_SAMPLES,),
        in_specs=[pl.BlockSpec((SEL_SAMPLES, NSA_ROWS, NSA_KV_W), lambda i: (i, 0, 0)),
                  pl.BlockSpec((SEL_SAMPLES, n_rows, w), lambda i: (i, 0, 0)),
                  pl.BlockSpec((n_rows, n_rows), lambda i: (0, 0))],
        out_specs=pl.BlockSpec((SEL_SAMPLES, 8, n_rows), lambda i: (i, 0, 0)),
        out_shape=jax.ShapeDtypeStruct((n, 8, n_rows), I32),
        compiler_params=_params("arbitrary"),
        name="nsa_sample_select",
    )(qx, kvc, pool)


Q_ROWS = 8


def _nsa_sample_attend_body(page_ref, half_ref, qh_ref, kvc_ref, win_ref, ksn_ref, kwn_ref, g_ref, *rest, n_cmp, q_pos):
    blk_refs, o_ref = rest[:-1], rest[-1]
    b = pl.program_id(0)
    dh = NSA_HEAD_DIM
    rnd = lambda a: a.astype(BF16).astype(F32)
    n_top = len(blk_refs) // NSA_KV_HEADS
    page = blk_refs[0].shape[-1]
    n_rows = kvc_ref.shape[1]
    lane_c = lax.broadcasted_iota(I32, (Q_ROWS, n_rows), 1)
    ok_c = (lane_c * CMP_STRIDE + (CMP_LEN - 1) <= q_pos) & (lane_c < n_cmp)
    lane_s = lax.broadcasted_iota(I32, (Q_ROWS, n_top * page), 1)
    gate = 1.0 / (1.0 + jnp.exp(-g_ref[0]))

    for h in range(NSA_KV_HEADS):
        q = qh_ref[0, h]
        qf = q.astype(F32)

        def with_new_key(k_t, v_t, ok, new_ref):
            s = jnp.dot(q, k_t, preferred_element_type=F32)
            s_new = jnp.sum(qf * rnd(new_ref[0, :, h * dh:(h + 1) * dh]), axis=1, keepdims=True)
            if ok is not None:
                s = jnp.where(ok, s, NEG)
            m = jnp.maximum(jnp.max(s, axis=1, keepdims=True), s_new)
            p = jnp.exp(s - m) if ok is None else jnp.where(ok, jnp.exp(s - m), 0.0)
            p_new = jnp.exp(s_new - m)
            l = jnp.sum(p, axis=1, keepdims=True) + p_new
            v_new = new_ref[0, :, NSA_KV_W + h * dh:NSA_KV_W + (h + 1) * dh]
            o = lax.dot_general(p.astype(BF16), v_t, _NT, preferred_element_type=F32)
            return (o + rnd(p_new) * rnd(v_new)) / l

        kc = kvc_ref[0, :, h * dh:(h + 1) * dh].astype(BF16)
        vc = kvc_ref[0, :, NSA_KV_W + h * dh:NSA_KV_W + (h + 1) * dh].astype(BF16)
        s = lax.dot_general(q, kc, _NT, preferred_element_type=F32)
        o_c = jnp.dot(_softmax_rows(s, ok_c).astype(BF16), vc, preferred_element_type=F32)

        o_w = with_new_key(win_ref[0, 0, h].astype(BF16), win_ref[0, 1, h].astype(BF16), None, kwn_ref)

        refs = blk_refs[h * n_top:(h + 1) * n_top]
        k_t = jnp.concatenate([r[0, 0, 0] for r in refs], axis=1).astype(BF16)
        v_t = jnp.concatenate([r[0, 1, 0] for r in refs], axis=1).astype(BF16)
        want = jnp.full(lane_s.shape, -1, I32)
        for k in range(n_top):
            want = jnp.where(lane_s // page == k, half_ref[(b * NSA_KV_HEADS + h) * n_top + k], want)
        o_s = with_new_key(k_t, v_t, (lane_s % page) // SEL_BLOCK == want, ksn_ref)

        o = gate[h, :, 0:1] * o_c + gate[h, :, 1:2] * o_s + gate[h, :, 2:3] * o_w
        o_ref[0, h] = o.astype(o_ref.dtype)


def nsa_sample_mixer(x, cache_cmp_t, cache_sel_t, win_t, page_table, w_in_p, r_cmp, b_cmp):
    n, d = x.shape
    n_pages = page_table.shape[1]
    past = n_pages * PAGE_SIZE
    dh, kvh, grp = NSA_HEAD_DIM, NSA_KV_HEADS, NSA_GROUP
    q, kv_c, kv_s, kv_w, gates = matmul_split(x, w_in_p, _NSA_SPLITS, _NSA_DTYPES, tm=n)
    n_cmp = (past + 1 - CMP_LEN) // CMP_STRIDE + 1
    n_sel = -(-(past + 1) // SEL_BLOCK)
    n_past_blk = past // SEL_BLOCK
    kvc = nsa_compress_paged(cache_cmp_t, page_table, r_cmp, b_cmp)
    q4 = q.reshape(n, kvh, grp, dh)
    q4t = q4.transpose(0, 2, 1, 3)
    qx = (q4t[:, :, :, None, :] * jnp.eye(kvh, dtype=q.dtype)[None, None, :, :, None]).reshape(n, NSA_ROWS, kvh * dh)
    idx = nsa_sample_select(qx, kvc, n_cmp, n_sel, past)[:, :kvh, :TOP_N]
    sub = PAGE_SIZE // SEL_BLOCK
    is_past = (idx >= 0) & (idx < n_past_blk)
    phys = jnp.take_along_axis(page_table, jnp.clip(idx // sub, 0, n_pages - 1).reshape(n, -1), axis=1).reshape(idx.shape)
    page_idx = jnp.where(is_past, phys, 0).astype(I32).reshape(-1)
    half = jnp.where(is_past, idx % sub, -1).astype(I32).reshape(-1)
    pad_rows = lambda a: jnp.pad(a, ((0, 0), (0, 0), (0, Q_ROWS - grp), (0, 0)))
    qh = pad_rows(q4)
    gt = pad_rows(gates[:, :NSA_HEADS * 3].reshape(n, kvh, grp, 3))
    n_top = idx.shape[2]
    per_seq = kvh * n_top
    vec = lambda a: a.reshape(n, 1, a.shape[-1])
    blk_spec = lambda k: pl.BlockSpec((1, 2, 1, dh, PAGE_SIZE),
                                      lambda b, p_, h_: (p_[b * per_seq + k], 0, k // n_top, 0, 0))
    seq = lambda a: pl.BlockSpec((1,) + a.shape[1:], lambda b, p_, h_: (b,) + (0,) * (a.ndim - 1))
    ins = (qh, kvc, win_t, vec(kv_s), vec(kv_w), gt)
    out = pl.pallas_call(
        functools.partial(_nsa_sample_attend_body, n_cmp=n_cmp, q_pos=past),
        grid_spec=pltpu.PrefetchScalarGridSpec(
            num_scalar_prefetch=2,
            grid=(n,),
            in_specs=[seq(a) for a in ins] + [blk_spec(k) for k in range(per_seq)],
            out_specs=pl.BlockSpec((1, kvh, Q_ROWS, dh), lambda b, p_, h_: (b, 0, 0, 0))),
        out_shape=jax.ShapeDtypeStruct((n, kvh, Q_ROWS, dh), BF16),
        compiler_params=_params("arbitrary"),
        name="nsa_sample_attend",
    )(page_idx, half, *ins, *([cache_sel_t] * per_seq))
    return out[:, :, :grp].reshape(n, NSA_Q_W), kv_c, kv_s, kv_w


PROMPT_TM = 512
ROUTER_TM = 384


def kernel(x_prompt, x_sample, cache_nsa_cmp, cache_nsa_sel, state_nsa_win, state_ret, state_pool, state_conv, page_table, nsa_w_in, nsa_w_cmp, nsa_b_cmp, nsa_w_o, ret_w_in, ret_gn_g, ret_gn_b, ret_w_o, pool_w, pool_scale, conv_w_in, conv_w, conv_w_out, ln_g, ln_b, moe_w_rg, moe_b_rg, moe_w_re, moe_b_re, moe_w1, moe_w3, moe_w2):
    bp, t, d = x_prompt.shape
    ns = x_sample.shape[0]
    assert x_sample.shape[1] == 1 and (bp * t) % PROMPT_TM == 0 and (bp * t + ns) % ROUTER_TM == 0
    past = page_table.shape[1] * PAGE_SIZE
    xp, xs = x_prompt.reshape(bp * t, d), x_sample.reshape(ns, d)
    kv5 = lambda a, n, rows: a.reshape(n, rows, 2, NSA_KV_HEADS, NSA_HEAD_DIM)
    shift_in = lambda old, new: jnp.concatenate([old[:, 1:], new[:, None]], axis=1)
    cmp_p, sel_p, win_p, ret_p, pool_p, conv_p = [], [], [], [], [], []
    cmp_s, sel_s, win_s, ret_s, pool_s, conv_s = [], [], [], [], [], []
    n_mix = 4
    for i in range(DEPTH):
        kind, j = i % n_mix, i // n_mix
        g0, b0 = ln_g[i, 0], ln_b[i, 0]
        if kind == 0:
            w_in_p, r_cmp, w_o = _nsa_w_in_padded(nsa_w_in[j]), _cmp_weights(nsa_w_cmp[j]), nsa_w_o[j].astype(BF16)
            o, a, b, c = nsa_prompt_mixer(xp.reshape(bp, t, d), w_in_p, r_cmp, nsa_b_cmp[j])
            cmp_p.append(kv5(a, bp, t)); sel_p.append(kv5(b, bp, t)); win_p.append(kv5(c, bp, t)[:, -min(WINDOW, t):])
            xp = matmul_res_ln(o, w_o, xp, g0, b0, PROMPT_TM)
            win = state_nsa_win[j]
            rows_minor = lambda a: jnp.transpose(a, (0, 2, 3, 4, 1))
            o, a, b, c = nsa_sample_mixer(xs, rows_minor(cache_nsa_cmp[j]), rows_minor(cache_nsa_sel[j]), rows_minor(win),
                                          page_table, w_in_p, r_cmp, nsa_b_cmp[j])
            cmp_s.append(kv5(a, ns, 1)); sel_s.append(kv5(b, ns, 1)); win_s.append(shift_in(win, kv5(c, ns, 1)[:, 0]))
            xs = matmul_res_ln(o, w_o, xs, g0, b0, ns)
        elif kind == 1:
            w_in, w_o = ret_w_in[j].astype(BF16), ret_w_o[j].astype(BF16)
            q, k, v, g = matmul_split(xp, w_in, _RET_SPLITS, _RET_DTYPES, tm=256)
            og, s_fin = retention_prompt(q, k, v, g, ret_gn_g[j], ret_gn_b[j], bp, t)
            ret_p.append(s_fin)
            xp = matmul_res_ln(og, w_o, xp, g0, b0, PROMPT_TM)
            q, k, v, g = matmul_split(xs, w_in, _RET_SPLITS, _RET_DTYPES, tm=ns)
            og, s_new = retention_sample(q, k, v, g, ret_gn_g[j], ret_gn_b[j], state_ret[j], past)
            ret_s.append(s_new)
            xs = matmul_res_ln(og, w_o, xs, g0, b0, ns)
        elif kind == 2:
            wp = pool_w[j].astype(BF16)
            pool_p.append(xp.reshape(bp, t, d)[:, -(POOL_MAX - 1):])
            xp = pool_prompt(xp.reshape(bp, t, d), wp, pool_scale[j], g0, b0).reshape(bp * t, d)
            pool_s.append(shift_in(state_pool[j], xs))
            xs = pool_sample(xs, state_pool[j].transpose(1, 0, 2), wp, pool_scale[j], g0, b0, past)
        else:
            w_in, w_out = conv_w_in[j].astype(BF16), conv_w_out[j].astype(BF16)
            bg, cg, h = matmul_split(xp, w_in, _CONV_SPLITS, _CONV_DTYPES, tm=PROMPT_TM)
            a, tail = conv_prompt(bg, cg, h, conv_w[j], bp, t)
            conv_p.append(tail[:, -(CONV_W - 1):])
            xp = matmul_res_ln(a, w_out, xp, g0, b0, PROMPT_TM)
            bg, cg, h = matmul_split(xs, w_in, _CONV_SPLITS, _CONV_DTYPES, tm=ns)
            a, u = conv_sample(bg, cg, h, state_conv[j].transpose(1, 0, 2), conv_w[j])
            conv_s.append(shift_in(state_conv[j], u))
            xs = matmul_res_ln(a, w_out, xs, g0, b0, ns)
        rw, rb = _router_weights(moe_w_rg[i], moe_b_rg[i], moe_w_re[i], moe_b_re[i])
        out, dest = moe_layer(jnp.concatenate([xp, xs], axis=0), rw, rb, moe_w1, moe_w3, moe_w2, i,
                              ln_g[i, 1], ln_b[i, 1], tm=PROMPT_TM, router_tm=ROUTER_TM)
        xp = jnp.take(out, dest[:bp * t], axis=0, mode="clip")
        xs = jnp.take(out, dest[bp * t:], axis=0, mode="clip")
    st = jnp.stack
    return (xp.reshape(bp, t, d), xs.reshape(ns, 1, d), st(cmp_p), st(sel_p), st(win_p), st(ret_p), st(pool_p), st(conv_p),
            st(cmp_s), st(sel_s), st(win_s), st(ret_s), st(pool_s), st(conv_s))
```

```python
import functools

import jax
import jax.numpy as jnp
import numpy as np
from jax import lax
from jax.experimental import pallas as pl
from jax.experimental.pallas import tpu as pltpu

F32, BF16, I32 = jnp.float32, jnp.bfloat16, jnp.int32

D_MODEL = 1024
DEPTH = 4
PAGE_SIZE = 128
NSA_HEADS = 16
NSA_HEAD_DIM = 64
NSA_KV_HEADS = 4
NSA_GROUP = NSA_HEADS // NSA_KV_HEADS
NSA_KV_W = NSA_KV_HEADS * NSA_HEAD_DIM
NSA_Q_W = NSA_HEADS * NSA_HEAD_DIM
CMP_LEN = 32
CMP_STRIDE = 16
SEL_BLOCK = 64
TOP_N = 8
WINDOW = 512
RET_HEADS = 4
RET_DK = D_MODEL // RET_HEADS
RET_DV = 2 * D_MODEL // RET_HEADS
RET_CHUNK = 128
ROPE_BASE = 10000.0
POOL_WINDOWS = (2, 4, 8, 16)
POOL_GROUP = D_MODEL // len(POOL_WINDOWS)
POOL_MAX = max(POOL_WINDOWS)
CONV_W = 3
MOE_GROUPS = 4
MOE_EPG = 8
MOE_EXPERTS = MOE_GROUPS * MOE_EPG
MOE_FF = D_MODEL // 4
ALPHA = (2.0 * DEPTH) ** 0.25
LN_EPS = 1e-5
NEG = -1e30

LANES = 128
Q_TILE = 128
KV_TILE = 256
Q_TILES_PER_CALL = 2
VMEM_LIMIT = 56 * 1024 * 1024

_NT = (((1,), (1,)), ((), ()))
_TN = (((0,), (0,)), ((), ()))


def _params(*sem):
    return pltpu.CompilerParams(dimension_semantics=sem, vmem_limit_bytes=VMEM_LIMIT)


def _mm_body(x_ref, w_ref, *o_refs, splits, chunk):
    x = x_ref[...].astype(BF16)
    col = 0
    for o_ref, width in zip(o_refs, splits):
        for j in range(0, width, chunk):
            c = min(chunk, width - j)
            y = jnp.dot(x, w_ref[:, col + j:col + j + c], preferred_element_type=F32)
            o_ref[:, j:j + c] = y.astype(o_ref.dtype)
        col += width


def matmul_split(x, w, splits, dtypes, tm, chunk=512):
    m, k = x.shape
    n = w.shape[1]
    assert n == sum(splits) and m % tm == 0
    return pl.pallas_call(
        functools.partial(_mm_body, splits=tuple(splits), chunk=chunk),
        grid=(m // tm,),
        in_specs=[pl.BlockSpec((tm, k), lambda i: (i, 0)), pl.BlockSpec((k, n), lambda i: (0, 0))],
        out_specs=[pl.BlockSpec((tm, s), lambda i: (i, 0)) for s in splits],
        out_shape=[jax.ShapeDtypeStruct((m, s), d) for s, d in zip(splits, dtypes)],
        compiler_params=_params("arbitrary"),
        name="matmul_split",
    )(x, w)


def _layer_norm(v, g, b):
    mu = jnp.mean(v, axis=-1, keepdims=True)
    c = v - mu
    var = jnp.mean(c * c, axis=-1, keepdims=True)
    return c * lax.rsqrt(var + LN_EPS) * g + b


def _mm_res_ln_body(a_ref, w_ref, x_ref, g_ref, b_ref, o_ref):
    y = jnp.dot(a_ref[...].astype(BF16), w_ref[...], preferred_element_type=F32)
    o_ref[...] = _layer_norm(ALPHA * x_ref[...] + y, g_ref[...], b_ref[...])


def matmul_res_ln(a, w, x, g, b, tm):
    m, k = a.shape
    d = x.shape[1]
    return pl.pallas_call(
        _mm_res_ln_body,
        grid=(m // tm,),
        in_specs=[pl.BlockSpec((tm, k), lambda i: (i, 0)), pl.BlockSpec((k, d), lambda i: (0, 0)),
                  pl.BlockSpec((tm, d), lambda i: (i, 0)), pl.BlockSpec((1, d), lambda i: (0, 0)),
                  pl.BlockSpec((1, d), lambda i: (0, 0))],
        out_specs=pl.BlockSpec((tm, d), lambda i: (i, 0)),
        out_shape=jax.ShapeDtypeStruct((m, d), F32),
        compiler_params=_params("arbitrary"),
        name="matmul_res_ln",
    )(a, w, x, g.reshape(1, d), b.reshape(1, d))


def _cmp_weights(w_cmp):
    dh = NSA_HEAD_DIM
    w6 = w_cmp.reshape(2, 2, CMP_STRIDE // 2, 2, dh, dh)
    eye = jnp.eye(2, dtype=w_cmp.dtype)
    r = jnp.einsum("chpsde,xy->cpsxdhye", w6, eye)
    return r.reshape(2, CMP_STRIDE // 2, 4 * dh, 4 * dh).astype(BF16)


def _cmp_body(*refs):
    x_refs, (r_ref, b_ref, o_ref) = refs[:-3], refs[-3:]
    row_w = 2 * NSA_KV_W

    def cols(c0):
        return jnp.concatenate([x_ref[0, :, c0:c0 + LANES] for x_ref in x_refs], axis=0)

    for c in range(2):
        for hp in range(2):
            acc = None
            for sp in range(CMP_STRIDE // 2):
                c0 = (2 * sp) * row_w + c * NSA_KV_W + hp * LANES
                lhs = jnp.concatenate([cols(c0), cols(c0 + row_w)], axis=1).astype(BF16)
                part = jnp.dot(lhs, r_ref[c, sp], preferred_element_type=F32)
                acc = part if acc is None else acc + part
            lo, hi = acc[:, :LANES], acc[:, LANES:]
            nxt = pltpu.roll(hi, hi.shape[0] - 1, 0)
            o0 = c * NSA_KV_W + hp * LANES
            o_ref[0, :, o0:o0 + LANES] = lo + nxt + b_ref[:, o0:o0 + LANES]


def nsa_compress(rows, r_w, b_cmp):
    bsz, t, w = rows.shape
    n_ch = t // CMP_STRIDE
    x = rows.reshape(bsz, n_ch, CMP_STRIDE * w)
    bias = jnp.broadcast_to(b_cmp[:, None, :], (2, NSA_KV_HEADS, NSA_HEAD_DIM)).reshape(1, w)
    return pl.pallas_call(
        _cmp_body,
        grid=(bsz,),
        in_specs=[pl.BlockSpec((1, n_ch, CMP_STRIDE * w), lambda b: (b, 0, 0)),
                  pl.BlockSpec(r_w.shape, lambda b: (0, 0, 0, 0)),
                  pl.BlockSpec((1, w), lambda b: (0, 0))],
        out_specs=pl.BlockSpec((1, n_ch, w), lambda b: (b, 0, 0)),
        out_shape=jax.ShapeDtypeStruct((bsz, n_ch, w), F32),
        compiler_params=_params("arbitrary"),
        name="nsa_compress",
    )(x, r_w, bias)


def _cmp_paged_body(pt_ref, *refs):
    page_refs, (r_ref, b_ref, o_ref, rows_scr) = refs[:-4], refs[-4:]
    page = page_refs[0].shape[-1]
    for pi, p_ref in enumerate(page_refs):
        for c in range(2):
            for hp in range(2):
                pair = p_ref[0, c, 2 * hp:2 * hp + 2].reshape(2 * NSA_HEAD_DIM, page)
                rows_scr[c * 2 + hp, pi * page:(pi + 1) * page, :] = pair.T
    n_ch = o_ref.shape[1]
    for c in range(2):
        for hp in range(2):
            acc = None
            for sp in range(CMP_STRIDE // 2):
                pos = [rows_scr[c * 2 + hp, pl.ds(2 * sp + sl, n_ch, stride=CMP_STRIDE), :] for sl in range(2)]
                part = jnp.dot(jnp.concatenate(pos, axis=1).astype(BF16), r_ref[c, sp], preferred_element_type=F32)
                acc = part if acc is None else acc + part
            lo, hi = acc[:, :LANES], acc[:, LANES:]
            nxt = pltpu.roll(hi, hi.shape[0] - 1, 0)
            o0 = c * NSA_KV_W + hp * LANES
            o_ref[0, :, o0:o0 + LANES] = lo + nxt + b_ref[:, o0:o0 + LANES]


def nsa_compress_paged(cache_t, page_table, r_w, b_cmp):
    n_phys, _, kvh, dh, page = cache_t.shape
    n, n_pages = page_table.shape
    w = 2 * kvh * dh
    n_ch = n_pages * page // CMP_STRIDE
    bias = jnp.broadcast_to(b_cmp[:, None, :], (2, kvh, dh)).reshape(1, w)
    page_spec = lambda k: pl.BlockSpec((1, 2, kvh, dh, page), lambda b, pt: (pt[b * n_pages + k], 0, 0, 0, 0))
    return pl.pallas_call(
        _cmp_paged_body,
        grid_spec=pltpu.PrefetchScalarGridSpec(
            num_scalar_prefetch=1,
            grid=(n,),
            in_specs=[page_spec(k) for k in range(n_pages)]
            + [pl.BlockSpec(r_w.shape, lambda b, pt: (0, 0, 0, 0)), pl.BlockSpec((1, w), lambda b, pt: (0, 0))],
            out_specs=pl.BlockSpec((1, n_ch, w), lambda b, pt: (b, 0, 0)),
            scratch_shapes=[pltpu.VMEM((4, n_pages * page, LANES), F32)]),
        out_shape=jax.ShapeDtypeStruct((n, n_ch, w), F32),
        compiler_params=_params("arbitrary"),
        name="nsa_compress_paged",
    )(page_table.reshape(-1), *([cache_t] * n_pages), r_w, bias)


def _softmax_cols(s, ok):
    s = jnp.where(ok, s, NEG)
    m = jnp.max(s, axis=0, keepdims=True)
    p = jnp.where(ok, jnp.exp(s - m), 0.0)
    l = jnp.sum(p, axis=0, keepdims=True)
    return p / jnp.where(l > 0.0, l, 1.0)


def _nsa_prompt_body(q_ref, kc_ref, vct_ref, ks_ref, vst_ref, kw_ref, vwt_ref, g_ref, o_ref, imp_ref, bias_ref, s_ref,
                     *, n_cmp, n_sel, i0, n_sel_tiles):
    i = i0 + pl.program_id(2)
    t0 = i * Q_TILE
    gq = NSA_GROUP * Q_TILE
    tq = t0 + lax.broadcasted_iota(I32, (1, Q_TILE), 1)
    tq4 = t0 + (lax.broadcasted_iota(I32, (1, gq), 1) & (Q_TILE - 1))
    q = q_ref[0, 0, 0]

    n_rows = kc_ref.shape[2]
    s = lax.dot_general(kc_ref[0, 0], q, _NT, preferred_element_type=F32)
    n_idx = lax.broadcasted_iota(I32, (n_rows, 1), 0)
    ok_c = (n_idx * CMP_STRIDE + (CMP_LEN - 1) <= tq4) & (n_idx < n_cmp)
    p_c = _softmax_cols(s, ok_c)
    o_c = jnp.dot(vct_ref[0, 0], p_c.astype(BF16), preferred_element_type=F32)

    imp = p_c[:, 0:Q_TILE]
    for g in range(1, NSA_GROUP):
        imp = imp + p_c[:, g * Q_TILE:(g + 1) * Q_TILE]
    imp_ref[...] = imp
    r_sel = SEL_BLOCK // CMP_STRIDE
    n_blk = n_rows // r_sel
    blk_imp = imp_ref[pl.ds(0, n_blk, stride=r_sel), :]
    for r in range(1, r_sel):
        blk_imp = blk_imp + imp_ref[pl.ds(r, n_blk, stride=r_sel), :]
    j_idx = lax.broadcasted_iota(I32, (n_blk, 1), 0)
    score = jnp.where(j_idx == tq // SEL_BLOCK, NSA_GROUP + 2.0,
                      jnp.where(j_idx == 0, NSA_GROUP + 1.0,
                                jnp.where(j_idx * SEL_BLOCK <= tq, blk_imp, -1.0)))
    score = jnp.where(j_idx < n_sel, score, -3.0)
    sel = jnp.zeros((n_blk, Q_TILE), F32)
    for _ in range(min(TOP_N, n_sel)):
        mx = jnp.max(score, axis=0, keepdims=True)
        first = jnp.min(jnp.where(score == mx, j_idx, n_blk), axis=0, keepdims=True)
        pick = j_idx == first
        sel = jnp.where(pick & (mx >= 0.0), 1.0, sel)
        score = jnp.where(pick, -2.0, score)
    blk_per_tile = KV_TILE // SEL_BLOCK
    for j in range(min(n_sel, n_sel_tiles * blk_per_tile)):
        bias_ref[j * SEL_BLOCK:(j + 1) * SEL_BLOCK, :] = jnp.broadcast_to(
            jnp.where(sel[j:j + 1, :] > 0.0, 0.0, NEG), (SEL_BLOCK, Q_TILE))

    def attend(k_ref, vt_ref, tiles, bias_fn, unmasked=()):
        m = jnp.full((1, gq), NEG, F32)
        for n, kt in enumerate(tiles):
            k0 = kt * KV_TILE if isinstance(kt, int) else pl.multiple_of(kt * KV_TILE, KV_TILE)
            s = lax.dot_general(k_ref[0, 0, pl.ds(k0, KV_TILE), :], q, _NT, preferred_element_type=F32)
            kpos = k0 + lax.broadcasted_iota(I32, (KV_TILE, 1), 0)
            if n not in unmasked:
                s = s + jnp.concatenate([bias_fn(kt, k0, kpos)] * NSA_GROUP, axis=1)
            s_ref[n * KV_TILE:(n + 1) * KV_TILE, :] = s
            m = jnp.maximum(m, jnp.max(s, axis=0, keepdims=True))
        acc = jnp.zeros((VT_ROWS, gq), F32)
        for n, kt in enumerate(tiles):
            p = jnp.exp(s_ref[n * KV_TILE:(n + 1) * KV_TILE, :] - m).astype(BF16)
            acc = acc + jnp.dot(vt_ref[0, 0, kt], p, preferred_element_type=F32)
        return acc[0:NSA_HEAD_DIM] / acc[NSA_HEAD_DIM:NSA_HEAD_DIM + 1]

    first_diag = (i0 * Q_TILE) // KV_TILE

    def sel_bias(kt, k0, kpos):
        bias = bias_ref[pl.ds(k0, KV_TILE), :]
        return bias if kt < first_diag else jnp.where(kpos <= tq, bias, NEG)

    def win_bias(kt, k0, kpos):
        dist = tq - kpos
        return jnp.where((dist >= 0) & (dist <= WINDOW), 0.0, NEG)

    o_s = attend(ks_ref, vst_ref, list(range(n_sel_tiles)), sel_bias)
    last = (t0 + Q_TILE - 1) // KV_TILE
    n_win_tiles = WINDOW // KV_TILE + 1
    if (i0 * Q_TILE + Q_TILE - 1) // KV_TILE < n_win_tiles - 1:
        win_tiles, inside = list(range(n_sel_tiles)), ()
    else:
        win_tiles = [last - (n_win_tiles - 1) + r for r in range(n_win_tiles)]
        inside = tuple(range(1, n_win_tiles - 1))
    o_w = attend(kw_ref, vwt_ref, win_tiles, win_bias, inside)

    gate = 1.0 / (1.0 + jnp.exp(-g_ref[0, 0]))
    for g in range(NSA_GROUP):
        sl = slice(g * Q_TILE, (g + 1) * Q_TILE)
        o = gate[g, 0:1, :] * o_c[:, sl] + gate[g, 1:2, :] * o_s[:, sl] + gate[g, 2:3, :] * o_w[:, sl]
        o_ref[0, 0, g] = o.astype(o_ref.dtype)


VT_ROWS = NSA_HEAD_DIM + 16


def _head_major(kv):
    bsz, t, _ = kv.shape
    kv = kv.astype(BF16).reshape(bsz, t, 2, NSA_KV_HEADS, NSA_HEAD_DIM)
    k = kv[:, :, 0].transpose(0, 2, 1, 3)
    vt = kv[:, :, 1].reshape(bsz, t // KV_TILE, KV_TILE, NSA_KV_HEADS, NSA_HEAD_DIM).transpose(0, 3, 1, 4, 2)
    lead = vt.shape[:3]
    ones = jnp.ones(lead + (1, KV_TILE), BF16)
    zeros = jnp.zeros(lead + (VT_ROWS - NSA_HEAD_DIM - 1, KV_TILE), BF16)
    return k, jnp.concatenate([vt, ones, zeros], axis=3)


def nsa_prompt_attention(q, kvc, kv_s, kv_w, gates, n_cmp):
    bsz, t, _ = q.shape
    nq = t // Q_TILE
    n_sel = -(-t // SEL_BLOCK)
    n_rows = kvc.shape[1]
    dh, kvh, grp = NSA_HEAD_DIM, NSA_KV_HEADS, NSA_GROUP
    qh = q.reshape(bsz, nq, Q_TILE, kvh, grp, dh).transpose(0, 3, 1, 4, 2, 5).reshape(bsz, kvh, nq, grp * Q_TILE, dh)
    kc5 = kvc.astype(BF16).reshape(bsz, n_rows, 2, kvh, dh)
    kc = kc5[:, :, 0].transpose(0, 2, 1, 3)
    vct = kc5[:, :, 1].transpose(0, 2, 3, 1)
    ks, vst = _head_major(kv_s)
    kw, vwt = _head_major(kv_w)
    gt = gates[:, :, :NSA_HEADS * 3].reshape(bsz, t, kvh, grp, 3).transpose(0, 2, 3, 4, 1)
    nkt = t // KV_TILE
    assert WINDOW % KV_TILE == 0 and KV_TILE % Q_TILE == 0 and nq % Q_TILES_PER_CALL == 0
    bh = lambda b, h, i: (b, h, 0, 0)
    outs = []
    for i0 in range(0, nq, Q_TILES_PER_CALL):
        n_sel_tiles = ((i0 + Q_TILES_PER_CALL) * Q_TILE - 1) // KV_TILE + 1
        outs.append(pl.pallas_call(
            functools.partial(_nsa_prompt_body, n_cmp=n_cmp, n_sel=n_sel, i0=i0, n_sel_tiles=n_sel_tiles),
            grid=(bsz, kvh, Q_TILES_PER_CALL),
            in_specs=[pl.BlockSpec((1, 1, 1, grp * Q_TILE, dh), lambda b, h, i, i0=i0: (b, h, i0 + i, 0, 0)),
                      pl.BlockSpec((1, 1, n_rows, dh), bh),
                      pl.BlockSpec((1, 1, dh, n_rows), bh),
                      pl.BlockSpec((1, 1, t, dh), bh),
                      pl.BlockSpec((1, 1, nkt, VT_ROWS, KV_TILE), lambda b, h, i: (b, h, 0, 0, 0)),
                      pl.BlockSpec((1, 1, t, dh), bh),
                      pl.BlockSpec((1, 1, nkt, VT_ROWS, KV_TILE), lambda b, h, i: (b, h, 0, 0, 0)),
                      pl.BlockSpec((1, 1, grp, 3, Q_TILE), lambda b, h, i, i0=i0: (b, h, 0, 0, i0 + i))],
            out_specs=pl.BlockSpec((1, 1, grp, dh, Q_TILE), lambda b, h, i: (b, h, 0, 0, i)),
            out_shape=jax.ShapeDtypeStruct((bsz, kvh, grp, dh, Q_TILES_PER_CALL * Q_TILE), BF16),
            scratch_shapes=[pltpu.VMEM((n_rows, Q_TILE), F32), pltpu.VMEM((n_sel * SEL_BLOCK, Q_TILE), F32),
                            pltpu.VMEM((max(n_sel_tiles, WINDOW // KV_TILE + 1) * KV_TILE, grp * Q_TILE), F32)],
            compiler_params=_params("arbitrary", "arbitrary", "arbitrary"),
            name=f"nsa_prompt_attention_q{i0}",
        )(qh, kc, vct, ks, vst, kw, vwt, gt))
    out = jnp.concatenate(outs, axis=-1)
    return out.transpose(0, 4, 1, 2, 3).reshape(bsz, t, kvh * grp * dh)


def _nsa_w_in_padded(w_in):
    wq = w_in[:, :NSA_Q_W] * NSA_HEAD_DIM ** -0.5
    pad = jnp.zeros((w_in.shape[0], LANES - 3 * NSA_HEADS), w_in.dtype)
    return jnp.concatenate([wq, w_in[:, NSA_Q_W:], pad], axis=1).astype(BF16)


_NSA_SPLITS = (NSA_Q_W, 2 * NSA_KV_W, 2 * NSA_KV_W, 2 * NSA_KV_W, LANES)
_NSA_DTYPES = (BF16, F32, F32, F32, F32)


def nsa_prompt_mixer(x, w_in_p, r_cmp, b_cmp):
    bsz, t, d = x.shape
    q, kv_c, kv_s, kv_w, gates = matmul_split(x.reshape(bsz * t, d), w_in_p, _NSA_SPLITS, _NSA_DTYPES, tm=min(512, bsz * t))
    r3 = lambda a: a.reshape(bsz, t, a.shape[-1])
    kv_c, kv_s, kv_w = r3(kv_c), r3(kv_s), r3(kv_w)
    n_cmp = (t - CMP_LEN) // CMP_STRIDE + 1
    kvc = nsa_compress(kv_c, r_cmp, b_cmp)
    o = nsa_prompt_attention(r3(q), kvc, kv_s, kv_w, r3(gates), n_cmp)
    return o.reshape(bsz * t, NSA_Q_W), kv_c, kv_s, kv_w


def _ret_log_decay():
    return jnp.log(1.0 - 2.0 ** (-5.0 - jnp.arange(RET_HEADS, dtype=F32)))


def _rope_tables(pos):
    half = RET_DK // 2
    inv = ROPE_BASE ** (-jnp.linspace(0.0, 1.0, half, dtype=F32))
    ang = pos.astype(F32)[:, None] * inv[None, :]
    return jnp.cos(ang), jnp.sin(ang)


def _rope(x, cos, sin):
    half = RET_DK // 2
    x1, x2 = x[:, :half], x[:, half:]
    return jnp.concatenate([x1 * cos - x2 * sin, x2 * cos + x1 * sin], axis=1)


def _group_norm_gate(o, gate, g, b):
    mu = jnp.mean(o, axis=-1, keepdims=True)
    c = o - mu
    var = jnp.mean(c * c, axis=-1, keepdims=True)
    on = c * lax.rsqrt(var + LN_EPS) * g + b
    return gate * (1.0 / (1.0 + jnp.exp(-gate))) * on


def _ret_prompt_body(q_ref, k_ref, v_ref, gt_ref, cos_ref, sin_ref, dm_ref, di_ref, dr_ref, dc_ref, gg_ref, gb_ref,
                     o_ref, s_ref, s_scr):
    c = pl.program_id(1)

    @pl.when(c == 0)
    def _():
        s_scr[...] = jnp.zeros_like(s_scr)

    cos, sin = cos_ref[...], sin_ref[...]
    for h in range(RET_HEADS):
        qk = slice(h * RET_DK, (h + 1) * RET_DK)
        vv = slice(h * RET_DV, (h + 1) * RET_DV)
        q = _rope(q_ref[:, qk], cos, sin)
        k = _rope(k_ref[:, qk], cos, sin) * RET_DK ** -0.5
        qb, v = q.astype(BF16), v_ref[:, vv]
        inner = lax.dot_general(qb, k.astype(BF16), _NT, preferred_element_type=F32) * dm_ref[h]
        o = jnp.dot(inner.astype(BF16), v, preferred_element_type=F32)
        s_old = s_scr[h]
        o = o + jnp.dot(qb, s_old.astype(BF16), preferred_element_type=F32) * di_ref[h]
        kd = (k * dr_ref[h]).astype(BF16)
        s_new = dc_ref[h] * s_old + lax.dot_general(kd, v, _TN, preferred_element_type=F32)
        s_scr[h] = s_new
        s_ref[0, h] = s_new
        o_ref[:, vv] = _group_norm_gate(o, gt_ref[:, vv], gg_ref[:, vv], gb_ref[:, vv]).astype(o_ref.dtype)


def retention_prompt(q, k, v, gate, gn_g, gn_b, bsz, t):
    ch = RET_CHUNK
    n_ch = t // ch
    lg = _ret_log_decay()
    i = jnp.arange(ch, dtype=F32)
    diff = i[:, None] - i[None, :]
    dmask = jnp.where(diff >= 0, jnp.exp(lg[:, None, None] * jnp.maximum(diff, 0.0)), 0.0)
    d_in = jnp.exp((i[None, :] + 1.0) * lg[:, None])[:, :, None]
    d_rev = jnp.exp((ch - 1.0 - i)[None, :] * lg[:, None])[:, :, None]
    d_c = jnp.exp(ch * lg)[:, None, None]
    cos, sin = _rope_tables(jnp.arange(t))
    half = RET_DK // 2
    nh = RET_HEADS
    rows = lambda w: pl.BlockSpec((ch, w), lambda b, c: (b * n_ch + c, 0))
    whole = lambda a: pl.BlockSpec(a.shape, lambda b, c: (0,) * a.ndim)
    consts = (dmask, d_in, d_rev, d_c, gn_g.reshape(1, -1), gn_b.reshape(1, -1))
    return pl.pallas_call(
        _ret_prompt_body,
        grid=(bsz, n_ch),
        in_specs=[rows(nh * RET_DK), rows(nh * RET_DK), rows(nh * RET_DV), rows(nh * RET_DV),
                  pl.BlockSpec((ch, half), lambda b, c: (c, 0)), pl.BlockSpec((ch, half), lambda b, c: (c, 0))]
        + [whole(a) for a in consts],
        out_specs=[rows(nh * RET_DV), pl.BlockSpec((1, nh, RET_DK, RET_DV), lambda b, c: (b, 0, 0, 0))],
        out_shape=[jax.ShapeDtypeStruct((bsz * t, nh * RET_DV), BF16),
                   jax.ShapeDtypeStruct((bsz, nh, RET_DK, RET_DV), F32)],
        scratch_shapes=[pltpu.VMEM((nh, RET_DK, RET_DV), F32)],
        compiler_params=_params("arbitrary", "arbitrary"),
        name="retention_prompt",
    )(q, k, v, gate, cos, sin, *consts)


_RET_SPLITS = (RET_HEADS * RET_DK, RET_HEADS * RET_DK, RET_HEADS * RET_DV, RET_HEADS * RET_DV)
_RET_DTYPES = (F32, F32, BF16, F32)


def _ret_sample_body(q_ref, k_ref, v_ref, gt_ref, cos_ref, sin_ref, dec_ref, gg_ref, gb_ref, s_in_ref,
                     o_ref, s_out_ref):
    cos, sin = cos_ref[...], sin_ref[...]
    row0 = lax.broadcasted_iota(I32, (8, 1), 0) == 0
    for h in range(RET_HEADS):
        qh = _rope(q_ref[0, :, h * RET_DK:(h + 1) * RET_DK], cos, sin)
        kh = _rope(k_ref[0, :, h * RET_DK:(h + 1) * RET_DK], cos, sin) * RET_DK ** -0.5
        vh = v_ref[0, :, h * RET_DV:(h + 1) * RET_DV]
        qb, kb = qh.astype(BF16), kh.astype(BF16)
        dec = dec_ref[h]
        s_old = s_in_ref[0, h]
        q8 = jnp.broadcast_to(qb, (8, RET_DK))
        cross = jnp.dot(q8, s_old.astype(BF16), preferred_element_type=F32)[0:1] * dec
        inner = jnp.sum(qb.astype(F32) * kb.astype(F32), axis=1, keepdims=True)
        o = inner.astype(BF16).astype(F32) * vh.astype(F32) + cross
        k8 = jnp.where(row0, jnp.broadcast_to(kb.astype(F32), (8, RET_DK)), 0.0).astype(BF16)
        v8 = jnp.broadcast_to(vh, (8, RET_DV))
        s_out_ref[0, h] = dec * s_old + lax.dot_general(k8, v8, _TN, preferred_element_type=F32)
        sl = slice(h * RET_DV, (h + 1) * RET_DV)
        o_ref[0, :, sl] = _group_norm_gate(o, gt_ref[0, :, sl], gg_ref[:, sl], gb_ref[:, sl]).astype(o_ref.dtype)


def retention_sample(q, k, v, gate, gn_g, gn_b, state, pos0):
    n = q.shape[0]
    cos, sin = _rope_tables(jnp.full((1,), pos0))
    dec = jnp.exp(_ret_log_decay())
    r3 = lambda a: a.reshape(n, 1, a.shape[-1])
    w_qk, w_v = RET_HEADS * RET_DK, RET_HEADS * RET_DV
    half = RET_DK // 2
    vec = lambda w: pl.BlockSpec((1, 1, w), lambda b: (b, 0, 0))
    og, s_new = pl.pallas_call(
        _ret_sample_body,
        grid=(n,),
        in_specs=[vec(w_qk), vec(w_qk), vec(w_v), vec(w_v),
                  pl.BlockSpec((1, half), lambda b: (0, 0)), pl.BlockSpec((1, half), lambda b: (0, 0)),
                  pl.BlockSpec(memory_space=pltpu.SMEM),
                  pl.BlockSpec((1, w_v), lambda b: (0, 0)), pl.BlockSpec((1, w_v), lambda b: (0, 0)),
                  pl.BlockSpec((1, RET_HEADS, RET_DK, RET_DV), lambda b: (b, 0, 0, 0))],
        out_specs=[vec(w_v), pl.BlockSpec((1, RET_HEADS, RET_DK, RET_DV), lambda b: (b, 0, 0, 0))],
        out_shape=[jax.ShapeDtypeStruct((n, 1, w_v), BF16), jax.ShapeDtypeStruct(state.shape, F32)],
        compiler_params=_params("arbitrary"),
        name="retention_sample",
    )(r3(q), r3(k), r3(v), r3(gate), cos, sin, dec, gn_g.reshape(1, -1), gn_b.reshape(1, -1), state)
    return og.reshape(n, w_v), s_new


POOL_HALO = 16


def _pool_mix(win_sum_fn, x, pos, w_ref, sc_ref, g_ref, b_ref):
    ys = []
    for gi, w in enumerate(POOL_WINDOWS):
        sl = slice(gi * POOL_GROUP, (gi + 1) * POOL_GROUP)
        pooled = win_sum_fn(gi, w) / jnp.minimum(float(w), pos + 1.0) - x[:, sl]
        ys.append(jnp.dot(pooled.astype(BF16), w_ref[gi], preferred_element_type=F32))
    y = jnp.concatenate(ys, axis=1) * sc_ref[...]
    return _layer_norm(ALPHA * x + y, g_ref[...], b_ref[...])


def _pool_prompt_body(x_ref, w_ref, sc_ref, g_ref, b_ref, o_ref, xh_scr, *, tt):
    j = pl.program_id(1)

    @pl.when(j == 0)
    def _():
        xh_scr[0:POOL_HALO, :] = jnp.zeros((POOL_HALO, D_MODEL), F32)

    x = x_ref[0]
    xh_scr[POOL_HALO:, :] = x
    pos = (j * tt + lax.broadcasted_iota(I32, (tt, 1), 0)).astype(F32)

    def win_sum(gi, w):
        c0 = gi * POOL_GROUP
        acc = x[:, c0:c0 + POOL_GROUP]
        for u in range(1, w):
            acc = acc + xh_scr[POOL_HALO - u:POOL_HALO - u + tt, c0:c0 + POOL_GROUP]
        return acc

    o_ref[0] = _pool_mix(win_sum, x, pos, w_ref, sc_ref, g_ref, b_ref)
    xh_scr[0:POOL_HALO, :] = x[tt - POOL_HALO:, :]


def pool_prompt(x, w_pool, scale, g, b, tt=256):
    bsz, t, d = x.shape
    tt = min(tt, t)
    row = lambda b_, j: (0, 0)
    return pl.pallas_call(
        functools.partial(_pool_prompt_body, tt=tt),
        grid=(bsz, t // tt),
        in_specs=[pl.BlockSpec((1, tt, d), lambda b_, j: (b_, j, 0)),
                  pl.BlockSpec(w_pool.shape, lambda b_, j: (0, 0, 0)),
                  pl.BlockSpec((1, d), row), pl.BlockSpec((1, d), row), pl.BlockSpec((1, d), row)],
        out_specs=pl.BlockSpec((1, tt, d), lambda b_, j: (b_, j, 0)),
        out_shape=jax.ShapeDtypeStruct(x.shape, F32),
        scratch_shapes=[pltpu.VMEM((POOL_HALO + tt, d), F32)],
        compiler_params=_params("arbitrary", "arbitrary"),
        name="pool_prompt",
    )(x, w_pool, scale.reshape(1, d), g.reshape(1, d), b.reshape(1, d))


def _pool_sample_body(x_ref, st_ref, w_ref, sc_ref, g_ref, b_ref, o_ref, *, pos0):
    x = x_ref[...]
    n_hist = st_ref.shape[0]

    def win_sum(gi, w):
        c0 = gi * POOL_GROUP
        acc = x[:, c0:c0 + POOL_GROUP]
        for u in range(1, w):
            acc = acc + st_ref[n_hist - u, :, c0:c0 + POOL_GROUP]
        return acc

    pos = jnp.full((x.shape[0], 1), float(pos0), F32)
    o_ref[...] = _pool_mix(win_sum, x, pos, w_ref, sc_ref, g_ref, b_ref)


def pool_sample(x, hist, w_pool, scale, g, b, pos0):
    n, d = x.shape
    whole = lambda a: pl.BlockSpec(a.shape, lambda i: (0,) * a.ndim)
    args = (x, hist, w_pool, scale.reshape(1, d), g.reshape(1, d), b.reshape(1, d))
    return pl.pallas_call(
        functools.partial(_pool_sample_body, pos0=pos0),
        grid=(1,),
        in_specs=[whole(a) for a in args],
        out_specs=pl.BlockSpec((n, d), lambda i: (0, 0)),
        out_shape=jax.ShapeDtypeStruct((n, d), F32),
        compiler_params=_params("arbitrary"),
        name="pool_sample",
    )(*args)


CONV_HALO = 8


def _conv_prompt_body(bg_ref, cg_ref, h_ref, wc_ref, a_ref, tail_ref, uh_scr, *, tt):
    j = pl.program_id(1)

    @pl.when(j == 0)
    def _():
        uh_scr[0:CONV_HALO, :] = jnp.zeros((CONV_HALO, D_MODEL), F32)

    u = cg_ref[...] * h_ref[...]
    uh_scr[CONV_HALO:, :] = u
    conv = wc_ref[CONV_W - 1:CONV_W, :] * u
    for jj in range(CONV_W - 1):
        back = CONV_W - 1 - jj
        conv = conv + wc_ref[jj:jj + 1, :] * uh_scr[CONV_HALO - back:CONV_HALO - back + tt, :]
    a_ref[...] = (bg_ref[...] * conv).astype(a_ref.dtype)
    uh_scr[0:CONV_HALO, :] = u[tt - CONV_HALO:, :]
    tail_ref[0] = u[tt - CONV_HALO:, :]


def conv_prompt(bg, cg, h, w_conv, bsz, t, tt=256):
    d = bg.shape[1]
    tt = min(tt, t)
    nt = t // tt
    blk = pl.BlockSpec((tt, d), lambda b_, j: (b_ * nt + j, 0))
    return pl.pallas_call(
        functools.partial(_conv_prompt_body, tt=tt),
        grid=(bsz, nt),
        in_specs=[blk, blk, blk, pl.BlockSpec((CONV_W, d), lambda b_, j: (0, 0))],
        out_specs=[blk, pl.BlockSpec((1, CONV_HALO, d), lambda b_, j: (b_, 0, 0))],
        out_shape=[jax.ShapeDtypeStruct((bsz * t, d), BF16), jax.ShapeDtypeStruct((bsz, CONV_HALO, d), F32)],
        scratch_shapes=[pltpu.VMEM((CONV_HALO + tt, d), F32)],
        compiler_params=_params("arbitrary", "arbitrary"),
        name="conv_prompt",
    )(bg, cg, h, w_conv)


def _conv_sample_body(bg_ref, cg_ref, h_ref, prev_ref, wc_ref, a_ref, u_ref):
    u = cg_ref[...] * h_ref[...]
    conv = wc_ref[CONV_W - 1:CONV_W, :] * u
    for jj in range(CONV_W - 1):
        conv = conv + wc_ref[jj:jj + 1, :] * prev_ref[jj]
    a_ref[...] = (bg_ref[...] * conv).astype(a_ref.dtype)
    u_ref[...] = u


def conv_sample(bg, cg, h, prev, w_conv):
    n, d = bg.shape
    whole = lambda a: pl.BlockSpec(a.shape, lambda i: (0,) * a.ndim)
    args = (bg, cg, h, prev, w_conv)
    return pl.pallas_call(
        _conv_sample_body,
        grid=(1,),
        in_specs=[whole(a) for a in args],
        out_specs=[pl.BlockSpec((n, d), lambda i: (0, 0)), pl.BlockSpec((n, d), lambda i: (0, 0))],
        out_shape=[jax.ShapeDtypeStruct((n, d), BF16), jax.ShapeDtypeStruct((n, d), F32)],
        compiler_params=_params("arbitrary"),
        name="conv_sample",
    )(*args)


_CONV_SPLITS = (D_MODEL, D_MODEL, D_MODEL)
_CONV_DTYPES = (F32, F32, F32)


ROUTER_LANES = LANES
EXPERT_LANE0 = MOE_GROUPS


COMB_LANE0 = 8


def _router_body(x_ref, w_ref, b_ref, o_ref, cnt_ref, run_scr):
    i = pl.program_id(0)

    @pl.when(i == 0)
    def _():
        run_scr[...] = jnp.zeros_like(run_scr)

    x = x_ref[...]
    x_hi = x.astype(BF16)
    x_lo = (x - x_hi.astype(F32)).astype(BF16)
    logits = (jnp.dot(x_hi, w_ref[0], preferred_element_type=F32) + jnp.dot(x_lo, w_ref[0], preferred_element_type=F32)
              + jnp.dot(x_hi, w_ref[1], preferred_element_type=F32) + b_ref[...])
    tm = logits.shape[0]
    lane = lax.broadcasted_iota(I32, logits.shape, 1)
    big = ROUTER_LANES

    def top1(mask):
        v = jnp.max(jnp.where(mask, logits, -jnp.inf), axis=1, keepdims=True)
        idx = jnp.min(jnp.where(mask & (logits == v), lane, big), axis=1, keepdims=True)
        return v, idx

    is_g = lane < MOE_GROUPS
    vg, gsel = top1(is_g)
    pg_sel = 1.0 / jnp.sum(jnp.where(is_g, jnp.exp(logits - vg), 0.0), axis=1, keepdims=True)
    e0 = EXPERT_LANE0 + gsel * MOE_EPG
    is_e = (lane >= e0) & (lane < e0 + MOE_EPG)
    v1, i1 = top1(is_e)
    v2, i2 = top1(is_e & (lane != i1))
    r = jnp.exp(v2 - v1)
    pe1 = pg_sel / (1.0 + r)
    pe2 = pg_sel * r / (1.0 + r)
    local = lane - COMB_LANE0
    comb = jnp.where(local == i1 - e0, pe1, 0.0) + jnp.where(local == i2 - e0, pe2, 0.0)

    onehot = jnp.where(lane == gsel, 1.0, 0.0)
    rows = lax.broadcasted_iota(I32, (tm, tm), 0)
    cols = lax.broadcasted_iota(I32, (tm, tm), 1)
    before = jnp.where(cols < rows, 1.0, 0.0).astype(BF16)
    prefix = jnp.dot(before, onehot.astype(BF16), preferred_element_type=F32)
    run = run_scr[0:1, :]
    rank = jnp.sum(jnp.where(lane == gsel, prefix + run, 0.0), axis=1, keepdims=True)
    run = run + jnp.sum(onehot, axis=0, keepdims=True)
    run_scr[0:1, :] = run
    cnt_ref[...] = jnp.broadcast_to(run, cnt_ref.shape)
    o_ref[...] = jnp.where(lane == 0, gsel.astype(F32), jnp.where(lane == 1, rank, comb))


def moe_router(x, w_r, b_r, tm):
    m, d = x.shape
    return pl.pallas_call(
        _router_body,
        grid=(m // tm,),
        in_specs=[pl.BlockSpec((tm, d), lambda i: (i, 0)), pl.BlockSpec((2, d, ROUTER_LANES), lambda i: (0, 0, 0)),
                  pl.BlockSpec((1, ROUTER_LANES), lambda i: (0, 0))],
        out_specs=[pl.BlockSpec((tm, ROUTER_LANES), lambda i: (i, 0)),
                   pl.BlockSpec((8, ROUTER_LANES), lambda i: (0, 0))],
        out_shape=[jax.ShapeDtypeStruct((m, ROUTER_LANES), F32), jax.ShapeDtypeStruct((8, ROUTER_LANES), F32)],
        scratch_shapes=[pltpu.VMEM((8, ROUTER_LANES), F32)],
        compiler_params=_params("arbitrary"),
        name="moe_router",
    )(x, w_r, b_r)


def _router_weights(w_rg, b_rg, w_re, b_re):
    d = w_rg.shape[0]
    pad = ROUTER_LANES - MOE_GROUPS - MOE_EXPERTS
    w = jnp.concatenate([w_rg, w_re, jnp.zeros((d, pad), F32)], axis=1)
    b = jnp.concatenate([b_rg, b_re, jnp.zeros((pad,), F32)]).reshape(1, ROUTER_LANES)
    w_hi = w.astype(BF16)
    w_lo = (w - w_hi.astype(F32)).astype(BF16)
    return jnp.stack([w_hi, w_lo]), b


def _moe_ffn_body(gid_ref, src_ref, x_hbm, c_ref, w1_ref, w3_ref, w2_ref, g_ref, b_ref, o_ref,
                  w1b, w3b, w2b, xbuf, sem_in, *, tm, n_tiles):
    i = pl.program_id(0)
    slot = i % 2
    other = 1 - slot

    def fetch_rows(tile, buf_slot):
        for r in range(tm):
            row = src_ref[tile * tm + r]
            pltpu.make_async_copy(x_hbm.at[pl.ds(row, 1)], xbuf.at[buf_slot, pl.ds(r, 1)],
                                  sem_in.at[buf_slot]).start(priority=r % 2)

    def wait_fetch(buf_slot):
        pltpu.make_async_copy(x_hbm.at[pl.ds(0, tm)], xbuf.at[buf_slot], sem_in.at[buf_slot]).wait()

    @pl.when(i == 0)
    def _():
        fetch_rows(0, 0)

    fetch_rows(jnp.minimum(i + 1, n_tiles - 1), other)
    wait_fetch(slot)

    @pl.when((i == 0) | (gid_ref[i] != gid_ref[jnp.maximum(i - 1, 0)]))
    def _():
        for e in range(MOE_EPG):
            w1b[e] = w1_ref[0, e].astype(BF16)
            w3b[e] = w3_ref[0, e].astype(BF16)
            w2b[e] = w2_ref[0, e].astype(BF16)

    x = xbuf[slot]
    xb = x.astype(BF16)
    comb = c_ref[...]
    y = jnp.zeros(x.shape, F32)
    for e in range(MOE_EPG):
        a = jnp.dot(xb, w1b[e], preferred_element_type=F32)
        gte = jnp.dot(xb, w3b[e], preferred_element_type=F32)
        hcol = a * (1.0 / (1.0 + jnp.exp(-a))) * gte * comb[:, e:e + 1]
        y = y + jnp.dot(hcol.astype(BF16), w2b[e], preferred_element_type=F32)
    o_ref[...] = _layer_norm(ALPHA * x + y, g_ref[...], b_ref[...])

    @pl.when(i == n_tiles - 1)
    def _():
        wait_fetch(other)


def moe_layer(x, router_w, router_b, w1, w3, w2, layer, g, b, tm, router_tm):
    m, d = x.shape
    r, cnt = moe_router(x, router_w, router_b, tm=router_tm)
    gid, rank = r[:, 0].astype(I32), r[:, 1].astype(I32)
    comb = r[:, COMB_LANE0:COMB_LANE0 + MOE_EPG]
    counts = cnt[0, :MOE_GROUPS].astype(I32)
    tiles = (counts + tm - 1) // tm
    tile_end = jnp.cumsum(tiles)
    start = (tile_end - tiles) * tm
    dest = rank
    for grp in range(MOE_GROUPS):
        dest = dest + jnp.where(gid == grp, start[grp], 0)
    n_tiles = -(-m // tm) + MOE_GROUPS
    mp = n_tiles * tm
    src = jnp.full((mp,), -1, I32).at[dest].set(jnp.arange(m, dtype=I32))
    live = (src >= 0).astype(F32)
    src = jnp.maximum(src, 0)
    cs = jnp.take(comb, src, axis=0, mode="clip") * live[:, None]
    tile_gid = jnp.minimum((jnp.arange(n_tiles, dtype=I32)[:, None] >= tile_end[None, :]).sum(axis=1),
                           MOE_GROUPS - 1).astype(I32)
    wspec = lambda shp: pl.BlockSpec((1,) + shp, lambda i, gid_, src_: (layer, gid_[i], 0, 0),
                                     pipeline_mode=pl.Buffered(1))
    const = lambda i, gid_, src_: (0, 0)
    out = pl.pallas_call(
        functools.partial(_moe_ffn_body, tm=tm, n_tiles=n_tiles),
        grid_spec=pltpu.PrefetchScalarGridSpec(
            num_scalar_prefetch=2,
            grid=(n_tiles,),
            in_specs=[pl.BlockSpec(memory_space=pl.ANY),
                      pl.BlockSpec((tm, MOE_EPG), lambda i, gid_, src_: (i, 0)),
                      wspec((MOE_EPG, d, MOE_FF)), wspec((MOE_EPG, d, MOE_FF)), wspec((MOE_EPG, MOE_FF, d)),
                      pl.BlockSpec((1, d), const), pl.BlockSpec((1, d), const)],
            out_specs=pl.BlockSpec((tm, d), lambda i, gid_, src_: (i, 0)),
            scratch_shapes=[pltpu.VMEM((MOE_EPG, d, MOE_FF), BF16), pltpu.VMEM((MOE_EPG, d, MOE_FF), BF16),
                            pltpu.VMEM((MOE_EPG, MOE_FF, d), BF16),
                            pltpu.VMEM((2, tm, d), F32), pltpu.SemaphoreType.DMA((2,))]),
        out_shape=jax.ShapeDtypeStruct((mp, d), F32),
        compiler_params=_params("arbitrary"),
        name="moe_ffn",
    )(tile_gid, src, x, cs, w1, w3, w2, g.reshape(1, d), b.reshape(1, d))
    return out, dest


NSA_ROWS = NSA_HEADS
SEL_SAMPLES = 8


def _softmax_rows(s, ok):
    s = jnp.where(ok, s, NEG)
    m = jnp.max(s, axis=1, keepdims=True)
    p = jnp.where(ok, jnp.exp(s - m), 0.0)
    l = jnp.sum(p, axis=1, keepdims=True)
    return p / jnp.where(l > 0.0, l, 1.0)


def _nsa_sample_select_body(qx_ref, kvc_ref, pool_ref, idx_ref, *, n_cmp, n_sel, q_pos):
    n_rows = kvc_ref.shape[1]
    lane = lax.broadcasted_iota(I32, (NSA_ROWS, n_rows), 1)
    ok_c = (lane * CMP_STRIDE + (CMP_LEN - 1) <= q_pos) & (lane < n_cmp)
    for s_i in range(SEL_SAMPLES):
        kc = kvc_ref[s_i, :, 0:NSA_KV_W].astype(BF16)
        s = lax.dot_general(qx_ref[s_i], kc, _NT, preferred_element_type=F32)
        p = _softmax_rows(s, ok_c)
        imp = p
        for g in range(1, NSA_GROUP):
            imp = imp + pltpu.roll(p, g * NSA_KV_HEADS, 0)
        blk = jnp.dot(imp, pool_ref[...], preferred_element_type=F32, precision=lax.Precision.HIGHEST)
        score = jnp.where(lane == q_pos // SEL_BLOCK, NSA_GROUP + 2.0,
                          jnp.where(lane == 0, NSA_GROUP + 1.0,
                                    jnp.where(lane * SEL_BLOCK <= q_pos, blk, -1.0)))
        score = jnp.where(lane < n_sel, score, -3.0)
        idx = jnp.full(score.shape, -1, I32)
        for k in range(min(TOP_N, n_sel)):
            mx = jnp.max(score, axis=1, keepdims=True)
            first = jnp.min(jnp.where(score == mx, lane, n_rows), axis=1, keepdims=True)
            idx = jnp.where(lane == k, jnp.where(mx >= 0.0, first, -1), idx)
            score = jnp.where(lane == first, -2.0, score)
        idx_ref[s_i] = idx[0:8, :]


def nsa_sample_select(qx, kvc, n_cmp, n_sel, q_pos):
    n, n_rows, w = kvc.shape
    r_sel = SEL_BLOCK // CMP_STRIDE
    pool = (jnp.arange(n_rows)[:, None] // r_sel == jnp.arange(n_rows)[None, :]).astype(F32)
    return pl.pallas_call(
        functools.partial(_nsa_sample_select_body, n_cmp=n_cmp, n_sel=n_sel, q_pos=q_pos),
        grid=(n // SEL_SAMPLES,),
        in_specs=[pl.BlockSpec((SEL_SAMPLES, NSA_ROWS, NSA_KV_W), lambda i: (i, 0, 0)),
                  pl.BlockSpec((SEL_SAMPLES, n_rows, w), lambda i: (i, 0, 0)),
                  pl.BlockSpec((n_rows, n_rows), lambda i: (0, 0))],
        out_specs=pl.BlockSpec((SEL_SAMPLES, 8, n_rows), lambda i: (i, 0, 0)),
        out_shape=jax.ShapeDtypeStruct((n, 8, n_rows), I32),
        compiler_params=_params("arbitrary"),
        name="nsa_sample_select",
    )(qx, kvc, pool)


Q_ROWS = 8


def _nsa_sample_attend_body(page_ref, half_ref, qh_ref, kvc_ref, win_ref, ksn_ref, kwn_ref, g_ref, *rest, n_cmp, q_pos):
    blk_refs, o_ref = rest[:-1], rest[-1]
    b = pl.program_id(0)
    dh = NSA_HEAD_DIM
    rnd = lambda a: a.astype(BF16).astype(F32)
    n_top = len(blk_refs) // NSA_KV_HEADS
    page = blk_refs[0].shape[-1]
    n_rows = kvc_ref.shape[1]
    lane_c = lax.broadcasted_iota(I32, (Q_ROWS, n_rows), 1)
    ok_c = (lane_c * CMP_STRIDE + (CMP_LEN - 1) <= q_pos) & (lane_c < n_cmp)
    lane_s = lax.broadcasted_iota(I32, (Q_ROWS, n_top * page), 1)
    gate = 1.0 / (1.0 + jnp.exp(-g_ref[0]))

    for h in range(NSA_KV_HEADS):
        q = qh_ref[0, h]
        qf = q.astype(F32)

        def with_new_key(k_t, v_t, ok, new_ref):
            s = jnp.dot(q, k_t, preferred_element_type=F32)
            s_new = jnp.sum(qf * rnd(new_ref[0, :, h * dh:(h + 1) * dh]), axis=1, keepdims=True)
            if ok is not None:
                s = jnp.where(ok, s, NEG)
            m = jnp.maximum(jnp.max(s, axis=1, keepdims=True), s_new)
            p = jnp.exp(s - m) if ok is None else jnp.where(ok, jnp.exp(s - m), 0.0)
            p_new = jnp.exp(s_new - m)
            l = jnp.sum(p, axis=1, keepdims=True) + p_new
            v_new = new_ref[0, :, NSA_KV_W + h * dh:NSA_KV_W + (h + 1) * dh]
            o = lax.dot_general(p.astype(BF16), v_t, _NT, preferred_element_type=F32)
            return (o + rnd(p_new) * rnd(v_new)) / l

        kc = kvc_ref[0, :, h * dh:(h + 1) * dh].astype(BF16)
        vc = kvc_ref[0, :, NSA_KV_W + h * dh:NSA_KV_W + (h + 1) * dh].astype(BF16)
        s = lax.dot_general(q, kc, _NT, preferred_element_type=F32)
        o_c = jnp.dot(_softmax_rows(s, ok_c).astype(BF16), vc, preferred_element_type=F32)

        o_w = with_new_key(win_ref[0, 0, h].astype(BF16), win_ref[0, 1, h].astype(BF16), None, kwn_ref)

        refs = blk_refs[h * n_top:(h + 1) * n_top]
        k_t = jnp.concatenate([r[0, 0, 0] for r in refs], axis=1).astype(BF16)
        v_t = jnp.concatenate([r[0, 1, 0] for r in refs], axis=1).astype(BF16)
        want = jnp.full(lane_s.shape, -1, I32)
        for k in range(n_top):
            want = jnp.where(lane_s // page == k, half_ref[(b * NSA_KV_HEADS + h) * n_top + k], want)
        o_s = with_new_key(k_t, v_t, (lane_s % page) // SEL_BLOCK == want, ksn_ref)

        o = gate[h, :, 0:1] * o_c + gate[h, :, 1:2] * o_s + gate[h, :, 2:3] * o_w
        o_ref[0, h] = o.astype(o_ref.dtype)


def nsa_sample_mixer(x, cache_cmp_t, cache_sel_t, win_t, page_table, w_in_p, r_cmp, b_cmp):
    n, d = x.shape
    n_pages = page_table.shape[1]
    past = n_pages * PAGE_SIZE
    dh, kvh, grp = NSA_HEAD_DIM, NSA_KV_HEADS, NSA_GROUP
    q, kv_c, kv_s, kv_w, gates = matmul_split(x, w_in_p, _NSA_SPLITS, _NSA_DTYPES, tm=n)
    n_cmp = (past + 1 - CMP_LEN) // CMP_STRIDE + 1
    n_sel = -(-(past + 1) // SEL_BLOCK)
    n_past_blk = past // SEL_BLOCK
    kvc = nsa_compress_paged(cache_cmp_t, page_table, r_cmp, b_cmp)
    q4 = q.reshape(n, kvh, grp, dh)
    q4t = q4.transpose(0, 2, 1, 3)
    qx = (q4t[:, :, :, None, :] * jnp.eye(kvh, dtype=q.dtype)[None, None, :, :, None]).reshape(n, NSA_ROWS, kvh * dh)
    idx = nsa_sample_select(qx, kvc, n_cmp, n_sel, past)[:, :kvh, :TOP_N]
    sub = PAGE_SIZE // SEL_BLOCK
    is_past = (idx >= 0) & (idx < n_past_blk)
    phys = jnp.take_along_axis(page_table, jnp.clip(idx // sub, 0, n_pages - 1).reshape(n, -1), axis=1).reshape(idx.shape)
    page_idx = jnp.where(is_past, phys, 0).astype(I32).reshape(-1)
    half = jnp.where(is_past, idx % sub, -1).astype(I32).reshape(-1)
    pad_rows = lambda a: jnp.pad(a, ((0, 0), (0, 0), (0, Q_ROWS - grp), (0, 0)))
    qh = pad_rows(q4)
    gt = pad_rows(gates[:, :NSA_HEADS * 3].reshape(n, kvh, grp, 3))
    n_top = idx.shape[2]
    per_seq = kvh * n_top
    vec = lambda a: a.reshape(n, 1, a.shape[-1])
    blk_spec = lambda k: pl.BlockSpec((1, 2, 1, dh, PAGE_SIZE),
                                      lambda b, p_, h_: (p_[b * per_seq + k], 0, k // n_top, 0, 0))
    seq = lambda a: pl.BlockSpec((1,) + a.shape[1:], lambda b, p_, h_: (b,) + (0,) * (a.ndim - 1))
    ins = (qh, kvc, win_t, vec(kv_s), vec(kv_w), gt)
    out = pl.pallas_call(
        functools.partial(_nsa_sample_attend_body, n_cmp=n_cmp, q_pos=past),
        grid_spec=pltpu.PrefetchScalarGridSpec(
            num_scalar_prefetch=2,
            grid=(n,),
            in_specs=[seq(a) for a in ins] + [blk_spec(k) for k in range(per_seq)],
            out_specs=pl.BlockSpec((1, kvh, Q_ROWS, dh), lambda b, p_, h_: (b, 0, 0, 0))),
        out_shape=jax.ShapeDtypeStruct((n, kvh, Q_ROWS, dh), BF16),
        compiler_params=_params("arbitrary"),
        name="nsa_sample_attend",
    )(page_idx, half, *ins, *([cache_sel_t] * per_seq))
    return out[:, :, :grp].reshape(n, NSA_Q_W), kv_c, kv_s, kv_w


PROMPT_TM = 512
ROUTER_TM = 384


def kernel(x_prompt, x_sample, cache_nsa_cmp, cache_nsa_sel, state_nsa_win, state_ret, state_pool, state_conv, page_table, nsa_w_in, nsa_w_cmp, nsa_b_cmp, nsa_w_o, ret_w_in, ret_gn_g, ret_gn_b, ret_w_o, pool_w, pool_scale, conv_w_in, conv_w, conv_w_out, ln_g, ln_b, moe_w_rg, moe_b_rg, moe_w_re, moe_b_re, moe_w1, moe_w3, moe_w2):
    bp, t, d = x_prompt.shape
    ns = x_sample.shape[0]
    assert x_sample.shape[1] == 1 and (bp * t) % PROMPT_TM == 0 and (bp * t + ns) % ROUTER_TM == 0
    past = page_table.shape[1] * PAGE_SIZE
    xp, xs = x_prompt.reshape(bp * t, d), x_sample.reshape(ns, d)
    kv5 = lambda a, n, rows: a.reshape(n, rows, 2, NSA_KV_HEADS, NSA_HEAD_DIM)
    shift_in = lambda old, new: jnp.concatenate([old[:, 1:], new[:, None]], axis=1)
    cmp_p, sel_p, win_p, ret_p, pool_p, conv_p = [], [], [], [], [], []
    cmp_s, sel_s, win_s, ret_s, pool_s, conv_s = [], [], [], [], [], []
    n_mix = 4
    for i in range(DEPTH):
        kind, j = i % n_mix, i // n_mix
        g0, b0 = ln_g[i, 0], ln_b[i, 0]
        if kind == 0:
            w_in_p, r_cmp, w_o = _nsa_w_in_padded(nsa_w_in[j]), _cmp_weights(nsa_w_cmp[j]), nsa_w_o[j].astype(BF16)
            o, a, b, c = nsa_prompt_mixer(xp.reshape(bp, t, d), w_in_p, r_cmp, nsa_b_cmp[j])
            cmp_p.append(kv5(a, bp, t)); sel_p.append(kv5(b, bp, t)); win_p.append(kv5(c, bp, t)[:, -min(WINDOW, t):])
            xp = matmul_res_ln(o, w_o, xp, g0, b0, PROMPT_TM)
            win = state_nsa_win[j]
            rows_minor = lambda a: jnp.transpose(a, (0, 2, 3, 4, 1))
            o, a, b, c = nsa_sample_mixer(xs, rows_minor(cache_nsa_cmp[j]), rows_minor(cache_nsa_sel[j]), rows_minor(win),
                                          page_table, w_in_p, r_cmp, nsa_b_cmp[j])
            cmp_s.append(kv5(a, ns, 1)); sel_s.append(kv5(b, ns, 1)); win_s.append(shift_in(win, kv5(c, ns, 1)[:, 0]))
            xs = matmul_res_ln(o, w_o, xs, g0, b0, ns)
        elif kind == 1:
            w_in, w_o = ret_w_in[j].astype(BF16), ret_w_o[j].astype(BF16)
            q, k, v, g = matmul_split(xp, w_in, _RET_SPLITS, _RET_DTYPES, tm=256)
            og, s_fin = retention_prompt(q, k, v, g, ret_gn_g[j], ret_gn_b[j], bp, t)
            ret_p.append(s_fin)
            xp = matmul_res_ln(og, w_o, xp, g0, b0, PROMPT_TM)
            q, k, v, g = matmul_split(xs, w_in, _RET_SPLITS, _RET_DTYPES, tm=ns)
            og, s_new = retention_sample(q, k, v, g, ret_gn_g[j], ret_gn_b[j], state_ret[j], past)
            ret_s.append(s_new)
            xs = matmul_res_ln(og, w_o, xs, g0, b0, ns)
        elif kind == 2:
            wp = pool_w[j].astype(BF16)
            pool_p.append(xp.reshape(bp, t, d)[:, -(POOL_MAX - 1):])
            xp = pool_prompt(xp.reshape(bp, t, d), wp, pool_scale[j], g0, b0).reshape(bp * t, d)
            pool_s.append(shift_in(state_pool[j], xs))
            xs = pool_sample(xs, state_pool[j].transpose(1, 0, 2), wp, pool_scale[j], g0, b0, past)
        else:
            w_in, w_out = conv_w_in[j].astype(BF16), conv_w_out[j].astype(BF16)
            bg, cg, h = matmul_split(xp, w_in, _CONV_SPLITS, _CONV_DTYPES, tm=PROMPT_TM)
            a, tail = conv_prompt(bg, cg, h, conv_w[j], bp, t)
            conv_p.append(tail[:, -(CONV_W - 1):])
            xp = matmul_res_ln(a, w_out, xp, g0, b0, PROMPT_TM)
            bg, cg, h = matmul_split(xs, w_in, _CONV_SPLITS, _CONV_DTYPES, tm=ns)
            a, u = conv_sample(bg, cg, h, state_conv[j].transpose(1, 0, 2), conv_w[j])
            conv_s.append(shift_in(state_conv[j], u))
            xs = matmul_res_ln(a, w_out, xs, g0, b0, ns)
        rw, rb = _router_weights(moe_w_rg[i], moe_b_rg[i], moe_w_re[i], moe_b_re[i])
        out, dest = moe_layer(jnp.concatenate([xp, xs], axis=0), rw, rb, moe_w1, moe_w3, moe_w2, i,
                              ln_g[i, 1], ln_b[i, 1], tm=PROMPT_TM, router_tm=ROUTER_TM)
        xp = jnp.take(out, dest[:bp * t], axis=0, mode="clip")
        xs = jnp.take(out, dest[bp * t:], axis=0, mode="clip")
    st = jnp.stack
    return (xp.reshape(bp, t, d), xs.reshape(ns, 1, d), st(cmp_p), st(sel_p), st(win_p), st(ret_p), st(pool_p), st(conv_p),
            st(cmp_s), st(sel_s), st(win_s), st(ret_s), st(pool_s), st(conv_s))
```

```python
import functools

import jax
import jax.numpy as jnp
import numpy as np
from jax import lax
from jax.experimental import pallas as pl
from jax.experimental.pallas import tpu as pltpu

F32, BF16, I32 = jnp.float32, jnp.bfloat16, jnp.int32

D_MODEL = 1024
DEPTH = 4
PAGE_SIZE = 128
NSA_HEADS = 16
NSA_HEAD_DIM = 64
NSA_KV_HEADS = 4
NSA_GROUP = NSA_HEADS // NSA_KV_HEADS
NSA_KV_W = NSA_KV_HEADS * NSA_HEAD_DIM
NSA_Q_W = NSA_HEADS * NSA_HEAD_DIM
CMP_LEN = 32
CMP_STRIDE = 16
SEL_BLOCK = 64
TOP_N = 8
WINDOW = 512
RET_HEADS = 4
RET_DK = D_MODEL // RET_HEADS
RET_DV = 2 * D_MODEL // RET_HEADS
RET_CHUNK = 128
ROPE_BASE = 10000.0
POOL_WINDOWS = (2, 4, 8, 16)
POOL_GROUP = D_MODEL // len(POOL_WINDOWS)
POOL_MAX = max(POOL_WINDOWS)
CONV_W = 3
MOE_GROUPS = 4
MOE_EPG = 8
MOE_EXPERTS = MOE_GROUPS * MOE_EPG
MOE_FF = D_MODEL // 4
ALPHA = (2.0 * DEPTH) ** 0.25
LN_EPS = 1e-5
NEG = -1e30

LANES = 128
Q_TILE = 128
KV_TILE = 256
Q_TILES_PER_CALL = 2
VMEM_LIMIT = 56 * 1024 * 1024

_NT = (((1,), (1,)), ((), ()))
_TN = (((0,), (0,)), ((), ()))


def _params(*sem):
    return pltpu.CompilerParams(dimension_semantics=sem, vmem_limit_bytes=VMEM_LIMIT)


def _mm_body(x_ref, w_ref, *o_refs, splits, chunk):
    x = x_ref[...].astype(BF16)
    col = 0
    for o_ref, width in zip(o_refs, splits):
        for j in range(0, width, chunk):
            c = min(chunk, width - j)
            y = jnp.dot(x, w_ref[:, col + j:col + j + c], preferred_element_type=F32)
            o_ref[:, j:j + c] = y.astype(o_ref.dtype)
        col += width


def matmul_split(x, w, splits, dtypes, tm, chunk=512):
    m, k = x.shape
    n = w.shape[1]
    assert n == sum(splits) and m % tm == 0
    return pl.pallas_call(
        functools.partial(_mm_body, splits=tuple(splits), chunk=chunk),
        grid=(m // tm,),
        in_specs=[pl.BlockSpec((tm, k), lambda i: (i, 0)), pl.BlockSpec((k, n), lambda i: (0, 0))],
        out_specs=[pl.BlockSpec((tm, s), lambda i: (i, 0)) for s in splits],
        out_shape=[jax.ShapeDtypeStruct((m, s), d) for s, d in zip(splits, dtypes)],
        compiler_params=_params("arbitrary"),
        name="matmul_split",
    )(x, w)


def _layer_norm(v, g, b):
    mu = jnp.mean(v, axis=-1, keepdims=True)
    c = v - mu
    var = jnp.mean(c * c, axis=-1, keepdims=True)
    return c * lax.rsqrt(var + LN_EPS) * g + b


def _mm_res_ln_body(a_ref, w_ref, x_ref, g_ref, b_ref, o_ref):
    y = jnp.dot(a_ref[...].astype(BF16), w_ref[...], preferred_element_type=F32)
    o_ref[...] = _layer_norm(ALPHA * x_ref[...] + y, g_ref[...], b_ref[...])


def matmul_res_ln(a, w, x, g, b, tm):
    m, k = a.shape
    d = x.shape[1]
    return pl.pallas_call(
        _mm_res_ln_body,
        grid=(m // tm,),
        in_specs=[pl.BlockSpec((tm, k), lambda i: (i, 0)), pl.BlockSpec((k, d), lambda i: (0, 0)),
                  pl.BlockSpec((tm, d), lambda i: (i, 0)), pl.BlockSpec((1, d), lambda i: (0, 0)),
                  pl.BlockSpec((1, d), lambda i: (0, 0))],
        out_specs=pl.BlockSpec((tm, d), lambda i: (i, 0)),
        out_shape=jax.ShapeDtypeStruct((m, d), F32),
        compiler_params=_params("arbitrary"),
        name="matmul_res_ln",
    )(a, w, x, g.reshape(1, d), b.reshape(1, d))


def _cmp_weights(w_cmp):
    dh = NSA_HEAD_DIM
    w6 = w_cmp.reshape(2, 2, CMP_STRIDE // 2, 2, dh, dh)
    eye = jnp.eye(2, dtype=w_cmp.dtype)
    r = jnp.einsum("chpsde,xy->cpsxdhye", w6, eye)
    return r.reshape(2, CMP_STRIDE // 2, 4 * dh, 4 * dh).astype(BF16)


def _cmp_body(*refs):
    x_refs, (r_ref, b_ref, o_ref) = refs[:-3], refs[-3:]
    row_w = 2 * NSA_KV_W

    def cols(c0):
        return jnp.concatenate([x_ref[0, :, c0:c0 + LANES] for x_ref in x_refs], axis=0)

    for c in range(2):
        for hp in range(2):
            acc = None
            for sp in range(CMP_STRIDE // 2):
                c0 = (2 * sp) * row_w + c * NSA_KV_W + hp * LANES
                lhs = jnp.concatenate([cols(c0), cols(c0 + row_w)], axis=1).astype(BF16)
                part = jnp.dot(lhs, r_ref[c, sp], preferred_element_type=F32)
                acc = part if acc is None else acc + part
            lo, hi = acc[:, :LANES], acc[:, LANES:]
            nxt = pltpu.roll(hi, hi.shape[0] - 1, 0)
            o0 = c * NSA_KV_W + hp * LANES
            o_ref[0, :, o0:o0 + LANES] = lo + nxt + b_ref[:, o0:o0 + LANES]


def nsa_compress(rows, r_w, b_cmp):
    bsz, t, w = rows.shape
    n_ch = t // CMP_STRIDE
    x = rows.reshape(bsz, n_ch, CMP_STRIDE * w)
    bias = jnp.broadcast_to(b_cmp[:, None, :], (2, NSA_KV_HEADS, NSA_HEAD_DIM)).reshape(1, w)
    return pl.pallas_call(
        _cmp_body,
        grid=(bsz,),
        in_specs=[pl.BlockSpec((1, n_ch, CMP_STRIDE * w), lambda b: (b, 0, 0)),
                  pl.BlockSpec(r_w.shape, lambda b: (0, 0, 0, 0)),
                  pl.BlockSpec((1, w), lambda b: (0, 0))],
        out_specs=pl.BlockSpec((1, n_ch, w), lambda b: (b, 0, 0)),
        out_shape=jax.ShapeDtypeStruct((bsz, n_ch, w), F32),
        compiler_params=_params("arbitrary"),
        name="nsa_compress",
    )(x, r_w, bias)


def _cmp_paged_body(pt_ref, *refs):
    page_refs, (r_ref, b_ref, o_ref, rows_scr) = refs[:-4], refs[-4:]
    page = page_refs[0].shape[-1]
    for pi, p_ref in enumerate(page_refs):
        for c in range(2):
            for hp in range(2):
                pair = p_ref[0, c, 2 * hp:2 * hp + 2].reshape(2 * NSA_HEAD_DIM, page)
                rows_scr[c * 2 + hp, pi * page:(pi + 1) * page, :] = pair.T
    n_ch = o_ref.shape[1]
    for c in range(2):
        for hp in range(2):
            acc = None
            for sp in range(CMP_STRIDE // 2):
                pos = [rows_scr[c * 2 + hp, pl.ds(2 * sp + sl, n_ch, stride=CMP_STRIDE), :] for sl in range(2)]
                part = jnp.dot(jnp.concatenate(pos, axis=1).astype(BF16), r_ref[c, sp], preferred_element_type=F32)
                acc = part if acc is None else acc + part
            lo, hi = acc[:, :LANES], acc[:, LANES:]
            nxt = pltpu.roll(hi, hi.shape[0] - 1, 0)
            o0 = c * NSA_KV_W + hp * LANES
            o_ref[0, :, o0:o0 + LANES] = lo + nxt + b_ref[:, o0:o0 + LANES]


def nsa_compress_paged(cache_t, page_table, r_w, b_cmp):
    n_phys, _, kvh, dh, page = cache_t.shape
    n, n_pages = page_table.shape
    w = 2 * kvh * dh
    n_ch = n_pages * page // CMP_STRIDE
    bias = jnp.broadcast_to(b_cmp[:, None, :], (2, kvh, dh)).reshape(1, w)
    page_spec = lambda k: pl.BlockSpec((1, 2, kvh, dh, page), lambda b, pt: (pt[b * n_pages + k], 0, 0, 0, 0))
    return pl.pallas_call(
        _cmp_paged_body,
        grid_spec=pltpu.PrefetchScalarGridSpec(
            num_scalar_prefetch=1,
            grid=(n,),
            in_specs=[page_spec(k) for k in range(n_pages)]
            + [pl.BlockSpec(r_w.shape, lambda b, pt: (0, 0, 0, 0)), pl.BlockSpec((1, w), lambda b, pt: (0, 0))],
            out_specs=pl.BlockSpec((1, n_ch, w), lambda b, pt: (b, 0, 0)),
            scratch_shapes=[pltpu.VMEM((4, n_pages * page, LANES), F32)]),
        out_shape=jax.ShapeDtypeStruct((n, n_ch, w), F32),
        compiler_params=_params("arbitrary"),
        name="nsa_compress_paged",
    )(page_table.reshape(-1), *([cache_t] * n_pages), r_w, bias)


def _softmax_cols(s, ok):
    s = jnp.where(ok, s, NEG)
    m = jnp.max(s, axis=0, keepdims=True)
    p = jnp.where(ok, jnp.exp(s - m), 0.0)
    l = jnp.sum(p, axis=0, keepdims=True)
    return p / jnp.where(l > 0.0, l, 1.0)


def _nsa_prompt_body(q_ref, kc_ref, vct_ref, ks_ref, vst_ref, kw_ref, vwt_ref, g_ref, o_ref, imp_ref, bias_ref, s_ref,
                     *, n_cmp, n_sel, i0, n_sel_tiles):
    i = i0 + pl.program_id(2)
    t0 = i * Q_TILE
    gq = NSA_GROUP * Q_TILE
    tq = t0 + lax.broadcasted_iota(I32, (1, Q_TILE), 1)
    tq4 = t0 + (lax.broadcasted_iota(I32, (1, gq), 1) & (Q_TILE - 1))
    q = q_ref[0, 0, 0]

    n_rows = kc_ref.shape[2]
    s = lax.dot_general(kc_ref[0, 0], q, _NT, preferred_element_type=F32)
    n_idx = lax.broadcasted_iota(I32, (n_rows, 1), 0)
    ok_c = (n_idx * CMP_STRIDE + (CMP_LEN - 1) <= tq4) & (n_idx < n_cmp)
    p_c = _softmax_cols(s, ok_c)
    o_c = jnp.dot(vct_ref[0, 0], p_c.astype(BF16), preferred_element_type=F32)

    imp = p_c[:, 0:Q_TILE]
    for g in range(1, NSA_GROUP):
        imp = imp + p_c[:, g * Q_TILE:(g + 1) * Q_TILE]
    imp_ref[...] = imp
    r_sel = SEL_BLOCK // CMP_STRIDE
    n_blk = n_rows // r_sel
    blk_imp = imp_ref[pl.ds(0, n_blk, stride=r_sel), :]
    for r in range(1, r_sel):
        blk_imp = blk_imp + imp_ref[pl.ds(r, n_blk, stride=r_sel), :]
    j_idx = lax.broadcasted_iota(I32, (n_blk, 1), 0)
    score = jnp.where(j_idx == tq // SEL_BLOCK, NSA_GROUP + 2.0,
                      jnp.where(j_idx == 0, NSA_GROUP + 1.0,
                                jnp.where(j_idx * SEL_BLOCK <= tq, blk_imp, -1.0)))
    score = jnp.where(j_idx < n_sel, score, -3.0)
    sel = jnp.zeros((n_blk, Q_TILE), F32)
    for _ in range(min(TOP_N, n_sel)):
        mx = jnp.max(score, axis=0, keepdims=True)
        first = jnp.min(jnp.where(score == mx, j_idx, n_blk), axis=0, keepdims=True)
        pick = j_idx == first
        sel = jnp.where(pick & (mx >= 0.0), 1.0, sel)
        score = jnp.where(pick, -2.0, score)
    blk_per_tile = KV_TILE // SEL_BLOCK
    for j in range(min(n_sel, n_sel_tiles * blk_per_tile)):
        bias_ref[j * SEL_BLOCK:(j + 1) * SEL_BLOCK, :] = jnp.broadcast_to(
            jnp.where(sel[j:j + 1, :] > 0.0, 0.0, NEG), (SEL_BLOCK, Q_TILE))

    def attend(k_ref, vt_ref, tiles, bias_fn, unmasked=()):
        m = jnp.full((1, gq), NEG, F32)
        for n, kt in enumerate(tiles):
            k0 = kt * KV_TILE if isinstance(kt, int) else pl.multiple_of(kt * KV_TILE, KV_TILE)
            s = lax.dot_general(k_ref[0, 0, pl.ds(k0, KV_TILE), :], q, _NT, preferred_element_type=F32)
            kpos = k0 + lax.broadcasted_iota(I32, (KV_TILE, 1), 0)
            if n not in unmasked:
                s = s + jnp.concatenate([bias_fn(kt, k0, kpos)] * NSA_GROUP, axis=1)
            s_ref[n * KV_TILE:(n + 1) * KV_TILE, :] = s
            m = jnp.maximum(m, jnp.max(s, axis=0, keepdims=True))
        acc = jnp.zeros((VT_ROWS, gq), F32)
        for n, kt in enumerate(tiles):
            p = jnp.exp(s_ref[n * KV_TILE:(n + 1) * KV_TILE, :] - m).astype(BF16)
            acc = acc + jnp.dot(vt_ref[0, 0, kt], p, preferred_element_type=F32)
        return acc[0:NSA_HEAD_DIM] / acc[NSA_HEAD_DIM:NSA_HEAD_DIM + 1]

    first_diag = (i0 * Q_TILE) // KV_TILE

    def sel_bias(kt, k0, kpos):
        bias = bias_ref[pl.ds(k0, KV_TILE), :]
        return bias if kt < first_diag else jnp.where(kpos <= tq, bias, NEG)

    def win_bias(kt, k0, kpos):
        dist = tq - kpos
        return jnp.where((dist >= 0) & (dist <= WINDOW), 0.0, NEG)

    o_s = attend(ks_ref, vst_ref, list(range(n_sel_tiles)), sel_bias)
    last = (t0 + Q_TILE - 1) // KV_TILE
    n_win_tiles = WINDOW // KV_TILE + 1
    if (i0 * Q_TILE + Q_TILE - 1) // KV_TILE < n_win_tiles - 1:
        win_tiles, inside = list(range(n_sel_tiles)), ()
    else:
        win_tiles = [last - (n_win_tiles - 1) + r for r in range(n_win_tiles)]
        inside = tuple(range(1, n_win_tiles - 1))
    o_w = attend(kw_ref, vwt_ref, win_tiles, win_bias, inside)

    gate = 1.0 / (1.0 + jnp.exp(-g_ref[0, 0]))
    for g in range(NSA_GROUP):
        sl = slice(g * Q_TILE, (g + 1) * Q_TILE)
        o = gate[g, 0:1, :] * o_c[:, sl] + gate[g, 1:2, :] * o_s[:, sl] + gate[g, 2:3, :] * o_w[:, sl]
        o_ref[0, 0, g] = o.astype(o_ref.dtype)


VT_ROWS = NSA_HEAD_DIM + 16


def _head_major(kv):
    bsz, t, _ = kv.shape
    kv = kv.astype(BF16).reshape(bsz, t, 2, NSA_KV_HEADS, NSA_HEAD_DIM)
    k = kv[:, :, 0].transpose(0, 2, 1, 3)
    vt = kv[:, :, 1].reshape(bsz, t // KV_TILE, KV_TILE, NSA_KV_HEADS, NSA_HEAD_DIM).transpose(0, 3, 1, 4, 2)
    lead = vt.shape[:3]
    ones = jnp.ones(lead + (1, KV_TILE), BF16)
    zeros = jnp.zeros(lead + (VT_ROWS - NSA_HEAD_DIM - 1, KV_TILE), BF16)
    return k, jnp.concatenate([vt, ones, zeros], axis=3)


def nsa_prompt_attention(q, kvc, kv_s, kv_w, gates, n_cmp):
    bsz, t, _ = q.shape
    nq = t // Q_TILE
    n_sel = -(-t // SEL_BLOCK)
    n_rows = kvc.shape[1]
    dh, kvh, grp = NSA_HEAD_DIM, NSA_KV_HEADS, NSA_GROUP
    qh = q.reshape(bsz, nq, Q_TILE, kvh, grp, dh).transpose(0, 3, 1, 4, 2, 5).reshape(bsz, kvh, nq, grp * Q_TILE, dh)
    kc5 = kvc.astype(BF16).reshape(bsz, n_rows, 2, kvh, dh)
    kc = kc5[:, :, 0].transpose(0, 2, 1, 3)
    vct = kc5[:, :, 1].transpose(0, 2, 3, 1)
    ks, vst = _head_major(kv_s)
    kw, vwt = _head_major(kv_w)
    gt = gates[:, :, :NSA_HEADS * 3].reshape(bsz, t, kvh, grp, 3).transpose(0, 2, 3, 4, 1)
    nkt = t // KV_TILE
    assert WINDOW % KV_TILE == 0 and KV_TILE % Q_TILE == 0 and nq % Q_TILES_PER_CALL == 0
    bh = lambda b, h, i: (b, h, 0, 0)
    outs = []
    for i0 in range(0, nq, Q_TILES_PER_CALL):
        n_sel_tiles = ((i0 + Q_TILES_PER_CALL) * Q_TILE - 1) // KV_TILE + 1
        outs.append(pl.pallas_call(
            functools.partial(_nsa_prompt_body, n_cmp=n_cmp, n_sel=n_sel, i0=i0, n_sel_tiles=n_sel_tiles),
            grid=(bsz, kvh, Q_TILES_PER_CALL),
            in_specs=[pl.BlockSpec((1, 1, 1, grp * Q_TILE, dh), lambda b, h, i, i0=i0: (b, h, i0 + i, 0, 0)),
                      pl.BlockSpec((1, 1, n_rows, dh), bh),
                      pl.BlockSpec((1, 1, dh, n_rows), bh),
                      pl.BlockSpec((1, 1, t, dh), bh),
                      pl.BlockSpec((1, 1, nkt, VT_ROWS, KV_TILE), lambda b, h, i: (b, h, 0, 0, 0)),
                      pl.BlockSpec((1, 1, t, dh), bh),
                      pl.BlockSpec((1, 1, nkt, VT_ROWS, KV_TILE), lambda b, h, i: (b, h, 0, 0, 0)),
                      pl.BlockSpec((1, 1, grp, 3, Q_TILE), lambda b, h, i, i0=i0: (b, h, 0, 0, i0 + i))],
            out_specs=pl.BlockSpec((1, 1, grp, dh, Q_TILE), lambda b, h, i: (b, h, 0, 0, i)),
            out_shape=jax.ShapeDtypeStruct((bsz, kvh, grp, dh, Q_TILES_PER_CALL * Q_TILE), BF16),
            scratch_shapes=[pltpu.VMEM((n_rows, Q_TILE), F32), pltpu.VMEM((n_sel * SEL_BLOCK, Q_TILE), F32),
                            pltpu.VMEM((max(n_sel_tiles, WINDOW // KV_TILE + 1) * KV_TILE, grp * Q_TILE), F32)],
            compiler_params=_params("arbitrary", "arbitrary", "arbitrary"),
            name=f"nsa_prompt_attention_q{i0}",
        )(qh, kc, vct, ks, vst, kw, vwt, gt))
    out = jnp.concatenate(outs, axis=-1)
    return out.transpose(0, 4, 1, 2, 3).reshape(bsz, t, kvh * grp * dh)


def _nsa_w_in_padded(w_in):
    wq = w_in[:, :NSA_Q_W] * NSA_HEAD_DIM ** -0.5
    pad = jnp.zeros((w_in.shape[0], LANES - 3 * NSA_HEADS), w_in.dtype)
    return jnp.concatenate([wq, w_in[:, NSA_Q_W:], pad], axis=1).astype(BF16)


_NSA_SPLITS = (NSA_Q_W, 2 * NSA_KV_W, 2 * NSA_KV_W, 2 * NSA_KV_W, LANES)
_NSA_DTYPES = (BF16, F32, F32, F32, F32)


def nsa_prompt_mixer(x, w_in_p, r_cmp, b_cmp):
    bsz, t, d = x.shape
    q, kv_c, kv_s, kv_w, gates = matmul_split(x.reshape(bsz * t, d), w_in_p, _NSA_SPLITS, _NSA_DTYPES, tm=min(512, bsz * t))
    r3 = lambda a: a.reshape(bsz, t, a.shape[-1])
    kv_c, kv_s, kv_w = r3(kv_c), r3(kv_s), r3(kv_w)
    n_cmp = (t - CMP_LEN) // CMP_STRIDE + 1
    kvc = nsa_compress(kv_c, r_cmp, b_cmp)
    o = nsa_prompt_attention(r3(q), kvc, kv_s, kv_w, r3(gates), n_cmp)
    return o.reshape(bsz * t, NSA_Q_W), kv_c, kv_s, kv_w


def _ret_log_decay():
    return jnp.log(1.0 - 2.0 ** (-5.0 - jnp.arange(RET_HEADS, dtype=F32)))


def _rope_tables(pos):
    half = RET_DK // 2
    inv = ROPE_BASE ** (-jnp.linspace(0.0, 1.0, half, dtype=F32))
    ang = pos.astype(F32)[:, None] * inv[None, :]
    return jnp.cos(ang), jnp.sin(ang)


def _rope(x, cos, sin):
    half = RET_DK // 2
    x1, x2 = x[:, :half], x[:, half:]
    return jnp.concatenate([x1 * cos - x2 * sin, x2 * cos + x1 * sin], axis=1)


def _group_norm_gate(o, gate, g, b):
    mu = jnp.mean(o, axis=-1, keepdims=True)
    c = o - mu
    var = jnp.mean(c * c, axis=-1, keepdims=True)
    on = c * lax.rsqrt(var + LN_EPS) * g + b
    return gate * (1.0 / (1.0 + jnp.exp(-gate))) * on


def _ret_prompt_body(q_ref, k_ref, v_ref, gt_ref, cos_ref, sin_ref, dm_ref, di_ref, dr_ref, dc_ref, gg_ref, gb_ref,
                     o_ref, s_ref, s_scr):
    c = pl.program_id(1)

    @pl.when(c == 0)
    def _():
        s_scr[...] = jnp.zeros_like(s_scr)

    cos, sin = cos_ref[...], sin_ref[...]
    for h in range(RET_HEADS):
        qk = slice(h * RET_DK, (h + 1) * RET_DK)
        vv = slice(h * RET_DV, (h + 1) * RET_DV)
        q = _rope(q_ref[:, qk], cos, sin)
        k = _rope(k_ref[:, qk], cos, sin) * RET_DK ** -0.5
        qb, v = q.astype(BF16), v_ref[:, vv]
        inner = lax.dot_general(qb, k.astype(BF16), _NT, preferred_element_type=F32) * dm_ref[h]
        o = jnp.dot(inner.astype(BF16), v, preferred_element_type=F32)
        s_old = s_scr[h]
        o = o + jnp.dot(qb, s_old.astype(BF16), preferred_element_type=F32) * di_ref[h]
        kd = (k * dr_ref[h]).astype(BF16)
        s_new = dc_ref[h] * s_old + lax.dot_general(kd, v, _TN, preferred_element_type=F32)
        s_scr[h] = s_new
        s_ref[0, h] = s_new
        o_ref[:, vv] = _group_norm_gate(o, gt_ref[:, vv], gg_ref[:, vv], gb_ref[:, vv]).astype(o_ref.dtype)


def retention_prompt(q, k, v, gate, gn_g, gn_b, bsz, t):
    ch = RET_CHUNK
    n_ch = t // ch
    lg = _ret_log_decay()
    i = jnp.arange(ch, dtype=F32)
    diff = i[:, None] - i[None, :]
    dmask = jnp.where(diff >= 0, jnp.exp(lg[:, None, None] * jnp.maximum(diff, 0.0)), 0.0)
    d_in = jnp.exp((i[None, :] + 1.0) * lg[:, None])[:, :, None]
    d_rev = jnp.exp((ch - 1.0 - i)[None, :] * lg[:, None])[:, :, None]
    d_c = jnp.exp(ch * lg)[:, None, None]
    cos, sin = _rope_tables(jnp.arange(t))
    half = RET_DK // 2
    nh = RET_HEADS
    rows = lambda w: pl.BlockSpec((ch, w), lambda b, c: (b * n_ch + c, 0))
    whole = lambda a: pl.BlockSpec(a.shape, lambda b, c: (0,) * a.ndim)
    consts = (dmask, d_in, d_rev, d_c, gn_g.reshape(1, -1), gn_b.reshape(1, -1))
    return pl.pallas_call(
        _ret_prompt_body,
        grid=(bsz, n_ch),
        in_specs=[rows(nh * RET_DK), rows(nh * RET_DK), rows(nh * RET_DV), rows(nh * RET_DV),
                  pl.BlockSpec((ch, half), lambda b, c: (c, 0)), pl.BlockSpec((ch, half), lambda b, c: (c, 0))]
        + [whole(a) for a in consts],
        out_specs=[rows(nh * RET_DV), pl.BlockSpec((1, nh, RET_DK, RET_DV), lambda b, c: (b, 0, 0, 0))],
        out_shape=[jax.ShapeDtypeStruct((bsz * t, nh * RET_DV), BF16),
                   jax.ShapeDtypeStruct((bsz, nh, RET_DK, RET_DV), F32)],
        scratch_shapes=[pltpu.VMEM((nh, RET_DK, RET_DV), F32)],
        compiler_params=_params("arbitrary", "arbitrary"),
        name="retention_prompt",
    )(q, k, v, gate, cos, sin, *consts)


_RET_SPLITS = (RET_HEADS * RET_DK, RET_HEADS * RET_DK, RET_HEADS * RET_DV, RET_HEADS * RET_DV)
_RET_DTYPES = (F32, F32, BF16, F32)


def _ret_sample_body(q_ref, k_ref, v_ref, gt_ref, cos_ref, sin_ref, dec_ref, gg_ref, gb_ref, s_in_ref,
                     o_ref, s_out_ref):
    cos, sin = cos_ref[...], sin_ref[...]
    row0 = lax.broadcasted_iota(I32, (8, 1), 0) == 0
    for h in range(RET_HEADS):
        qh = _rope(q_ref[0, :, h * RET_DK:(h + 1) * RET_DK], cos, sin)
        kh = _rope(k_ref[0, :, h * RET_DK:(h + 1) * RET_DK], cos, sin) * RET_DK ** -0.5
        vh = v_ref[0, :, h * RET_DV:(h + 1) * RET_DV]
        qb, kb = qh.astype(BF16), kh.astype(BF16)
        dec = dec_ref[h]
        s_old = s_in_ref[0, h]
        q8 = jnp.broadcast_to(qb, (8, RET_DK))
        cross = jnp.dot(q8, s_old.astype(BF16), preferred_element_type=F32)[0:1] * dec
        inner = jnp.sum(qb.astype(F32) * kb.astype(F32), axis=1, keepdims=True)
        o = inner.astype(BF16).astype(F32) * vh.astype(F32) + cross
        k8 = jnp.where(row0, jnp.broadcast_to(kb.astype(F32), (8, RET_DK)), 0.0).astype(BF16)
        v8 = jnp.broadcast_to(vh, (8, RET_DV))
        s_out_ref[0, h] = dec * s_old + lax.dot_general(k8, v8, _TN, preferred_element_type=F32)
        sl = slice(h * RET_DV, (h + 1) * RET_DV)
        o_ref[0, :, sl] = _group_norm_gate(o, gt_ref[0, :, sl], gg_ref[:, sl], gb_ref[:, sl]).astype(o_ref.dtype)


def retention_sample(q, k, v, gate, gn_g, gn_b, state, pos0):
    n = q.shape[0]
    cos, sin = _rope_tables(jnp.full((1,), pos0))
    dec = jnp.exp(_ret_log_decay())
    r3 = lambda a: a.reshape(n, 1, a.shape[-1])
    w_qk, w_v = RET_HEADS * RET_DK, RET_HEADS * RET_DV
    half = RET_DK // 2
    vec = lambda w: pl.BlockSpec((1, 1, w), lambda b: (b, 0, 0))
    og, s_new = pl.pallas_call(
        _ret_sample_body,
        grid=(n,),
        in_specs=[vec(w_qk), vec(w_qk), vec(w_v), vec(w_v),
                  pl.BlockSpec((1, half), lambda b: (0, 0)), pl.BlockSpec((1, half), lambda b: (0, 0)),
                  pl.BlockSpec(memory_space=pltpu.SMEM),
                  pl.BlockSpec((1, w_v), lambda b: (0, 0)), pl.BlockSpec((1, w_v), lambda b: (0, 0)),
                  pl.BlockSpec((1, RET_HEADS, RET_DK, RET_DV), lambda b: (b, 0, 0, 0))],
        out_specs=[vec(w_v), pl.BlockSpec((1, RET_HEADS, RET_DK, RET_DV), lambda b: (b, 0, 0, 0))],
        out_shape=[jax.ShapeDtypeStruct((n, 1, w_v), BF16), jax.ShapeDtypeStruct(state.shape, F32)],
        compiler_params=_params("arbitrary"),
        name="retention_sample",
    )(r3(q), r3(k), r3(v), r3(gate), cos, sin, dec, gn_g.reshape(1, -1), gn_b.reshape(1, -1), state)
    return og.reshape(n, w_v), s_new


POOL_HALO = 16


def _pool_mix(win_sum_fn, x, pos, w_ref, sc_ref, g_ref, b_ref):
    ys = []
    for gi, w in enumerate(POOL_WINDOWS):
        sl = slice(gi * POOL_GROUP, (gi + 1) * POOL_GROUP)
        pooled = win_sum_fn(gi, w) / jnp.minimum(float(w), pos + 1.0) - x[:, sl]
        ys.append(jnp.dot(pooled.astype(BF16), w_ref[gi], preferred_element_type=F32))
    y = jnp.concatenate(ys, axis=1) * sc_ref[...]
    return _layer_norm(ALPHA * x + y, g_ref[...], b_ref[...])


def _pool_prompt_body(x_ref, w_ref, sc_ref, g_ref, b_ref, o_ref, xh_scr, *, tt):
    j = pl.program_id(1)

    @pl.when(j == 0)
    def _():
        xh_scr[0:POOL_HALO, :] = jnp.zeros((POOL_HALO, D_MODEL), F32)

    x = x_ref[0]
    xh_scr[POOL_HALO:, :] = x
    pos = (j * tt + lax.broadcasted_iota(I32, (tt, 1), 0)).astype(F32)

    def win_sum(gi, w):
        c0 = gi * POOL_GROUP
        acc = x[:, c0:c0 + POOL_GROUP]
        for u in range(1, w):
            acc = acc + xh_scr[POOL_HALO - u:POOL_HALO - u + tt, c0:c0 + POOL_GROUP]
        return acc

    o_ref[0] = _pool_mix(win_sum, x, pos, w_ref, sc_ref, g_ref, b_ref)
    xh_scr[0:POOL_HALO, :] = x[tt - POOL_HALO:, :]


def pool_prompt(x, w_pool, scale, g, b, tt=256):
    bsz, t, d = x.shape
    tt = min(tt, t)
    row = lambda b_, j: (0, 0)
    return pl.pallas_call(
        functools.partial(_pool_prompt_body, tt=tt),
        grid=(bsz, t // tt),
        in_specs=[pl.BlockSpec((1, tt, d), lambda b_, j: (b_, j, 0)),
                  pl.BlockSpec(w_pool.shape, lambda b_, j: (0, 0, 0)),
                  pl.BlockSpec((1, d), row), pl.BlockSpec((1, d), row), pl.BlockSpec((1, d), row)],
        out_specs=pl.BlockSpec((1, tt, d), lambda b_, j: (b_, j, 0)),
        out_shape=jax.ShapeDtypeStruct(x.shape, F32),
        scratch_shapes=[pltpu.VMEM((POOL_HALO + tt, d), F32)],
        compiler_params=_params("arbitrary", "arbitrary"),
        name="pool_prompt",
    )(x, w_pool, scale.reshape(1, d), g.reshape(1, d), b.reshape(1, d))


def _pool_sample_body(x_ref, st_ref, w_ref, sc_ref, g_ref, b_ref, o_ref, *, pos0):
    x = x_ref[...]
    n_hist = st_ref.shape[0]

    def win_sum(gi, w):
        c0 = gi * POOL_GROUP
        acc = x[:, c0:c0 + POOL_GROUP]
        for u in range(1, w):
            acc = acc + st_ref[n_hist - u, :, c0:c0 + POOL_GROUP]
        return acc

    pos = jnp.full((x.shape[0], 1), float(pos0), F32)
    o_ref[...] = _pool_mix(win_sum, x, pos, w_ref, sc_ref, g_ref, b_ref)


def pool_sample(x, hist, w_pool, scale, g, b, pos0):
    n, d = x.shape
    whole = lambda a: pl.BlockSpec(a.shape, lambda i: (0,) * a.ndim)
    args = (x, hist, w_pool, scale.reshape(1, d), g.reshape(1, d), b.reshape(1, d))
    return pl.pallas_call(
        functools.partial(_pool_sample_body, pos0=pos0),
        grid=(1,),
        in_specs=[whole(a) for a in args],
        out_specs=pl.BlockSpec((n, d), lambda i: (0, 0)),
        out_shape=jax.ShapeDtypeStruct((n, d), F32),
        compiler_params=_params("arbitrary"),
        name="pool_sample",
    )(*args)


CONV_HALO = 8


def _conv_prompt_body(bg_ref, cg_ref, h_ref, wc_ref, wo_ref, x_ref, g_ref, b_ref, o_ref, tail_ref, uh_scr, *, tt):
    j = pl.program_id(1)

    @pl.when(j == 0)
    def _():
        uh_scr[0:CONV_HALO, :] = jnp.zeros((CONV_HALO, D_MODEL), F32)

    u = cg_ref[...] * h_ref[...]
    uh_scr[CONV_HALO:, :] = u
    conv = wc_ref[CONV_W - 1:CONV_W, :] * u
    for jj in range(CONV_W - 1):
        back = CONV_W - 1 - jj
        conv = conv + wc_ref[jj:jj + 1, :] * uh_scr[CONV_HALO - back:CONV_HALO - back + tt, :]
    a = (bg_ref[...] * conv).astype(BF16)
    y = jnp.dot(a, wo_ref[...], preferred_element_type=F32)
    o_ref[...] = _layer_norm(ALPHA * x_ref[...] + y, g_ref[...], b_ref[...])
    uh_scr[0:CONV_HALO, :] = u[tt - CONV_HALO:, :]
    tail_ref[0] = u[tt - CONV_HALO:, :]


def conv_prompt(bg, cg, h, w_conv, w_out, x, g, b, bsz, t, tt=256):
    d = bg.shape[1]
    tt = min(tt, t)
    nt = t // tt
    blk = pl.BlockSpec((tt, d), lambda b_, j: (b_ * nt + j, 0))
    const = lambda shp: pl.BlockSpec(shp, lambda b_, j: (0, 0))
    return pl.pallas_call(
        functools.partial(_conv_prompt_body, tt=tt),
        grid=(bsz, nt),
        in_specs=[blk, blk, blk, const((CONV_W, d)), const((d, d)), blk, const((1, d)), const((1, d))],
        out_specs=[blk, pl.BlockSpec((1, CONV_HALO, d), lambda b_, j: (b_, 0, 0))],
        out_shape=[jax.ShapeDtypeStruct((bsz * t, d), F32), jax.ShapeDtypeStruct((bsz, CONV_HALO, d), F32)],
        scratch_shapes=[pltpu.VMEM((CONV_HALO + tt, d), F32)],
        compiler_params=_params("arbitrary", "arbitrary"),
        name="conv_prompt",
    )(bg, cg, h, w_conv, w_out, x, g.reshape(1, d), b.reshape(1, d))


def _conv_sample_body(bg_ref, cg_ref, h_ref, prev_ref, wc_ref, a_ref, u_ref):
    u = cg_ref[...] * h_ref[...]
    conv = wc_ref[CONV_W - 1:CONV_W, :] * u
    for jj in range(CONV_W - 1):
        conv = conv + wc_ref[jj:jj + 1, :] * prev_ref[jj]
    a_ref[...] = (bg_ref[...] * conv).astype(a_ref.dtype)
    u_ref[...] = u


def conv_sample(bg, cg, h, prev, w_conv):
    n, d = bg.shape
    whole = lambda a: pl.BlockSpec(a.shape, lambda i: (0,) * a.ndim)
    args = (bg, cg, h, prev, w_conv)
    return pl.pallas_call(
        _conv_sample_body,
        grid=(1,),
        in_specs=[whole(a) for a in args],
        out_specs=[pl.BlockSpec((n, d), lambda i: (0, 0)), pl.BlockSpec((n, d), lambda i: (0, 0))],
        out_shape=[jax.ShapeDtypeStruct((n, d), BF16), jax.ShapeDtypeStruct((n, d), F32)],
        compiler_params=_params("arbitrary"),
        name="conv_sample",
    )(*args)


_CONV_SPLITS = (D_MODEL, D_MODEL, D_MODEL)
_CONV_DTYPES = (F32, F32, F32)


ROUTER_LANES = LANES
EXPERT_LANE0 = MOE_GROUPS


COMB_LANE0 = 8


def _router_body(x_ref, w_ref, b_ref, o_ref, cnt_ref, run_scr):
    i = pl.program_id(0)

    @pl.when(i == 0)
    def _():
        run_scr[...] = jnp.zeros_like(run_scr)

    x = x_ref[...]
    x_hi = x.astype(BF16)
    x_lo = (x - x_hi.astype(F32)).astype(BF16)
    logits = (jnp.dot(x_hi, w_ref[0], preferred_element_type=F32) + jnp.dot(x_lo, w_ref[0], preferred_element_type=F32)
              + jnp.dot(x_hi, w_ref[1], preferred_element_type=F32) + b_ref[...])
    tm = logits.shape[0]
    lane = lax.broadcasted_iota(I32, logits.shape, 1)
    big = ROUTER_LANES

    def top1(mask):
        v = jnp.max(jnp.where(mask, logits, -jnp.inf), axis=1, keepdims=True)
        idx = jnp.min(jnp.where(mask & (logits == v), lane, big), axis=1, keepdims=True)
        return v, idx

    is_g = lane < MOE_GROUPS
    vg, gsel = top1(is_g)
    pg_sel = 1.0 / jnp.sum(jnp.where(is_g, jnp.exp(logits - vg), 0.0), axis=1, keepdims=True)
    e0 = EXPERT_LANE0 + gsel * MOE_EPG
    is_e = (lane >= e0) & (lane < e0 + MOE_EPG)
    v1, i1 = top1(is_e)
    v2, i2 = top1(is_e & (lane != i1))
    r = jnp.exp(v2 - v1)
    pe1 = pg_sel / (1.0 + r)
    pe2 = pg_sel * r / (1.0 + r)
    local = lane - COMB_LANE0
    comb = jnp.where(local == i1 - e0, pe1, 0.0) + jnp.where(local == i2 - e0, pe2, 0.0)

    onehot = jnp.where(lane == gsel, 1.0, 0.0)
    rows = lax.broadcasted_iota(I32, (tm, tm), 0)
    cols = lax.broadcasted_iota(I32, (tm, tm), 1)
    before = jnp.where(cols < rows, 1.0, 0.0).astype(BF16)
    prefix = jnp.dot(before, onehot.astype(BF16), preferred_element_type=F32)
    run = run_scr[0:1, :]
    rank = jnp.sum(jnp.where(lane == gsel, prefix + run, 0.0), axis=1, keepdims=True)
    run = run + jnp.sum(onehot, axis=0, keepdims=True)
    run_scr[0:1, :] = run
    cnt_ref[...] = jnp.broadcast_to(run, cnt_ref.shape)
    o_ref[...] = jnp.where(lane == 0, gsel.astype(F32), jnp.where(lane == 1, rank, comb))


def moe_router(x, w_r, b_r, tm):
    m, d = x.shape
    return pl.pallas_call(
        _router_body,
        grid=(m // tm,),
        in_specs=[pl.BlockSpec((tm, d), lambda i: (i, 0)), pl.BlockSpec((2, d, ROUTER_LANES), lambda i: (0, 0, 0)),
                  pl.BlockSpec((1, ROUTER_LANES), lambda i: (0, 0))],
        out_specs=[pl.BlockSpec((tm, ROUTER_LANES), lambda i: (i, 0)),
                   pl.BlockSpec((8, ROUTER_LANES), lambda i: (0, 0))],
        out_shape=[jax.ShapeDtypeStruct((m, ROUTER_LANES), F32), jax.ShapeDtypeStruct((8, ROUTER_LANES), F32)],
        scratch_shapes=[pltpu.VMEM((8, ROUTER_LANES), F32)],
        compiler_params=_params("arbitrary"),
        name="moe_router",
    )(x, w_r, b_r)


def _router_weights(w_rg, b_rg, w_re, b_re):
    d = w_rg.shape[0]
    pad = ROUTER_LANES - MOE_GROUPS - MOE_EXPERTS
    w = jnp.concatenate([w_rg, w_re, jnp.zeros((d, pad), F32)], axis=1)
    b = jnp.concatenate([b_rg, b_re, jnp.zeros((pad,), F32)]).reshape(1, ROUTER_LANES)
    w_hi = w.astype(BF16)
    w_lo = (w - w_hi.astype(F32)).astype(BF16)
    return jnp.stack([w_hi, w_lo]), b


def _moe_ffn_body(gid_ref, src_ref, x_hbm, c_ref, w1_ref, w3_ref, w2_ref, g_ref, b_ref, o_ref,
                  w1b, w3b, w2b, xbuf, sem_in, *, tm, n_tiles):
    i = pl.program_id(0)
    slot = i % 2
    other = 1 - slot

    def fetch_rows(tile, buf_slot):
        for r in range(tm):
            row = src_ref[tile * tm + r]
            pltpu.make_async_copy(x_hbm.at[pl.ds(row, 1)], xbuf.at[buf_slot, pl.ds(r, 1)],
                                  sem_in.at[buf_slot]).start(priority=r % 2)

    def wait_fetch(buf_slot):
        pltpu.make_async_copy(x_hbm.at[pl.ds(0, tm)], xbuf.at[buf_slot], sem_in.at[buf_slot]).wait()

    @pl.when(i == 0)
    def _():
        fetch_rows(0, 0)

    fetch_rows(jnp.minimum(i + 1, n_tiles - 1), other)
    wait_fetch(slot)

    @pl.when((i == 0) | (gid_ref[i] != gid_ref[jnp.maximum(i - 1, 0)]))
    def _():
        for e in range(MOE_EPG):
            w1b[e] = w1_ref[0, e].astype(BF16)
            w3b[e] = w3_ref[0, e].astype(BF16)
            w2b[e] = w2_ref[0, e].astype(BF16)

    x = xbuf[slot]
    xb = x.astype(BF16)
    comb = c_ref[...]
    y = jnp.zeros(x.shape, F32)
    for e in range(MOE_EPG):
        a = jnp.dot(xb, w1b[e], preferred_element_type=F32)
        gte = jnp.dot(xb, w3b[e], preferred_element_type=F32)
        hcol = a * (1.0 / (1.0 + jnp.exp(-a))) * gte * comb[:, e:e + 1]
        y = y + jnp.dot(hcol.astype(BF16), w2b[e], preferred_element_type=F32)
    o_ref[...] = _layer_norm(ALPHA * x + y, g_ref[...], b_ref[...])

    @pl.when(i == n_tiles - 1)
    def _():
        wait_fetch(other)


def moe_layer(x, router_w, router_b, w1, w3, w2, layer, g, b, tm, router_tm):
    m, d = x.shape
    r, cnt = moe_router(x, router_w, router_b, tm=router_tm)
    gid, rank = r[:, 0].astype(I32), r[:, 1].astype(I32)
    comb = r[:, COMB_LANE0:COMB_LANE0 + MOE_EPG]
    counts = cnt[0, :MOE_GROUPS].astype(I32)
    tiles = (counts + tm - 1) // tm
    tile_end = jnp.cumsum(tiles)
    start = (tile_end - tiles) * tm
    dest = rank
    for grp in range(MOE_GROUPS):
        dest = dest + jnp.where(gid == grp, start[grp], 0)
    n_tiles = -(-m // tm) + MOE_GROUPS
    mp = n_tiles * tm
    src = jnp.full((mp,), -1, I32).at[dest].set(jnp.arange(m, dtype=I32))
    live = (src >= 0).astype(F32)
    src = jnp.maximum(src, 0)
    cs = jnp.take(comb, src, axis=0, mode="clip") * live[:, None]
    tile_gid = jnp.minimum((jnp.arange(n_tiles, dtype=I32)[:, None] >= tile_end[None, :]).sum(axis=1),
                           MOE_GROUPS - 1).astype(I32)
    wspec = lambda shp: pl.BlockSpec((1,) + shp, lambda i, gid_, src_: (layer, gid_[i], 0, 0),
                                     pipeline_mode=pl.Buffered(1))
    const = lambda i, gid_, src_: (0, 0)
    out = pl.pallas_call(
        functools.partial(_moe_ffn_body, tm=tm, n_tiles=n_tiles),
        grid_spec=pltpu.PrefetchScalarGridSpec(
            num_scalar_prefetch=2,
            grid=(n_tiles,),
            in_specs=[pl.BlockSpec(memory_space=pl.ANY),
                      pl.BlockSpec((tm, MOE_EPG), lambda i, gid_, src_: (i, 0)),
                      wspec((MOE_EPG, d, MOE_FF)), wspec((MOE_EPG, d, MOE_FF)), wspec((MOE_EPG, MOE_FF, d)),
                      pl.BlockSpec((1, d), const), pl.BlockSpec((1, d), const)],
            out_specs=pl.BlockSpec((tm, d), lambda i, gid_, src_: (i, 0)),
            scratch_shapes=[pltpu.VMEM((MOE_EPG, d, MOE_FF), BF16), pltpu.VMEM((MOE_EPG, d, MOE_FF), BF16),
                            pltpu.VMEM((MOE_EPG, MOE_FF, d), BF16),
                            pltpu.VMEM((2, tm, d), F32), pltpu.SemaphoreType.DMA((2,))]),
        out_shape=jax.ShapeDtypeStruct((mp, d), F32),
        compiler_params=_params("arbitrary"),
        name="moe_ffn",
    )(tile_gid, src, x, cs, w1, w3, w2, g.reshape(1, d), b.reshape(1, d))
    return out, dest


NSA_ROWS = NSA_HEADS
SEL_SAMPLES = 8


def _softmax_rows(s, ok):
    s = jnp.where(ok, s, NEG)
    m = jnp.max(s, axis=1, keepdims=True)
    p = jnp.where(ok, jnp.exp(s - m), 0.0)
    l = jnp.sum(p, axis=1, keepdims=True)
    return p / jnp.where(l > 0.0, l, 1.0)


def _nsa_sample_select_body(qx_ref, kvc_ref, pool_ref, idx_ref, *, n_cmp, n_sel, q_pos):
    n_rows = kvc_ref.shape[1]
    lane = lax.broadcasted_iota(I32, (NSA_ROWS, n_rows), 1)
    ok_c = (lane * CMP_STRIDE + (CMP_LEN - 1) <= q_pos) & (lane < n_cmp)
    for s_i in range(SEL_SAMPLES):
        kc = kvc_ref[s_i, :, 0:NSA_KV_W].astype(BF16)
        s = lax.dot_general(qx_ref[s_i], kc, _NT, preferred_element_type=F32)
        p = _softmax_rows(s, ok_c)
        imp = p
        for g in range(1, NSA_GROUP):
            imp = imp + pltpu.roll(p, g * NSA_KV_HEADS, 0)
        blk = jnp.dot(imp, pool_ref[...], preferred_element_type=F32, precision=lax.Precision.HIGHEST)
        score = jnp.where(lane == q_pos // SEL_BLOCK, NSA_GROUP + 2.0,
                          jnp.where(lane == 0, NSA_GROUP + 1.0,
                                    jnp.where(lane * SEL_BLOCK <= q_pos, blk, -1.0)))
        score = jnp.where(lane < n_sel, score, -3.0)
        idx = jnp.full(score.shape, -1, I32)
        for k in range(min(TOP_N, n_sel)):
            mx = jnp.max(score, axis=1, keepdims=True)
            first = jnp.min(jnp.where(score == mx, lane, n_rows), axis=1, keepdims=True)
            idx = jnp.where(lane == k, jnp.where(mx >= 0.0, first, -1), idx)
            score = jnp.where(lane == first, -2.0, score)
        idx_ref[s_i] = idx[0:8, :]


def nsa_sample_select(qx, kvc, n_cmp, n_sel, q_pos):
    n, n_rows, w = kvc.shape
    r_sel = SEL_BLOCK // CMP_STRIDE
    pool = (jnp.arange(n_rows)[:, None] // r_sel == jnp.arange(n_rows)[None, :]).astype(F32)
    return pl.pallas_call(
        functools.partial(_nsa_sample_select_body, n_cmp=n_cmp, n_sel=n_sel, q_pos=q_pos),
        grid=(n // SEL_SAMPLES,),
        in_specs=[pl.BlockSpec((SEL_SAMPLES, NSA_ROWS, NSA_KV_W), lambda i: (i, 0, 0)),
                  pl.BlockSpec((SEL_SAMPLES, n_rows, w), lambda i: (i, 0, 0)),
                  pl.BlockSpec((n_rows, n_rows), lambda i: (0, 0))],
        out_specs=pl.BlockSpec((SEL_SAMPLES, 8, n_rows), lambda i: (i, 0, 0)),
        out_shape=jax.ShapeDtypeStruct((n, 8, n_rows), I32),
        compiler_params=_params("arbitrary"),
        name="nsa_sample_select",
    )(qx, kvc, pool)


Q_ROWS = 8


def _nsa_sample_attend_body(page_ref, half_ref, qh_ref, kvc_ref, win_ref, ksn_ref, kwn_ref, g_ref, *rest, n_cmp, q_pos):
    blk_refs, o_ref = rest[:-1], rest[-1]
    b = pl.program_id(0)
    dh = NSA_HEAD_DIM
    rnd = lambda a: a.astype(BF16).astype(F32)
    n_top = len(blk_refs) // NSA_KV_HEADS
    page = blk_refs[0].shape[-1]
    n_rows = kvc_ref.shape[1]
    lane_c = lax.broadcasted_iota(I32, (Q_ROWS, n_rows), 1)
    ok_c = (lane_c * CMP_STRIDE + (CMP_LEN - 1) <= q_pos) & (lane_c < n_cmp)
    lane_s = lax.broadcasted_iota(I32, (Q_ROWS, n_top * page), 1)
    gate = 1.0 / (1.0 + jnp.exp(-g_ref[0]))

    for h in range(NSA_KV_HEADS):
        q = qh_ref[0, h]
        qf = q.astype(F32)

        def with_new_key(k_t, v_t, ok, new_ref):
            s = jnp.dot(q, k_t, preferred_element_type=F32)
            s_new = jnp.sum(qf * rnd(new_ref[0, :, h * dh:(h + 1) * dh]), axis=1, keepdims=True)
            if ok is not None:
                s = jnp.where(ok, s, NEG)
            m = jnp.maximum(jnp.max(s, axis=1, keepdims=True), s_new)
            p = jnp.exp(s - m) if ok is None else jnp.where(ok, jnp.exp(s - m), 0.0)
            p_new = jnp.exp(s_new - m)
            l = jnp.sum(p, axis=1, keepdims=True) + p_new
            v_new = new_ref[0, :, NSA_KV_W + h * dh:NSA_KV_W + (h + 1) * dh]
            o = lax.dot_general(p.astype(BF16), v_t, _NT, preferred_element_type=F32)
            return (o + rnd(p_new) * rnd(v_new)) / l

        kc = kvc_ref[0, :, h * dh:(h + 1) * dh].astype(BF16)
        vc = kvc_ref[0, :, NSA_KV_W + h * dh:NSA_KV_W + (h + 1) * dh].astype(BF16)
        s = lax.dot_general(q, kc, _NT, preferred_element_type=F32)
        o_c = jnp.dot(_softmax_rows(s, ok_c).astype(BF16), vc, preferred_element_type=F32)

        o_w = with_new_key(win_ref[0, 0, h].astype(BF16), win_ref[0, 1, h].astype(BF16), None, kwn_ref)

        refs = blk_refs[h * n_top:(h + 1) * n_top]
        k_t = jnp.concatenate([r[0, 0, 0] for r in refs], axis=1).astype(BF16)
        v_t = jnp.concatenate([r[0, 1, 0] for r in refs], axis=1).astype(BF16)
        want = jnp.full(lane_s.shape, -1, I32)
        for k in range(n_top):
            want = jnp.where(lane_s // page == k, half_ref[(b * NSA_KV_HEADS + h) * n_top + k], want)
        o_s = with_new_key(k_t, v_t, (lane_s % page) // SEL_BLOCK == want, ksn_ref)

        o = gate[h, :, 0:1] * o_c + gate[h, :, 1:2] * o_s + gate[h, :, 2:3] * o_w
        o_ref[0, h] = o.astype(o_ref.dtype)


def nsa_sample_mixer(x, cache_cmp_t, cache_sel_t, win_t, page_table, w_in_p, r_cmp, b_cmp):
    n, d = x.shape
    n_pages = page_table.shape[1]
    past = n_pages * PAGE_SIZE
    dh, kvh, grp = NSA_HEAD_DIM, NSA_KV_HEADS, NSA_GROUP
    q, kv_c, kv_s, kv_w, gates = matmul_split(x, w_in_p, _NSA_SPLITS, _NSA_DTYPES, tm=n)
    n_cmp = (past + 1 - CMP_LEN) // CMP_STRIDE + 1
    n_sel = -(-(past + 1) // SEL_BLOCK)
    n_past_blk = past // SEL_BLOCK
    kvc = nsa_compress_paged(cache_cmp_t, page_table, r_cmp, b_cmp)
    q4 = q.reshape(n, kvh, grp, dh)
    q4t = q4.transpose(0, 2, 1, 3)
    qx = (q4t[:, :, :, None, :] * jnp.eye(kvh, dtype=q.dtype)[None, None, :, :, None]).reshape(n, NSA_ROWS, kvh * dh)
    idx = nsa_sample_select(qx, kvc, n_cmp, n_sel, past)[:, :kvh, :TOP_N]
    sub = PAGE_SIZE // SEL_BLOCK
    is_past = (idx >= 0) & (idx < n_past_blk)
    phys = jnp.take_along_axis(page_table, jnp.clip(idx // sub, 0, n_pages - 1).reshape(n, -1), axis=1).reshape(idx.shape)
    page_idx = jnp.where(is_past, phys, 0).astype(I32).reshape(-1)
    half = jnp.where(is_past, idx % sub, -1).astype(I32).reshape(-1)
    pad_rows = lambda a: jnp.pad(a, ((0, 0), (0, 0), (0, Q_ROWS - grp), (0, 0)))
    qh = pad_rows(q4)
    gt = pad_rows(gates[:, :NSA_HEADS * 3].reshape(n, kvh, grp, 3))
    n_top = idx.shape[2]
    per_seq = kvh * n_top
    vec = lambda a: a.reshape(n, 1, a.shape[-1])
    blk_spec = lambda k: pl.BlockSpec((1, 2, 1, dh, PAGE_SIZE),
                                      lambda b, p_, h_: (p_[b * per_seq + k], 0, k // n_top, 0, 0))
    seq = lambda a: pl.BlockSpec((1,) + a.shape[1:], lambda b, p_, h_: (b,) + (0,) * (a.ndim - 1))
    ins = (qh, kvc, win_t, vec(kv_s), vec(kv_w), gt)
    out = pl.pallas_call(
        functools.partial(_nsa_sample_attend_body, n_cmp=n_cmp, q_pos=past),
        grid_spec=pltpu.PrefetchScalarGridSpec(
            num_scalar_prefetch=2,
            grid=(n,),
            in_specs=[seq(a) for a in ins] + [blk_spec(k) for k in range(per_seq)],
            out_specs=pl.BlockSpec((1, kvh, Q_ROWS, dh), lambda b, p_, h_: (b, 0, 0, 0))),
        out_shape=jax.ShapeDtypeStruct((n, kvh, Q_ROWS, dh), BF16),
        compiler_params=_params("arbitrary"),
        name="nsa_sample_attend",
    )(page_idx, half, *ins, *([cache_sel_t] * per_seq))
    return out[:, :, :grp].reshape(n, NSA_Q_W), kv_c, kv_s, kv_w


PROMPT_TM = 512
ROUTER_TM = 384


def kernel(x_prompt, x_sample, cache_nsa_cmp, cache_nsa_sel, state_nsa_win, state_ret, state_pool, state_conv, page_table, nsa_w_in, nsa_w_cmp, nsa_b_cmp, nsa_w_o, ret_w_in, ret_gn_g, ret_gn_b, ret_w_o, pool_w, pool_scale, conv_w_in, conv_w, conv_w_out, ln_g, ln_b, moe_w_rg, moe_b_rg, moe_w_re, moe_b_re, moe_w1, moe_w3, moe_w2):
    bp, t, d = x_prompt.shape
    ns = x_sample.shape[0]
    assert x_sample.shape[1] == 1 and (bp * t) % PROMPT_TM == 0 and (bp * t + ns) % ROUTER_TM == 0
    past = page_table.shape[1] * PAGE_SIZE
    xp, xs = x_prompt.reshape(bp * t, d), x_sample.reshape(ns, d)
    kv5 = lambda a, n, rows: a.reshape(n, rows, 2, NSA_KV_HEADS, NSA_HEAD_DIM)
    shift_in = lambda old, new: jnp.concatenate([old[:, 1:], new[:, None]], axis=1)
    cmp_p, sel_p, win_p, ret_p, pool_p, conv_p = [], [], [], [], [], []
    cmp_s, sel_s, win_s, ret_s, pool_s, conv_s = [], [], [], [], [], []
    n_mix = 4
    for i in range(DEPTH):
        kind, j = i % n_mix, i // n_mix
        g0, b0 = ln_g[i, 0], ln_b[i, 0]
        if kind == 0:
            w_in_p, r_cmp, w_o = _nsa_w_in_padded(nsa_w_in[j]), _cmp_weights(nsa_w_cmp[j]), nsa_w_o[j].astype(BF16)
            o, a, b, c = nsa_prompt_mixer(xp.reshape(bp, t, d), w_in_p, r_cmp, nsa_b_cmp[j])
            cmp_p.append(kv5(a, bp, t)); sel_p.append(kv5(b, bp, t)); win_p.append(kv5(c, bp, t)[:, -min(WINDOW, t):])
            xp = matmul_res_ln(o, w_o, xp, g0, b0, PROMPT_TM)
            win = state_nsa_win[j]
            rows_minor = lambda a: jnp.transpose(a, (0, 2, 3, 4, 1))
            o, a, b, c = nsa_sample_mixer(xs, rows_minor(cache_nsa_cmp[j]), rows_minor(cache_nsa_sel[j]), rows_minor(win),
                                          page_table, w_in_p, r_cmp, nsa_b_cmp[j])
            cmp_s.append(kv5(a, ns, 1)); sel_s.append(kv5(b, ns, 1)); win_s.append(shift_in(win, kv5(c, ns, 1)[:, 0]))
            xs = matmul_res_ln(o, w_o, xs, g0, b0, ns)
        elif kind == 1:
            w_in, w_o = ret_w_in[j].astype(BF16), ret_w_o[j].astype(BF16)
            q, k, v, g = matmul_split(xp, w_in, _RET_SPLITS, _RET_DTYPES, tm=256)
            og, s_fin = retention_prompt(q, k, v, g, ret_gn_g[j], ret_gn_b[j], bp, t)
            ret_p.append(s_fin)
            xp = matmul_res_ln(og, w_o, xp, g0, b0, PROMPT_TM)
            q, k, v, g = matmul_split(xs, w_in, _RET_SPLITS, _RET_DTYPES, tm=ns)
            og, s_new = retention_sample(q, k, v, g, ret_gn_g[j], ret_gn_b[j], state_ret[j], past)
            ret_s.append(s_new)
            xs = matmul_res_ln(og, w_o, xs, g0, b0, ns)
        elif kind == 2:
            wp = pool_w[j].astype(BF16)
            pool_p.append(xp.reshape(bp, t, d)[:, -(POOL_MAX - 1):])
            xp = pool_prompt(xp.reshape(bp, t, d), wp, pool_scale[j], g0, b0).reshape(bp * t, d)
            pool_s.append(shift_in(state_pool[j], xs))
            xs = pool_sample(xs, state_pool[j].transpose(1, 0, 2), wp, pool_scale[j], g0, b0, past)
        else:
            w_in, w_out = conv_w_in[j].astype(BF16), conv_w_out[j].astype(BF16)
            bg, cg, h = matmul_split(xp, w_in, _CONV_SPLITS, _CONV_DTYPES, tm=PROMPT_TM)
            xp, tail = conv_prompt(bg, cg, h, conv_w[j], w_out, xp, g0, b0, bp, t, tt=PROMPT_TM)
            conv_p.append(tail[:, -(CONV_W - 1):])
            bg, cg, h = matmul_split(xs, w_in, _CONV_SPLITS, _CONV_DTYPES, tm=ns)
            a, u = conv_sample(bg, cg, h, state_conv[j].transpose(1, 0, 2), conv_w[j])
            conv_s.append(shift_in(state_conv[j], u))
            xs = matmul_res_ln(a, w_out, xs, g0, b0, ns)
        rw, rb = _router_weights(moe_w_rg[i], moe_b_rg[i], moe_w_re[i], moe_b_re[i])
        out, dest = moe_layer(jnp.concatenate([xp, xs], axis=0), rw, rb, moe_w1, moe_w3, moe_w2, i,
                              ln_g[i, 1], ln_b[i, 1], tm=PROMPT_TM, router_tm=ROUTER_TM)
        xp = jnp.take(out, dest[:bp * t], axis=0, mode="clip")
        xs = jnp.take(out, dest[bp * t:], axis=0, mode="clip")
    st = jnp.stack
    return (xp.reshape(bp, t, d), xs.reshape(ns, 1, d), st(cmp_p), st(sel_p), st(win_p), st(ret_p), st(pool_p), st(conv_p),
            st(cmp_s), st(sel_s), st(win_s), st(ret_s), st(pool_s), st(conv_s))
```
